```python
import math
import jax, jax.numpy as jnp
from jax import lax
import numpy as np

D_MODEL = 1024
BATCH = 2
SEQ = 16384
DEPTH = 4

CTX_LEN = 256
GRID_W = 64

ATTN_WIDTH = D_MODEL // 2
RWKV_WIDTH = D_MODEL - ATTN_WIDTH
ATTN_QK_DIM = 64
ATTN_V_DIM = 2 * ATTN_QK_DIM
ATTN_HEADS = ATTN_WIDTH // ATTN_V_DIM
ROPE_THETA = 10000.0
Q_BLOCK = 128
SUBLN_EPS = 1e-5
RWKV_HEAD = 64
RWKV_HEADS = RWKV_WIDTH // RWKV_HEAD
DECAY_LORA = 32
AAA_LORA = 32
MV_LORA = 32
GATE_LORA = 96
LNX_EPS = 64e-5
IN_SIZES = (ATTN_WIDTH, ATTN_WIDTH, ATTN_WIDTH,
            RWKV_WIDTH, RWKV_WIDTH, RWKV_WIDTH,
            2 * DECAY_LORA, 2 * AAA_LORA, GATE_LORA)
N_IN = sum(IN_SIZES)
D_FF = 2816
N_EXPERTS = 8
TOP_K = 2
D_FF_EXPERT = 3584
NORM_EPS = 1e-6

kernel_name = "hybrid_diffattn_rwkv7_moe_dit"


def _rms_norm(x, g, eps=NORM_EPS):
    xf = x.astype(jnp.float32)
    y = xf * lax.rsqrt(jnp.mean(xf * xf, axis=-1, keepdims=True) + eps)
    return y.astype(x.dtype) * g


def _split_in(p):
    return jnp.split(p, np.cumsum(IN_SIZES)[:-1].tolist(), axis=-1)


def _axial_rope_tables(n_tokens, dtype):
    rows = n_tokens // GRID_W
    row = jnp.repeat(jnp.arange(rows), GRID_W).astype(jnp.float32)
    col = jnp.tile(jnp.arange(GRID_W), rows).astype(jnp.float32)
    axis_dim = ATTN_QK_DIM // 2
    inv_freq = ROPE_THETA ** (-jnp.arange(0, axis_dim, 2, dtype=jnp.float32) / axis_dim)
    tabs = []
    for pos in (row, col):
        ang = (pos[:, None] * inv_freq[None, :])[None, :, None, None, :]
        tabs += [jnp.cos(ang).astype(dtype), jnp.sin(ang).astype(dtype)]
    return tuple(tabs)


def _rotate(x, cos, sin):
    h = x.shape[-1] // 2
    x1, x2 = x[..., :h], x[..., h:]
    return jnp.concatenate([x1 * cos - x2 * sin, x1 * sin + x2 * cos], axis=-1)


def _apply_axial_rope(x, rope):
    cos_r, sin_r, cos_c, sin_c = rope
    a = ATTN_QK_DIM // 2
    return jnp.concatenate([_rotate(x[..., :a], cos_r, sin_r),
                            _rotate(x[..., a:], cos_c, sin_c)], axis=-1)


def _diff_mix(q, keys, vals, lam):
    s = jnp.einsum("bqhmd,bkhmd->bhmqk", q, keys, preferred_element_type=jnp.float32)
    p = jax.nn.softmax(s * (ATTN_QK_DIM ** -0.5), axis=-1)
    w = p[:, :, 0] - lam * p[:, :, 1]
    return jnp.einsum("bhqk,bkhd->bqhd", w.astype(vals.dtype), vals)


def _subln(o, g, lam_init):
    b, t = o.shape[:2]
    return (_rms_norm(o, g, SUBLN_EPS) * (1.0 - lam_init)).reshape(b, t, ATTN_WIDTH)


def _diff_attention_group(pq, pk, pv, cq, ck, cv, rope, lam_p, subln_g, lam_init, need_ctx):
    b, t, _ = pq.shape

    def qk_heads(z):
        return z.reshape(z.shape[0], z.shape[1], ATTN_HEADS, 2, ATTN_QK_DIM)

    def v_heads(z):
        return z.reshape(z.shape[0], z.shape[1], ATTN_HEADS, ATTN_V_DIM)

    q = _apply_axial_rope(qk_heads(pq), rope)
    k = _apply_axial_rope(qk_heads(pk), rope)
    qc, kc, vc = qk_heads(cq), qk_heads(ck), v_heads(cv)
    lp = lam_p.astype(jnp.float32)
    lam = jnp.exp(jnp.sum(lp[0] * lp[1])) - jnp.exp(jnp.sum(lp[2] * lp[3])) + lam_init
    keys = jnp.concatenate([kc, k], axis=1)
    vals = jnp.concatenate([vc, v_heads(pv)], axis=1)
    n_blk = t // Q_BLOCK
    qb = jnp.moveaxis(q.reshape(b, n_blk, Q_BLOCK, ATTN_HEADS, 2, ATTN_QK_DIM), 1, 0)
    ob = lax.map(lambda blk: _diff_mix(blk, keys, vals, lam), qb)
    o_lat = jnp.moveaxis(ob, 0, 1).reshape(b, t, ATTN_HEADS, ATTN_V_DIM)
    out_lat = _subln(o_lat, subln_g, lam_init)
    out_ctx = _subln(_diff_mix(qc, kc, vc, lam), subln_g, lam_init) if need_ctx else None
    return out_lat, out_ctx


def _shift_mix(p, mu_prev, mu_next):
    z = jnp.zeros_like(p[:, :1])
    p_prev = jnp.concatenate([z, p[:, :-1]], axis=1)
    p_next = jnp.concatenate([p[:, 1:], z], axis=1)
    return p + mu_prev * (p_prev - p) + mu_next * (p_next - p)


def _heads(z):
    return z.reshape(z.shape[0], z.shape[1], RWKV_HEADS, RWKV_HEAD).astype(jnp.float32)


def _wkv_scan(s0, r, w, k, v, kk, a, reverse):
    def step(s, inp):
        r_t, w_t, k_t, v_t, kk_t, a_t = inp
        sa = jnp.einsum("bhvk,bhk->bhv", s, -kk_t)
        s = (s * w_t[:, :, None, :] + sa[..., None] * (kk_t * a_t)[:, :, None, :]
             + v_t[..., None] * k_t[:, :, None, :])
        return s, jnp.einsum("bhvk,bhk->bhv", s, r_t)

    xs = tuple(jnp.moveaxis(z, 1, 0) for z in (r, w, k, v, kk, a))
    s_fin, ys = lax.scan(step, s0, xs, reverse=reverse)
    return jnp.moveaxis(ys, 0, 1), s_fin


def _rwkv_prepare(seg, mu, w0, w2, a0, a2, g2, k_k, k_a, vres):
    r, k, v, wd, ad, gd = seg
    r = _shift_mix(r, mu[0, 0], mu[0, 1])
    k = _shift_mix(k, mu[1, 0], mu[1, 1])
    v = _shift_mix(v, mu[2, 0], mu[2, 1])
    if vres is not None:
        v_first, v0, v1, v2 = vres
        v = v + (v_first - v) * jax.nn.sigmoid(v0 + (v @ v1) @ v2)
    g = jax.nn.sigmoid(gd) @ g2
    kk = _heads(k * k_k)
    kk = kk * lax.rsqrt(jnp.maximum(jnp.sum(kk * kk, axis=-1, keepdims=True), 1e-24))
    kf = k.astype(jnp.float32)
    dirs = []
    for d in range(2):
        wl = (w0[d] + jnp.tanh(wd[..., d * DECAY_LORA:(d + 1) * DECAY_LORA]) @ w2[d]).astype(jnp.float32)
        decay = jnp.exp(-jnp.exp(-jax.nn.softplus(-wl) - 0.5))
        a = jax.nn.sigmoid((a0[d] + ad[..., d * AAA_LORA:(d + 1) * AAA_LORA] @ a2[d]).astype(jnp.float32))
        kd = kf * (1.0 + (a - 1.0) * k_a.astype(jnp.float32))
        dirs.append((_heads(decay), _heads(a), _heads(kd)))
    return {"r": _heads(r), "v": _heads(v), "g": g, "kk": kk, "dirs": dirs, "v_raw": v}


def _rwkv_finish(y, p, r_k, ln_w, ln_b):
    mean = jnp.mean(y, axis=-1, keepdims=True)
    var = jnp.mean(jnp.square(y - mean), axis=-1, keepdims=True)
    o = (y - mean) * lax.rsqrt(var + LNX_EPS)
    o = (o * ln_w.astype(jnp.float32).reshape(RWKV_HEADS, RWKV_HEAD)
         + ln_b.astype(jnp.float32).reshape(RWKV_HEADS, RWKV_HEAD))
    rkh = r_k.astype(jnp.float32).reshape(RWKV_HEADS, RWKV_HEAD)
    for _, _, kd in p["dirs"]:
        o = o + jnp.sum(p["r"] * kd * rkh, axis=-1, keepdims=True) * p["v"]
    b, t = o.shape[:2]
    return o.reshape(b, t, RWKV_WIDTH).astype(p["g"].dtype) * p["g"]


def _rwkv_group(seg_lat, seg_ctx, mu, w0, w2, a0, a2, g2, k_k, k_a, r_k, ln_w, ln_b, vres, need_ctx):
    if vres is None:
        vres_l = vres_c = None
    else:
        vf_l, vf_c, v0, v1, v2 = vres
        vres_l, vres_c = (vf_l, v0, v1, v2), (vf_c, v0, v1, v2)
    pl = _rwkv_prepare(seg_lat, mu, w0, w2, a0, a2, g2, k_k, k_a, vres_l)
    pc = _rwkv_prepare(seg_ctx, mu, w0, w2, a0, a2, g2, k_k, k_a, vres_c)
    b = seg_lat[0].shape[0]
    ys_l, ys_c = [], []
    for d, rev in enumerate((False, True)):
        s0 = jnp.zeros((b, RWKV_HEADS, RWKV_HEAD, RWKV_HEAD), jnp.float32)
        dc, dl = pc["dirs"][d], pl["dirs"][d]
        yc, s_ctx = _wkv_scan(s0, pc["r"], dc[0], dc[2], pc["v"], pc["kk"], dc[1], rev)
        yl, _ = _wkv_scan(s_ctx, pl["r"], dl[0], dl[2], pl["v"], pl["kk"], dl[1], rev)
        ys_l.append(yl)
        ys_c.append(yc)
    out_lat = _rwkv_finish(ys_l[0] + ys_l[1], pl, r_k, ln_w, ln_b)
    out_ctx = _rwkv_finish(ys_c[0] + ys_c[1], pc, r_k, ln_w, ln_b) if need_ctx else None
    return out_lat, out_ctx, (pl["v_raw"], pc["v_raw"])


def _swiglu(h, w1, w3, w2):
    return (jax.nn.silu(h @ w1) * (h @ w3)) @ w2


def _moe_swiglu(h, router_w, w1, w3, w2):
    shp = h.shape
    hf = h.reshape(-1, shp[-1])
    logits = jnp.matmul(hf, router_w, preferred_element_type=jnp.float32)
    top_v, top_i = lax.top_k(logits, TOP_K)
    gates = jax.nn.softmax(top_v, axis=-1)
    comb = jnp.sum(jax.nn.one_hot(top_i, N_EXPERTS, dtype=jnp.float32) * gates[..., None], axis=1)
    comb = comb.astype(h.dtype)
    y = jnp.zeros_like(hf)
    for e in range(N_EXPERTS):
        y = y + comb[:, e:e + 1] * _swiglu(hf, w1[e], w3[e], w2[e])
    return y.reshape(shp)


def _channel_mixer(h, l, ffn_w1, ffn_w3, ffn_w2, router_w, exp_w1, exp_w3, exp_w2):
    i = l // 2
    if l % 2 == 0:
        return _swiglu(h, ffn_w1[i], ffn_w3[i], ffn_w2[i])
    return _moe_swiglu(h, router_w[i], exp_w1[i], exp_w3[i], exp_w2[i])


def setup_inputs(seed: int = 0) -> dict:
    key = jax.random.key(seed)
    keys = iter(jax.random.split(key, 48))
    f32 = jnp.float32
    n_dense = (DEPTH + 1) // 2
    n_moe = DEPTH // 2
    D = D_MODEL

    def normal(shape, scale):
        return jax.random.normal(next(keys), shape, f32) * scale

    def uniform(shape, lo, hi):
        return jax.random.uniform(next(keys), shape, f32, lo, hi)

    return {
        "x": normal((BATCH, SEQ, D), 1.0),
        "c": normal((BATCH, D), 1.0),
        "ctx": normal((BATCH, CTX_LEN, D), 1.0),
        "c_ctx": normal((D,), 1.0),
        "ada_w": normal((DEPTH, D, 6 * D), D ** -0.5),
        "ada_b": normal((DEPTH, 6 * D), 0.01),
        "norm1_g": 1.0 + normal((DEPTH, D), 0.02),
        "norm2_g": 1.0 + normal((DEPTH, D), 0.02),
        "w_in": normal((DEPTH, D, N_IN), D ** -0.5),
        "w_out": normal((DEPTH, D, D), D ** -0.5),
        "diff_lambda": normal((DEPTH, 4, ATTN_QK_DIM), 0.1),
        "subln_g": 1.0 + normal((DEPTH, ATTN_V_DIM), 0.02),
        "rwkv_mu": uniform((DEPTH, 3, 2, RWKV_WIDTH), 0.0, 0.5),
        "rwkv_w0": uniform((DEPTH, 2, RWKV_WIDTH), -6.0, -1.0),
        "rwkv_w2": normal((DEPTH, 2, DECAY_LORA, RWKV_WIDTH), 0.1),
        "rwkv_a0": normal((DEPTH, 2, RWKV_WIDTH), 0.1),
        "rwkv_a2": normal((DEPTH, 2, AAA_LORA, RWKV_WIDTH), 0.1),
        "rwkv_v0": 1.0 + normal((DEPTH - 1, RWKV_WIDTH), 0.1),
        "rwkv_v1": normal((DEPTH - 1, RWKV_WIDTH, MV_LORA), RWKV_WIDTH ** -0.5),
        "rwkv_v2": normal((DEPTH - 1, MV_LORA, RWKV_WIDTH), 0.1),
        "rwkv_g2": normal((DEPTH, GATE_LORA, RWKV_WIDTH), GATE_LORA ** -0.5),
        "rwkv_k_k": 0.85 + normal((DEPTH, RWKV_WIDTH), 0.02),
        "rwkv_k_a": 1.0 + normal((DEPTH, RWKV_WIDTH), 0.02),
        "rwkv_r_k": normal((DEPTH, RWKV_WIDTH), 0.1),
        "lnx_w": 1.0 + normal((DEPTH, RWKV_WIDTH), 0.02),
        "lnx_b": normal((DEPTH, RWKV_WIDTH), 0.01),
        "ffn_w1": normal((n_dense, D, D_FF), D ** -0.5),
        "ffn_w3": normal((n_dense, D, D_FF), D ** -0.5),
        "ffn_w2": normal((n_dense, D_FF, D), D_FF ** -0.5),
        "router_w": normal((n_moe, D, N_EXPERTS), D ** -0.5),
        "exp_w1": normal((n_moe, N_EXPERTS, D, D_FF_EXPERT), D ** -0.5),
        "exp_w3": normal((n_moe, N_EXPERTS, D, D_FF_EXPERT), D ** -0.5),
        "exp_w2": normal((n_moe, N_EXPERTS, D_FF_EXPERT, D), D_FF_EXPERT ** -0.5),
        "final_g": 1.0 + normal((D,), 0.02),
    }


def reference(x, c, ctx, c_ctx, ada_w, ada_b, norm1_g, norm2_g, w_in, w_out,
              diff_lambda, subln_g, rwkv_mu, rwkv_w0, rwkv_w2, rwkv_a0, rwkv_a2,
              rwkv_v0, rwkv_v1, rwkv_v2, rwkv_g2, rwkv_k_k, rwkv_k_a, rwkv_r_k,
              lnx_w, lnx_b, ffn_w1, ffn_w3, ffn_w2, router_w, exp_w1, exp_w3, exp_w2,
              final_g):
    rope = _axial_rope_tables(x.shape[1], x.dtype)
    xc = ctx
    v_first = None
    for l in range(DEPTH):
        need_ctx = l < DEPTH - 1
        lam_init = 0.8 - 0.6 * math.exp(-0.3 * l)
        mod = jax.nn.silu(c) @ ada_w[l] + ada_b[l]
        mod_c = jax.nn.silu(c_ctx) @ ada_w[l] + ada_b[l]
        sh1, sc1, gt1, sh2, sc2, gt2 = jnp.split(mod[:, None, :], 6, axis=-1)
        csh1, csc1, cgt1, csh2, csc2, cgt2 = jnp.split(mod_c[None, None, :], 6, axis=-1)

        h = _rms_norm(x, norm1_g[l]) * (1.0 + sc1) + sh1
        hc = _rms_norm(xc, norm1_g[l]) * (1.0 + csc1) + csh1
        pl = _split_in(h @ w_in[l])
        pc = _split_in(hc @ w_in[l])
        att_l, att_c = _diff_attention_group(pl[0], pl[1], pl[2], pc[0], pc[1], pc[2], rope,
                                             diff_lambda[l], subln_g[l], lam_init, need_ctx)
        vres = None if l == 0 else (v_first[0], v_first[1], rwkv_v0[l - 1], rwkv_v1[l - 1], rwkv_v2[l - 1])
        rw_l, rw_c, v_raw = _rwkv_group(tuple(pl[3:]), tuple(pc[3:]), rwkv_mu[l], rwkv_w0[l], rwkv_w2[l],
                                        rwkv_a0[l], rwkv_a2[l], rwkv_g2[l], rwkv_k_k[l], rwkv_k_a[l],
                                        rwkv_r_k[l], lnx_w[l], lnx_b[l], vres, need_ctx)
        if l == 0:
            v_first = v_raw
        x = x + gt1 * (jnp.concatenate([att_l, rw_l], axis=-1) @ w_out[l])

        h = _rms_norm(x, norm2_g[l]) * (1.0 + sc2) + sh2
        x = x + gt2 * _channel_mixer(h, l, ffn_w1, ffn_w3, ffn_w2, router_w, exp_w1, exp_w3, exp_w2)

        if need_ctx:
            xc = xc + cgt1 * (jnp.concatenate([att_c, rw_c], axis=-1) @ w_out[l])
            hc = _rms_norm(xc, norm2_g[l]) * (1.0 + csc2) + csh2
            xc = xc + cgt2 * _channel_mixer(hc, l, ffn_w1, ffn_w3, ffn_w2, router_w, exp_w1, exp_w3, exp_w2)
    return _rms_norm(x, final_g)
```

```python
import functools
import math

import numpy as np
import jax
import jax.numpy as jnp
from jax import lax
from jax.experimental import pallas as pl
from jax.experimental.pallas import tpu as pltpu

F32 = jnp.float32
BF16 = jnp.bfloat16

ATTN_HEADS = 4
ATTN_QK_DIM = 64
ATTN_V_DIM = 128
ATTN_WIDTH = ATTN_HEADS * ATTN_V_DIM
RWKV_HEADS = 8
RWKV_HEAD = 64
RWKV_WIDTH = RWKV_HEADS * RWKV_HEAD
DECAY_LORA = 32
AAA_LORA = 32
GATE_LORA = 96
LORA_PAD = 256
N_EXPERTS = 8
GRID_W = 64
ROPE_THETA = 10000.0
NORM_EPS = 1e-6
SUBLN_EPS = 1e-5
LNX_EPS = 64e-5
LANES = 128
CHUNK = 64
INV_BLOCK = 16
VMEM_LIMIT = 56 * 1024 * 1024


def _cparams(sem, vmem=VMEM_LIMIT):
    return pltpu.CompilerParams(dimension_semantics=sem, vmem_limit_bytes=vmem)


def _bdot(a, b):
    return jnp.dot(a.astype(BF16), b.astype(BF16), preferred_element_type=F32)


def _bdot_nt(a, b):
    return lax.dot_general(a.astype(BF16), b.astype(BF16), (((1,), (1,)), ((), ())),
                           preferred_element_type=F32)


def _bdot_tn(a, b):
    return lax.dot_general(a.astype(BF16), b.astype(BF16), (((0,), (0,)), ((), ())),
                           preferred_element_type=F32)


def _split(a):
    hi = a.astype(BF16)
    lo = (a - hi.astype(F32)).astype(BF16)
    return hi, lo


def _dot3(a, b):
    ah, al = _split(a)
    bh, bl = _split(b)
    return (jnp.dot(ah, bh, preferred_element_type=F32) + jnp.dot(al, bh, preferred_element_type=F32)
            + jnp.dot(ah, bl, preferred_element_type=F32))


def _dot_exact_rhs(a, b_bf16):
    hi, lo = _split(a)
    return (jnp.dot(hi, b_bf16, preferred_element_type=F32)
            + jnp.dot(lo, b_bf16, preferred_element_type=F32))


def _dot_exact_lhs(a_bf16, b):
    hi, lo = _split(b)
    return (jnp.dot(a_bf16, hi, preferred_element_type=F32)
            + jnp.dot(a_bf16, lo, preferred_element_type=F32))


def _modnorm(x, g, sc_b, sh_b, sc_c, sh_c, is_ctx):
    ms = jnp.mean(x * x, axis=-1, keepdims=True)
    y = x * lax.rsqrt(ms + NORM_EPS) * g
    sc = jnp.where(is_ctx, sc_c, sc_b)
    sh = jnp.where(is_ctx, sh_c, sh_b)
    return y * (1.0 + sc) + sh


def _is_ctx_rows(tile_idx, tm, ctx_len):
    row = tile_idx * tm + lax.broadcasted_iota(jnp.int32, (tm, 1), 0)
    return row < ctx_len


def _mod_specs(d, k_sh, k_sc):
    return [
        pl.BlockSpec((1, 1, d), lambda b, i, *_: (b, 0, k_sh)),
        pl.BlockSpec((1, 1, d), lambda b, i, *_: (b, 0, k_sc)),
        pl.BlockSpec((1, d), lambda b, i, *_: (0, k_sh)),
        pl.BlockSpec((1, d), lambda b, i, *_: (0, k_sc)),
    ]


def _mod_kernel(c_ref, w_ref, b_ref, o_ref):
    c = c_ref[...]
    s = c * jax.nn.sigmoid(c)
    o_ref[0] = _bdot(s, w_ref[0]) + b_ref[0]


def _modulation(cvec, ada_w, ada_b):
    depth, d, n = ada_w.shape
    tn = 2048
    return pl.pallas_call(
        _mod_kernel,
        grid=(depth, n // tn),
        in_specs=[pl.BlockSpec((8, d), lambda l, j: (0, 0)),
                  pl.BlockSpec((1, d, tn), lambda l, j: (l, 0, j)),
                  pl.BlockSpec((1, 1, tn), lambda l, j: (l, 0, j))],
        out_specs=pl.BlockSpec((1, 8, tn), lambda l, j: (l, 0, j)),
        out_shape=jax.ShapeDtypeStruct((depth, 8, n), F32),
        compiler_params=_cparams(("parallel", "parallel")),
        name="adaln_mod",
    )(cvec, ada_w, ada_b.reshape(depth, 1, n))


def _inproj_kernel(x_ref, g_ref, bsh_ref, bsc_ref, csh_ref, csc_ref, w_ref, cos_ref, sin_ref,
                   q_ref, k_ref, v_ref, rr_ref, rk_ref, rv_ref, lo_ref, *, tm, ctx_len):
    i = pl.program_id(1)
    is_ctx = _is_ctx_rows(i, tm, ctx_len)
    h = _modnorm(x_ref[0], g_ref[...], bsc_ref[0], bsh_ref[0], csc_ref[...], csh_ref[...], is_ctx)
    p = jnp.dot(h.astype(BF16), w_ref[...], preferred_element_type=F32)
    cos_t = cos_ref[...]
    sin_t = sin_ref[...]
    lane = lax.broadcasted_iota(jnp.int32, (tm, LANES), 1)
    first = (lane & 31) < 16

    def rope(z):
        partner = jnp.where(first, pltpu.roll(z, LANES - 16, 1), pltpu.roll(z, 16, 1))
        return z * cos_t + partner * sin_t

    for hh in range(ATTN_HEADS):
        lo = LANES * hh
        q_ref[0, :, lo:lo + LANES] = (rope(p[:, lo:lo + LANES]) * (ATTN_QK_DIM ** -0.5)).astype(BF16)
        k_ref[0, :, lo:lo + LANES] = rope(p[:, ATTN_WIDTH + lo:ATTN_WIDTH + lo + LANES]).astype(BF16)
    o = 2 * ATTN_WIDTH
    v_ref[0] = p[:, o:o + ATTN_WIDTH].astype(BF16)
    o += ATTN_WIDTH
    rr_ref[0] = p[:, o:o + RWKV_WIDTH]
    rk_ref[0] = p[:, o + RWKV_WIDTH:o + 2 * RWKV_WIDTH]
    rv_ref[0] = p[:, o + 2 * RWKV_WIDTH:o + 3 * RWKV_WIDTH]
    o += 3 * RWKV_WIDTH
    lo_ref[0] = p[:, o:o + LORA_PAD]


def _in_proj(xa, g1, modb, modc, w_in_p, cos_t, sin_t, ctx_len, tm):
    b, ta, d = xa.shape
    n = w_in_p.shape[1]
    tok = lambda w: pl.BlockSpec((1, tm, w), lambda bb, i: (bb, i, 0))
    outs = [jax.ShapeDtypeStruct((b, ta, ATTN_WIDTH), BF16)] * 3 \
        + [jax.ShapeDtypeStruct((b, ta, RWKV_WIDTH), F32)] * 3 \
        + [jax.ShapeDtypeStruct((b, ta, LORA_PAD), F32)]
    return pl.pallas_call(
        functools.partial(_inproj_kernel, tm=tm, ctx_len=ctx_len),
        grid=(b, ta // tm),
        in_specs=[tok(d), pl.BlockSpec((1, d), lambda bb, i: (0, 0))] + _mod_specs(d, 0, 1)
        + [pl.BlockSpec((d, n), lambda bb, i: (0, 0)),
           pl.BlockSpec((tm, LANES), lambda bb, i: (i, 0)),
           pl.BlockSpec((tm, LANES), lambda bb, i: (i, 0))],
        out_specs=[tok(ATTN_WIDTH)] * 3 + [tok(RWKV_WIDTH)] * 3 + [tok(LORA_PAD)],
        out_shape=outs,
        compiler_params=_cparams(("parallel", "parallel")),
        name="norm1_in_proj",
    )(xa, g1, modb, modb, modc, modc, w_in_p, cos_t, sin_t)


def _attn_kernel(lam_ref, sg_ref, q_ref, k_ref, v_ref, o_ref, qs_ref, m_ref, l_ref, acc_ref,
                 *, tq, tk, ctx_len, lam_init):
    qi = pl.program_id(2)
    ki = pl.program_id(3)
    nk = pl.num_programs(3)

    @pl.when(ki == 0)
    def _init():
        q = q_ref[0]
        lane = lax.broadcasted_iota(jnp.int32, (tq, LANES), 1)
        zero = jnp.zeros_like(q)
        qs_ref[0:tq, :] = jnp.where(lane < ATTN_QK_DIM, q, zero)
        qs_ref[tq:2 * tq, :] = jnp.where(lane >= ATTN_QK_DIM, q, zero)
        m_ref[...] = jnp.full(m_ref.shape, -jnp.inf, F32)
        l_ref[...] = jnp.zeros(l_ref.shape, F32)
        acc_ref[...] = jnp.zeros(acc_ref.shape, F32)

    is_ctx_q = qi < ctx_len // tq

    def step(masked):
        s = lax.dot_general(qs_ref[...], k_ref[0], (((1,), (1,)), ((), ())),
                            preferred_element_type=F32)
        if masked:
            col = lax.broadcasted_iota(jnp.int32, s.shape, 1)
            s = jnp.where(col < ctx_len, s, -jnp.inf)
        m_prev = m_ref[...]
        m_new = jnp.maximum(m_prev, jnp.max(s, axis=1, keepdims=True))
        alpha = jnp.exp(m_prev - m_new)
        p = jnp.exp(s - m_new)
        l_ref[...] = alpha * l_ref[...] + jnp.sum(p, axis=1, keepdims=True)
        acc_ref[...] = alpha * acc_ref[...] + jnp.dot(p.astype(BF16), v_ref[0],
                                                      preferred_element_type=F32)
        m_ref[...] = m_new

    @pl.when(jnp.logical_and(is_ctx_q, ki == 0))
    def _ctx_step():
        step(True)

    @pl.when(jnp.logical_not(is_ctx_q))
    def _lat_step():
        step(False)

    @pl.when(ki == nk - 1)
    def _finish():
        lp = lam_ref[...]
        lam = (jnp.exp(jnp.sum(lp[0:1] * lp[1:2], axis=1, keepdims=True))
               - jnp.exp(jnp.sum(lp[2:3] * lp[3:4], axis=1, keepdims=True)) + lam_init)
        o1 = acc_ref[0:tq, :] / l_ref[0:tq, :]
        o2 = acc_ref[tq:2 * tq, :] / l_ref[tq:2 * tq, :]
        o = o1 - lam * o2
        ms = jnp.mean(o * o, axis=-1, keepdims=True)
        y = o * lax.rsqrt(ms + SUBLN_EPS) * sg_ref[...]
        o_ref[0] = (y * (1.0 - lam_init)).astype(BF16)


def _attention(q, k, v, lam_p, subln_g, lam_init, ctx_len, tq, tk):
    b, ta, _ = q.shape
    return pl.pallas_call(
        functools.partial(_attn_kernel, tq=tq, tk=tk, ctx_len=ctx_len, lam_init=lam_init),
        grid=(b, ATTN_HEADS, ta // tq, ta // tk),
        in_specs=[pl.BlockSpec((4, ATTN_QK_DIM), lambda bb, h, i, j: (0, 0)),
                  pl.BlockSpec((1, ATTN_V_DIM), lambda bb, h, i, j: (0, 0)),
                  pl.BlockSpec((1, tq, LANES), lambda bb, h, i, j: (bb, i, h)),
                  pl.BlockSpec((1, tk, LANES), lambda bb, h, i, j: (bb, j, h)),
                  pl.BlockSpec((1, tk, LANES), lambda bb, h, i, j: (bb, j, h))],
        out_specs=pl.BlockSpec((1, tq, LANES), lambda bb, h, i, j: (bb, i, h)),
        out_shape=jax.ShapeDtypeStruct((b, ta, ATTN_WIDTH), BF16),
        scratch_shapes=[pltpu.VMEM((2 * tq, LANES), BF16),
                        pltpu.VMEM((2 * tq, 1), F32),
                        pltpu.VMEM((2 * tq, 1), F32),
                        pltpu.VMEM((2 * tq, ATTN_V_DIM), F32)],
        compiler_params=_cparams(("parallel", "parallel", "parallel", "arbitrary")),
        name="diff_attention",
    )(lam_p, subln_g, q, k, v)


def _prep_kernel(*refs, tt, ctx_len, ta, has_vres):
    (pr_ref, pk_ref, pv_ref, hpr_ref, hpk_ref, hpv_ref, hnr_ref, hnk_ref, hnv_ref, lo_ref,
     mu_ref, w0_ref, w2_ref, a0_ref, a2_ref, g2_ref, kk_ref, ka_ref, seg_ref) = refs[:19]
    n_in = 19
    if has_vres:
        vf_ref, v0_ref, v1_ref, v2_ref = refs[19:23]
        n_in = 23
    (r_out, v_out, kk_out, g_out, lwf_out, lwb_out, bf_out, bb_out, kdf_out, kdb_out) = refs[n_in:]

    i = pl.program_id(1)
    row = lax.broadcasted_iota(jnp.int32, (tt, 1), 0)
    gidx = i * tt + row
    no_prev = jnp.logical_or(gidx == 0, gidx == ctx_len)
    no_next = jnp.logical_or(gidx == ctx_len - 1, gidx == ta - 1)

    def shift_mix(p_ref, hp_ref, hn_ref, j):
        p = p_ref[0]
        pp = jnp.where(row == 0, hp_ref[0, 7:8, :], pltpu.roll(p, 1, 0))
        pp = jnp.where(no_prev, 0.0, pp)
        pn = jnp.where(row == tt - 1, hn_ref[0, 0:1, :], pltpu.roll(p, tt - 1, 0))
        pn = jnp.where(no_next, 0.0, pn)
        return p + mu_ref[2 * j:2 * j + 1, :] * (pp - p) + mu_ref[2 * j + 1:2 * j + 2, :] * (pn - p)

    r = shift_mix(pr_ref, hpr_ref, hnr_ref, 0)
    k = shift_mix(pk_ref, hpk_ref, hnk_ref, 1)
    v = shift_mix(pv_ref, hpv_ref, hnv_ref, 2)
    if has_vres:
        mix = jax.nn.sigmoid(v0_ref[...] + _bdot(_bdot(v, v1_ref[...]), v2_ref[...]))
        v = v + (vf_ref[0] - v) * mix
    lo = lo_ref[0]
    gd = lo[:, 2 * DECAY_LORA + 2 * AAA_LORA:2 * DECAY_LORA + 2 * AAA_LORA + GATE_LORA]
    g_out[0] = _bdot(jax.nn.sigmoid(gd), g2_ref[...])
    kk = k * kk_ref[...]
    ss = _dot_exact_rhs(kk * kk, seg_ref[...])
    kk = kk * lax.rsqrt(jnp.maximum(ss, 1e-24))
    r_out[0] = r
    v_out[0] = v
    kk_out[0] = kk
    lw_outs = (lwf_out, lwb_out)
    b_outs = (bf_out, bb_out)
    kd_outs = (kdf_out, kdb_out)
    for d in range(2):
        wd = lo[:, d * DECAY_LORA:(d + 1) * DECAY_LORA]
        wl = w0_ref[d:d + 1, :] + _bdot(jnp.tanh(wd), w2_ref[d])
        lw_outs[d][0] = -jax.nn.sigmoid(wl) * math.exp(-0.5)
        ad = lo[:, 2 * DECAY_LORA + d * AAA_LORA:2 * DECAY_LORA + (d + 1) * AAA_LORA]
        a = jax.nn.sigmoid(a0_ref[d:d + 1, :] + _bdot(ad, a2_ref[d]))
        b_outs[d][0] = a * kk
        kd_outs[d][0] = k * (1.0 + (a - 1.0) * ka_ref[...])


def _rwkv_prepare(pr, pk, pv, lora, mu, w0, w2, a0, a2, g2, k_k, k_a, seg, vres, ctx_len, tt):
    b, ta, w = pr.shape
    nb8 = ta // 8
    tok = pl.BlockSpec((1, tt, w), lambda bb, i: (bb, i, 0))
    prev = pl.BlockSpec((1, 8, w), lambda bb, i: (bb, jnp.maximum(i * (tt // 8) - 1, 0), 0))
    nxt = pl.BlockSpec((1, 8, w), lambda bb, i: (bb, jnp.minimum((i + 1) * (tt // 8), nb8 - 1), 0))
    full = lambda a: pl.BlockSpec(a.shape, lambda bb, i, _n=a.ndim: (0,) * _n)
    consts = [mu.reshape(6, w), w0, w2, a0, a2, g2, k_k.reshape(1, w), k_a.reshape(1, w), seg]
    ins = [pr, pk, pv, pr, pk, pv, pr, pk, pv, lora] + consts
    specs = [tok] * 3 + [prev] * 3 + [nxt] * 3 \
        + [pl.BlockSpec((1, tt, LORA_PAD), lambda bb, i: (bb, i, 0))] + [full(a) for a in consts]
    if vres is not None:
        v_first, v0, v1, v2 = vres
        extra = [v0.reshape(1, w), v1, v2]
        ins += [v_first] + extra
        specs += [tok] + [full(a) for a in extra]
    return pl.pallas_call(
        functools.partial(_prep_kernel, tt=tt, ctx_len=ctx_len, ta=ta, has_vres=vres is not None),
        grid=(b, ta // tt),
        in_specs=specs,
        out_specs=[tok] * 10,
        out_shape=[jax.ShapeDtypeStruct((b, ta, w), F32)] * 10,
        compiler_params=_cparams(("parallel", "parallel")),
        name="rwkv_prepare",
    )(*ins)


def _scan_kernel(rf_ref, vf_ref, kkf_ref, lwf_ref, btf_ref, kdf_ref,
                 rb_ref, vb_ref, kkb_ref, lwb_ref, btb_ref, kdb_ref,
                 yf_ref, yb_ref, st_ref):
    s = pl.program_id(1)
    c = CHUNK
    n = RWKV_HEAD

    @pl.when(s == 0)
    def _init():
        st_ref[...] = jnp.zeros(st_ref.shape, F32)

    t_i = lax.broadcasted_iota(jnp.int32, (c, c), 0)
    j_i = lax.broadcasted_iota(jnp.int32, (c, c), 1)
    t2 = lax.broadcasted_iota(jnp.int32, (2 * c, 2 * c), 0)
    j2 = lax.broadcasted_iota(jnp.int32, (2 * c, 2 * c), 1)
    tt2 = t2 & (c - 1)
    jj2 = j2 & (c - 1)
    eye = jnp.where(t_i == j_i, 1.0, 0.0)
    blk_bits = int(math.log2(INV_BLOCK))
    same_blk = (t_i >> blk_bits) == (j_i >> blk_bits)
    dirs =((rf_ref, vf_ref, kkf_ref, lwf_ref, btf_ref, kdf_ref, yf_ref, False),
            (rb_ref, vb_ref, kkb_ref, lwb_ref, btb_ref, kdb_ref, yb_ref, True))
    for d, (r_ref, v_ref, kk_ref, lw_ref, bt_ref, kd_ref, y_ref, rev) in enumerate(dirs):
        incl = (j_i >= t_i) if rev else (j_i <= t_i)
        tri = jnp.where(incl, 1.0, 0.0).astype(BF16)
        before = (jj2 > tt2) if rev else (jj2 < tt2)
        mask_m = jnp.logical_or(before, jnp.logical_and(t2 >= c, jj2 == tt2))
        lw = lw_ref[0]
        cs = _dot_exact_lhs(tri, lw)
        tot = cs[0:1, :] if rev else cs[c - 1:c, :]
        p_in = jnp.exp(cs)
        p_ex = jnp.exp(cs - lw)
        p_inv = jnp.exp(-cs)
        p_rest = jnp.exp(tot - cs)
        p_tot = jnp.exp(tot)
        kk = kk_ref[0]
        bt = bt_ref[0]
        kd = kd_ref[0]
        rt = r_ref[0] * p_in
        at = -kk * p_ex
        bti = bt * p_inv
        kti = kd * p_inv
        bp = bt * p_rest
        kp = kd * p_rest
        vv = v_ref[0]
        for h in range(RWKV_HEADS):
            sl = slice(h * n, (h + 1) * n)
            at_h = at[:, sl]
            rt_h = rt[:, sl]
            x1 = jnp.concatenate([at_h, rt_h], axis=0)
            x2 = jnp.concatenate([bti[:, sl], kti[:, sl]], axis=0)
            m = jnp.where(mask_m, _bdot_nt(x1, x2), 0.0)
            a_full = m[0:c, 0:c]
            d_pow = jnp.where(same_blk, a_full, 0.0)
            l_off = a_full - d_pow
            t_diag = eye + d_pow
            for _ in range(int(math.log2(INV_BLOCK)) - 1):
                d_pow = _dot3(d_pow, d_pow)
                t_diag = t_diag + _dot3(d_pow, t_diag)
            n_mat = _dot3(t_diag, l_off)
            x_mat = t_diag + _dot3(n_mat, t_diag)
            tinv = x_mat + _dot3(_dot3(n_mat, n_mat), x_mat)
            st = st_ref[d, h]
            vh = vv[:, sl]
            u = _bdot(tinv, _bdot(m[0:c, c:2 * c], vh) + _bdot_nt(at_h, st))
            uv = jnp.concatenate([u, vh], axis=0)
            y_ref[0, :, sl] = _bdot(m[c:2 * c, :], uv) + _bdot_nt(rt_h, st)
            bk = jnp.concatenate([bp[:, sl], kp[:, sl]], axis=0)
            st_ref[d, h] = p_tot[:, sl] * st + _bdot_tn(uv, bk)


def _rwkv_scan(r, v, kk, lw_f, lw_b, bt_f, bt_b, kd_f, kd_b, ctx_len):
    b, ta, w = r.shape
    nc = ta // CHUNK
    nctx = ctx_len // CHUNK

    def rev_idx(s):
        return jnp.where(s < nctx, nctx - 1 - s, nc - 1 - s + nctx)

    fwd = pl.BlockSpec((1, CHUNK, w), lambda bb, s: (bb, s, 0))
    bwd = pl.BlockSpec((1, CHUNK, w), lambda bb, s: (bb, rev_idx(s), 0))
    return pl.pallas_call(
        _scan_kernel,
        grid=(b, nc),
        in_specs=[fwd] * 6 + [bwd] * 6,
        out_specs=[fwd, bwd],
        out_shape=[jax.ShapeDtypeStruct((b, ta, w), F32)] * 2,
        scratch_shapes=[pltpu.VMEM((2, RWKV_HEADS, RWKV_HEAD, RWKV_HEAD), F32)],
        compiler_params=_cparams(("parallel", "arbitrary")),
        name="rwkv_scan",
    )(r, v, kk, lw_f, bt_f, kd_f, r, v, kk, lw_b, bt_b, kd_b)


def _postmix_kernel(x_ref, att_ref, yf_ref, yb_ref, r_ref, v_ref, kdf_ref, kdb_ref, g_ref,
                    bgt_ref, cgt_ref, wo_ref, lnw_ref, lnb_ref, rk_ref, seg_ref, o_ref, *, tm, ctx_len):
    i = pl.program_id(1)
    is_ctx = _is_ctx_rows(i, tm, ctx_len)
    seg = seg_ref[...]
    y = yf_ref[0] + yb_ref[0]
    inv_n = 1.0 / RWKV_HEAD
    mean = _dot_exact_rhs(y, seg) * inv_n
    dy = y - mean
    var = _dot_exact_rhs(dy * dy, seg) * inv_n
    o = dy * lax.rsqrt(var + LNX_EPS) * lnw_ref[...] + lnb_ref[...]
    r = r_ref[0]
    v = v_ref[0]
    rrk = r * rk_ref[...]
    o = o + _dot_exact_rhs(rrk * kdf_ref[0], seg) * v
    o = o + _dot_exact_rhs(rrk * kdb_ref[0], seg) * v
    rw = o * g_ref[0]
    mixed = (jnp.dot(att_ref[0], wo_ref[0:ATTN_WIDTH, :], preferred_element_type=F32)
             + jnp.dot(rw.astype(BF16), wo_ref[ATTN_WIDTH:, :], preferred_element_type=F32))
    gt = jnp.where(is_ctx, cgt_ref[...], bgt_ref[0])
    o_ref[0] = x_ref[0] + gt * mixed


def _post_mix(xa, att, y_f, y_b, r, v, kd_f, kd_b, g, modb, modc, w_out, lnw, lnb, r_k, seg, ctx_len, tm):
    b, ta, d = xa.shape
    w = RWKV_WIDTH
    tok = lambda ww: pl.BlockSpec((1, tm, ww), lambda bb, i: (bb, i, 0))
    full = lambda a: pl.BlockSpec(a.shape, lambda bb, i, _n=a.ndim: (0,) * _n)
    consts = [w_out, lnw.reshape(1, w), lnb.reshape(1, w), r_k.reshape(1, w), seg]
    return pl.pallas_call(
        functools.partial(_postmix_kernel, tm=tm, ctx_len=ctx_len),
        grid=(b, ta // tm),
        in_specs=[tok(d), tok(ATTN_WIDTH)] + [tok(w)] * 7
        + [pl.BlockSpec((1, 1, d), lambda bb, i: (bb, 0, 2)), pl.BlockSpec((1, d), lambda bb, i: (0, 2))]
        + [full(a) for a in consts],
        out_specs=tok(d),
        out_shape=jax.ShapeDtypeStruct((b, ta, d), F32),
        compiler_params=_cparams(("parallel", "parallel")),
        name="rwkv_finish_out_proj",
    )(xa, att, y_f, y_b, r, v, kd_f, kd_b, g, modb, modc, *consts)


def _ffn_kernel(x_ref, g_ref, bsh_ref, bsc_ref, csh_ref, csc_ref, bgt_ref, cgt_ref,
                w1_ref, w3_ref, w2_ref, o_ref, *, tm, ctx_len):
    i = pl.program_id(1)
    is_ctx = _is_ctx_rows(i, tm, ctx_len)
    x = x_ref[0]
    h = _modnorm(x, g_ref[...], bsc_ref[0], bsh_ref[0], csc_ref[...], csh_ref[...], is_ctx).astype(BF16)
    a = jnp.dot(h, w1_ref[...], preferred_element_type=F32)
    bb = jnp.dot(h, w3_ref[...], preferred_element_type=F32)
    act = (a * jax.nn.sigmoid(a) * bb).astype(BF16)
    out = jnp.dot(act, w2_ref[...], preferred_element_type=F32)
    gt = jnp.where(is_ctx, cgt_ref[...], bgt_ref[0])
    o_ref[0] = x + gt * out


def _dense_ffn(xa, g2, modb, modc, w1, w3, w2, ctx_len, tm):
    b, ta, d = xa.shape
    tok = pl.BlockSpec((1, tm, d), lambda bb, i: (bb, i, 0))
    full = lambda a: pl.BlockSpec(a.shape, lambda bb, i, _n=a.ndim: (0,) * _n,
                                  pipeline_mode=pl.Buffered(1))
    return pl.pallas_call(
        functools.partial(_ffn_kernel, tm=tm, ctx_len=ctx_len),
        grid=(b, ta // tm),
        in_specs=[tok, pl.BlockSpec((1, d), lambda bb, i: (0, 0))] + _mod_specs(d, 3, 4)
        + [pl.BlockSpec((1, 1, d), lambda bb, i: (bb, 0, 5)), pl.BlockSpec((1, d), lambda bb, i: (0, 5))]
        + [full(w1), full(w3), full(w2)],
        out_specs=tok,
        out_shape=jax.ShapeDtypeStruct((b, ta, d), F32),
        compiler_params=_cparams(("parallel", "parallel")),
        name="norm2_dense_swiglu",
    )(xa, g2, modb, modb, modc, modc, modb, modc, w1, w3, w2)


def _moe_kernel(x_ref, g_ref, bsh_ref, bsc_ref, csh_ref, csc_ref, bgt_ref, cgt_ref, rw_ref,
                w1_ref, w3_ref, w2_ref, o_ref, h_ref, comb_ref, acc_ref, *, tm, ctx_len):
    i = pl.program_id(1)
    e = pl.program_id(2)
    f = pl.program_id(3)
    last = jnp.logical_and(e == pl.num_programs(2) - 1, f == pl.num_programs(3) - 1)

    @pl.when(jnp.logical_and(e == 0, f == 0))
    def _route():
        is_ctx = _is_ctx_rows(i, tm, ctx_len)
        h = _modnorm(x_ref[0], g_ref[...], bsc_ref[0], bsh_ref[0], csc_ref[...], csh_ref[...], is_ctx)
        h_ref[...] = h.astype(BF16)
        hh, hl = _split(h)
        wh, wl = _split(rw_ref[...])
        logits = (jnp.dot(hh, wh, preferred_element_type=F32) + jnp.dot(hl, wh, preferred_element_type=F32)
                  + jnp.dot(hh, wl, preferred_element_type=F32))
        lane = lax.broadcasted_iota(jnp.int32, logits.shape, 1).astype(F32)
        neg = -jnp.inf
        logits = jnp.where(lane < N_EXPERTS, logits, neg)
        v1 = jnp.max(logits, axis=1, keepdims=True)
        i1 = jnp.min(jnp.where(logits == v1, lane, float(LANES)), axis=1, keepdims=True)
        rest = jnp.where(lane == i1, neg, logits)
        v2 = jnp.max(rest, axis=1, keepdims=True)
        i2 = jnp.min(jnp.where(rest == v2, lane, float(LANES)), axis=1, keepdims=True)
        e2 = jnp.exp(v2 - v1)
        den = 1.0 + e2
        comb_ref[...] = jnp.where(lane == i1, 1.0 / den, 0.0) + jnp.where(lane == i2, e2 / den, 0.0)
        acc_ref[...] = jnp.zeros(acc_ref.shape, F32)

    h = h_ref[...]
    lane = lax.broadcasted_iota(jnp.int32, comb_ref.shape, 1)
    gate = jnp.sum(jnp.where(lane == e, comb_ref[...], 0.0), axis=1, keepdims=True)
    a = jnp.dot(h, w1_ref[0], preferred_element_type=F32)
    bb = jnp.dot(h, w3_ref[0], preferred_element_type=F32)
    act = (a * jax.nn.sigmoid(a) * bb * gate).astype(BF16)
    acc_ref[...] += jnp.dot(act, w2_ref[0], preferred_element_type=F32)

    @pl.when(last)
    def _finish():
        is_ctx = _is_ctx_rows(i, tm, ctx_len)
        gt = jnp.where(is_ctx, cgt_ref[...], bgt_ref[0])
        o_ref[0] = x_ref[0] + gt * acc_ref[...]


def _moe_ffn(xa, g2, modb, modc, router_p, w1, w3, w2, ctx_len, tm, nf):
    b, ta, d = xa.shape
    ne, _, ff = w1.shape
    tf = ff // nf
    tok = pl.BlockSpec((1, tm, d), lambda bb, i, e, f: (bb, i, 0))
    return pl.pallas_call(
        functools.partial(_moe_kernel, tm=tm, ctx_len=ctx_len),
        grid=(b, ta // tm, ne, nf),
        in_specs=[tok, pl.BlockSpec((1, d), lambda bb, i, e, f: (0, 0))] + _mod_specs(d, 3, 4)
        + [pl.BlockSpec((1, 1, d), lambda bb, i, e, f: (bb, 0, 5)),
           pl.BlockSpec((1, d), lambda bb, i, e, f: (0, 5)),
           pl.BlockSpec((d, LANES), lambda bb, i, e, f: (0, 0)),
           pl.BlockSpec((1, d, tf), lambda bb, i, e, f: (e, 0, f)),
           pl.BlockSpec((1, d, tf), lambda bb, i, e, f: (e, 0, f)),
           pl.BlockSpec((1, tf, d), lambda bb, i, e, f: (e, f, 0))],
        out_specs=tok,
        out_shape=jax.ShapeDtypeStruct((b, ta, d), F32),
        scratch_shapes=[pltpu.VMEM((tm, d), BF16), pltpu.VMEM((tm, LANES), F32), pltpu.VMEM((tm, d), F32)],
        compiler_params=_cparams(("parallel", "parallel", "arbitrary", "arbitrary")),
        name="norm2_moe_swiglu",
    )(xa, g2, modb, modb, modc, modc, modb, modc, router_p, w1, w3, w2)


def _final_kernel(x_ref, g_ref, o_ref):
    x = x_ref[0]
    ms = jnp.mean(x * x, axis=-1, keepdims=True)
    o_ref[0] = x * lax.rsqrt(ms + NORM_EPS) * g_ref[...]


def _final_norm(xa, g, ctx_len, seq, tm):
    b, _, d = xa.shape
    off = ctx_len // tm
    return pl.pallas_call(
        _final_kernel,
        grid=(b, seq // tm),
        in_specs=[pl.BlockSpec((1, tm, d), lambda bb, i: (bb, i + off, 0)),
                  pl.BlockSpec((1, d), lambda bb, i: (0, 0))],
        out_specs=pl.BlockSpec((1, tm, d), lambda bb, i: (bb, i, 0)),
        out_shape=jax.ShapeDtypeStruct((b, seq, d), F32),
        compiler_params=_cparams(("parallel", "parallel")),
        name="final_norm",
    )(xa, g)


def _rope_tables(seq, ctx_len):
    axis_dim = ATTN_QK_DIM // 2
    inv_freq = ROPE_THETA ** (-jnp.arange(0, axis_dim, 2, dtype=F32) / axis_dim)
    t = jnp.arange(seq)
    pos = jnp.stack([(t // GRID_W).astype(F32), (t % GRID_W).astype(F32)], axis=1)
    lane = np.arange(LANES)
    axis = (lane % ATTN_QK_DIM) // axis_dim
    freq = lane % (axis_dim // 2)
    sign = np.where((lane % axis_dim) < axis_dim // 2, -1.0, 1.0).astype(np.float32)
    ang = pos[:, axis] * inv_freq[freq][None, :]
    cos_t = jnp.concatenate([jnp.ones((ctx_len, LANES), F32), jnp.cos(ang)], axis=0)
    sin_t = jnp.concatenate([jnp.zeros((ctx_len, LANES), F32), jnp.sin(ang) * sign[None, :]], axis=0)
    return cos_t, sin_t


def _pick_tile(n, candidates):
    for c in candidates:
        if n % c == 0:
            return c
    raise ValueError(f"no tile in {candidates} divides {n}")


def kernel(x, c, ctx, c_ctx, ada_w, ada_b, norm1_g, norm2_g, w_in, w_out, diff_lambda, subln_g, rwkv_mu, rwkv_w0, rwkv_w2, rwkv_a0, rwkv_a2, rwkv_v0, rwkv_v1, rwkv_v2, rwkv_g2, rwkv_k_k, rwkv_k_a, rwkv_r_k, lnx_w, lnx_b, ffn_w1, ffn_w3, ffn_w2, router_w, exp_w1, exp_w3, exp_w2, final_g):
    b, seq, d = x.shape
    ctx_len = ctx.shape[1]
    depth = ada_w.shape[0]
    ta = ctx_len + seq
    assert b + 1 <= 8 and seq % GRID_W == 0 and ctx_len % CHUNK == 0 and ta % CHUNK == 0

    tm = _pick_tile(math.gcd(ctx_len, ta), (256, 128))
    tq = tm
    tk = _pick_tile(ta, (1280, 640, 256, 128))
    assert ctx_len <= tk
    tm_moe = _pick_tile(ta, (640, 256, 128))

    xa = jnp.concatenate([ctx, x], axis=1)
    cvec = jnp.zeros((8, d), F32).at[:b].set(c).at[b].set(c_ctx)
    mod = _modulation(cvec, ada_w, ada_b)
    cos_t, sin_t = _rope_tables(seq, ctx_len)
    seg = jnp.asarray(np.kron(np.eye(RWKV_HEADS), np.ones((RWKV_HEAD, RWKV_HEAD))), BF16)
    n_in = w_in.shape[2]
    n_in_p = 3 * ATTN_WIDTH + 3 * RWKV_WIDTH + LORA_PAD
    router_p = jnp.pad(router_w, ((0, 0), (0, 0), (0, LANES - router_w.shape[2])))

    v_first = None
    for l in range(depth):
        lam_init = 0.8 - 0.6 * math.exp(-0.3 * l)
        modb = mod[l, :b].reshape(b, 1, 6 * d)
        modc = mod[l, b:b + 1]
        w_in_p = jnp.pad(w_in[l], ((0, 0), (0, n_in_p - n_in))).astype(BF16)
        q, k, v, pr, pk, pv, lora = _in_proj(xa, norm1_g[l].reshape(1, d), modb, modc, w_in_p,
                                             cos_t, sin_t, ctx_len, tm)
        att = _attention(q, k, v, diff_lambda[l], subln_g[l].reshape(1, ATTN_V_DIM), lam_init,
                         ctx_len, tq, tk)
        vres = None if l == 0 else (v_first, rwkv_v0[l - 1], rwkv_v1[l - 1], rwkv_v2[l - 1])
        r, vv, kk, g, lw_f, lw_b, bt_f, bt_b, kd_f, kd_b = _rwkv_prepare(
            pr, pk, pv, lora, rwkv_mu[l], rwkv_w0[l], rwkv_w2[l], rwkv_a0[l], rwkv_a2[l], rwkv_g2[l],
            rwkv_k_k[l], rwkv_k_a[l], seg, vres, ctx_len, tm)
        if l == 0:
            v_first = vv
        y_f, y_b = _rwkv_scan(r, vv, kk, lw_f, lw_b, bt_f, bt_b, kd_f, kd_b, ctx_len)
        xa = _post_mix(xa, att, y_f, y_b, r, vv, kd_f, kd_b, g, modb, modc, w_out[l].astype(BF16),
                       lnx_w[l], lnx_b[l], rwkv_r_k[l], seg, ctx_len, tm)
        i = l // 2
        if l % 2 == 0:
            xa = _dense_ffn(xa, norm2_g[l].reshape(1, d), modb, modc, ffn_w1[i].astype(BF16),
                            ffn_w3[i].astype(BF16), ffn_w2[i].astype(BF16), ctx_len, tm)
        else:
            xa = _moe_ffn(xa, norm2_g[l].reshape(1, d), modb, modc, router_p[i],
                          exp_w1[i].astype(BF16), exp_w3[i].astype(BF16), exp_w2[i].astype(BF16),
                          ctx_len, tm_moe, 2)
    return _final_norm(xa, final_g.reshape(1, d), ctx_len, seq, tm)
```

```python
import functools
import math

import numpy as np
import jax
import jax.numpy as jnp
from jax import lax
from jax.experimental import pallas as pl
from jax.experimental.pallas import tpu as pltpu

F32 = jnp.float32
BF16 = jnp.bfloat16

ATTN_HEADS = 4
ATTN_QK_DIM = 64
ATTN_V_DIM = 128
ATTN_WIDTH = ATTN_HEADS * ATTN_V_DIM
RWKV_HEADS = 8
RWKV_HEAD = 64
RWKV_WIDTH = RWKV_HEADS * RWKV_HEAD
DECAY_LORA = 32
AAA_LORA = 32
GATE_LORA = 96
LORA_PAD = 256
N_EXPERTS = 8
GRID_W = 64
ROPE_THETA = 10000.0
NORM_EPS = 1e-6
SUBLN_EPS = 1e-5
LNX_EPS = 64e-5
LANES = 128
CHUNK = 64
INV_BLOCK = 16
VMEM_LIMIT = 56 * 1024 * 1024
Q_SCALE = ATTN_QK_DIM ** -0.5 * math.log2(math.e)


def _cparams(sem, vmem=VMEM_LIMIT):
    return pltpu.CompilerParams(dimension_semantics=sem, vmem_limit_bytes=vmem)


def _bdot(a, b):
    return jnp.dot(a.astype(BF16), b.astype(BF16), preferred_element_type=F32)


def _bdot_nt(a, b):
    return lax.dot_general(a.astype(BF16), b.astype(BF16), (((1,), (1,)), ((), ())),
                           preferred_element_type=F32)


def _bdot_tn(a, b):
    return lax.dot_general(a.astype(BF16), b.astype(BF16), (((0,), (0,)), ((), ())),
                           preferred_element_type=F32)


def _split(a):
    hi = a.astype(BF16)
    lo = (a - hi.astype(F32)).astype(BF16)
    return hi, lo


def _dot3(a, b):
    ah, al = _split(a)
    bh, bl = _split(b)
    return (jnp.dot(ah, bh, preferred_element_type=F32) + jnp.dot(al, bh, preferred_element_type=F32)
            + jnp.dot(ah, bl, preferred_element_type=F32))


def _dot_exact_rhs(a, b_bf16):
    hi, lo = _split(a)
    return (jnp.dot(hi, b_bf16, preferred_element_type=F32)
            + jnp.dot(lo, b_bf16, preferred_element_type=F32))


def _dot_exact_lhs(a_bf16, b):
    hi, lo = _split(b)
    return (jnp.dot(a_bf16, hi, preferred_element_type=F32)
            + jnp.dot(a_bf16, lo, preferred_element_type=F32))


def _modnorm(x, g, sc_b, sh_b, sc_c, sh_c, is_ctx):
    ms = jnp.mean(x * x, axis=-1, keepdims=True)
    y = x * lax.rsqrt(ms + NORM_EPS) * g
    sc = jnp.where(is_ctx, sc_c, sc_b)
    sh = jnp.where(is_ctx, sh_c, sh_b)
    return y * (1.0 + sc) + sh


def _is_ctx_rows(tile_idx, tm, ctx_len):
    row = tile_idx * tm + lax.broadcasted_iota(jnp.int32, (tm, 1), 0)
    return row < ctx_len


def _mod_specs(d, k_sh, k_sc):
    return [
        pl.BlockSpec((1, 1, d), lambda b, i, *_: (b, 0, k_sh)),
        pl.BlockSpec((1, 1, d), lambda b, i, *_: (b, 0, k_sc)),
        pl.BlockSpec((1, d), lambda b, i, *_: (0, k_sh)),
        pl.BlockSpec((1, d), lambda b, i, *_: (0, k_sc)),
    ]


def _mod_kernel(c_ref, w_ref, b_ref, o_ref):
    c = c_ref[...]
    s = c * jax.nn.sigmoid(c)
    o_ref[0] = _bdot(s, w_ref[0]) + b_ref[0]


def _modulation(cvec, ada_w, ada_b):
    depth, d, n = ada_w.shape
    tn = 2048
    return pl.pallas_call(
        _mod_kernel,
        grid=(depth, n // tn),
        in_specs=[pl.BlockSpec((8, d), lambda l, j: (0, 0)),
                  pl.BlockSpec((1, d, tn), lambda l, j: (l, 0, j)),
                  pl.BlockSpec((1, 1, tn), lambda l, j: (l, 0, j))],
        out_specs=pl.BlockSpec((1, 8, tn), lambda l, j: (l, 0, j)),
        out_shape=jax.ShapeDtypeStruct((depth, 8, n), F32),
        compiler_params=_cparams(("parallel", "parallel")),
        name="adaln_mod",
    )(cvec, ada_w, ada_b.reshape(depth, 1, n))


def _inproj_kernel(x_ref, g_ref, bsh_ref, bsc_ref, csh_ref, csc_ref, w_ref, cos_ref, sin_ref,
                   q_ref, k_ref, v_ref, rr_ref, rk_ref, rv_ref, lo_ref, *, tm, ctx_len):
    i = pl.program_id(1)
    is_ctx = _is_ctx_rows(i, tm, ctx_len)
    h = _modnorm(x_ref[0], g_ref[...], bsc_ref[0], bsh_ref[0], csc_ref[...], csh_ref[...], is_ctx)
    p = jnp.dot(h.astype(BF16), w_ref[...], preferred_element_type=F32)
    cos_t = cos_ref[...]
    sin_t = sin_ref[...]
    lane = lax.broadcasted_iota(jnp.int32, (tm, LANES), 1)
    first = (lane & 31) < 16

    def rope(z):
        partner = jnp.where(first, pltpu.roll(z, LANES - 16, 1), pltpu.roll(z, 16, 1))
        return z * cos_t + partner * sin_t

    for hh in range(ATTN_HEADS):
        lo = LANES * hh
        q_ref[0, :, lo:lo + LANES] = (rope(p[:, lo:lo + LANES]) * Q_SCALE).astype(BF16)
        k_ref[0, :, lo:lo + LANES] = rope(p[:, ATTN_WIDTH + lo:ATTN_WIDTH + lo + LANES]).astype(BF16)
    o = 2 * ATTN_WIDTH
    v_ref[0] = p[:, o:o + ATTN_WIDTH].astype(BF16)
    o += ATTN_WIDTH
    rr_ref[0] = p[:, o:o + RWKV_WIDTH]
    rk_ref[0] = p[:, o + RWKV_WIDTH:o + 2 * RWKV_WIDTH]
    rv_ref[0] = p[:, o + 2 * RWKV_WIDTH:o + 3 * RWKV_WIDTH]
    o += 3 * RWKV_WIDTH
    lo_ref[0] = p[:, o:o + LORA_PAD]


def _in_proj(xa, g1, modb, modc, w_in_p, cos_t, sin_t, ctx_len, tm):
    b, ta, d = xa.shape
    n = w_in_p.shape[1]
    tok = lambda w: pl.BlockSpec((1, tm, w), lambda bb, i: (bb, i, 0))
    outs = [jax.ShapeDtypeStruct((b, ta, ATTN_WIDTH), BF16)] * 3 \
        + [jax.ShapeDtypeStruct((b, ta, RWKV_WIDTH), F32)] * 3 \
        + [jax.ShapeDtypeStruct((b, ta, LORA_PAD), F32)]
    return pl.pallas_call(
        functools.partial(_inproj_kernel, tm=tm, ctx_len=ctx_len),
        grid=(b, ta // tm),
        in_specs=[tok(d), pl.BlockSpec((1, d), lambda bb, i: (0, 0))] + _mod_specs(d, 0, 1)
        + [pl.BlockSpec((d, n), lambda bb, i: (0, 0)),
           pl.BlockSpec((tm, LANES), lambda bb, i: (i, 0)),
           pl.BlockSpec((tm, LANES), lambda bb, i: (i, 0))],
        out_specs=[tok(ATTN_WIDTH)] * 3 + [tok(RWKV_WIDTH)] * 3 + [tok(LORA_PAD)],
        out_shape=outs,
        compiler_params=_cparams(("parallel", "parallel")),
        name="norm1_in_proj",
    )(xa, g1, modb, modb, modc, modc, w_in_p, cos_t, sin_t)


def _attn_kernel(lam_ref, sg_ref, q_ref, k_ref, v_ref, o_ref, qs_ref, m_ref, l_ref, acc_ref,
                 *, tq, tk, ts, ctx_len, lam_init):
    qi = pl.program_id(2)
    ki = pl.program_id(3)
    nk = pl.num_programs(3)

    @pl.when(ki == 0)
    def _init():
        q = q_ref[0]
        lane = lax.broadcasted_iota(jnp.int32, (tq, LANES), 1)
        zero = jnp.zeros_like(q)
        qs_ref[0:tq, :] = jnp.where(lane < ATTN_QK_DIM, q, zero)
        qs_ref[tq:2 * tq, :] = jnp.where(lane >= ATTN_QK_DIM, q, zero)
        m_ref[...] = jnp.full(m_ref.shape, -jnp.inf, F32)
        l_ref[...] = jnp.zeros(l_ref.shape, F32)
        acc_ref[...] = jnp.zeros(acc_ref.shape, F32)

    is_ctx_q = qi < ctx_len // tq

    def step(n_valid):
        n_sub = pl.cdiv(n_valid, ts)
        qs = qs_ref[...]
        qk = lambda j: lax.dot_general(qs, k_ref[0, j * ts:(j + 1) * ts, :], (((1,), (1,)), ((), ())),
                                       preferred_element_type=F32)
        m_run = m_ref[...]
        l_run = l_ref[...]
        acc = acc_ref[...]
        s_cur = qk(0)
        for j in range(n_sub):
            s_nxt = qk(j + 1) if j + 1 < n_sub else None
            if (j + 1) * ts > n_valid:
                col = j * ts + lax.broadcasted_iota(jnp.int32, s_cur.shape, 1)
                s_cur = jnp.where(col < n_valid, s_cur, -jnp.inf)
            tiles = [s_cur[:, LANES * t:LANES * (t + 1)] for t in range(ts // LANES)]
            mx = tiles[0]
            for t in tiles[1:]:
                mx = jnp.maximum(mx, t)
            m_new = jnp.maximum(m_run, jnp.max(mx, axis=1, keepdims=True))
            alpha = jnp.exp2(m_run - m_new)
            ps = [jnp.exp2(t - m_new) for t in tiles]
            psum = ps[0]
            for pt in ps[1:]:
                psum = psum + pt
            l_run = alpha * l_run + psum
            p = jnp.concatenate(ps, axis=1).astype(BF16)
            acc = alpha * acc + jnp.dot(p, v_ref[0, j * ts:(j + 1) * ts, :], preferred_element_type=F32)
            m_run = m_new
            s_cur = s_nxt
        m_ref[...] = m_run
        l_ref[...] = l_run
        acc_ref[...] = acc

    @pl.when(jnp.logical_and(is_ctx_q, ki == 0))
    def _ctx_step():
        step(ctx_len)

    @pl.when(jnp.logical_not(is_ctx_q))
    def _lat_step():
        step(tk)

    @pl.when(ki == nk - 1)
    def _finish():
        lp = lam_ref[...]
        lam = (jnp.exp(jnp.sum(lp[0:1] * lp[1:2], axis=1, keepdims=True))
               - jnp.exp(jnp.sum(lp[2:3] * lp[3:4], axis=1, keepdims=True)) + lam_init)
        l = jnp.sum(l_ref[...], axis=1, keepdims=True)
        o1 = acc_ref[0:tq, :] / l[0:tq]
        o2 = acc_ref[tq:2 * tq, :] / l[tq:2 * tq]
        o = o1 - lam * o2
        ms = jnp.mean(o * o, axis=-1, keepdims=True)
        y = o * lax.rsqrt(ms + SUBLN_EPS) * sg_ref[...]
        o_ref[0] = (y * (1.0 - lam_init)).astype(BF16)


def _attention(q, k, v, lam_p, subln_g, lam_init, ctx_len, tq, tk):
    b, ta, _ = q.shape
    return pl.pallas_call(
        functools.partial(_attn_kernel, tq=tq, tk=tk, ts=_pick_tile(tk, (256, LANES)), ctx_len=ctx_len,
                          lam_init=lam_init),
        grid=(b, ATTN_HEADS, ta // tq, ta // tk),
        in_specs=[pl.BlockSpec((4, ATTN_QK_DIM), lambda bb, h, i, j: (0, 0)),
                  pl.BlockSpec((1, ATTN_V_DIM), lambda bb, h, i, j: (0, 0)),
                  pl.BlockSpec((1, tq, LANES), lambda bb, h, i, j: (bb, i, h)),
                  pl.BlockSpec((1, tk, LANES), lambda bb, h, i, j: (bb, j, h)),
                  pl.BlockSpec((1, tk, LANES), lambda bb, h, i, j: (bb, j, h))],
        out_specs=pl.BlockSpec((1, tq, LANES), lambda bb, h, i, j: (bb, i, h)),
        out_shape=jax.ShapeDtypeStruct((b, ta, ATTN_WIDTH), BF16),
        scratch_shapes=[pltpu.VMEM((2 * tq, LANES), BF16),
                        pltpu.VMEM((2 * tq, LANES), F32),
                        pltpu.VMEM((2 * tq, LANES), F32),
                        pltpu.VMEM((2 * tq, ATTN_V_DIM), F32)],
        compiler_params=_cparams(("parallel", "parallel", "parallel", "arbitrary")),
        name="diff_attention",
    )(lam_p, subln_g, q, k, v)


def _prep_kernel(*refs, tt, ctx_len, ta, has_vres):
    (pr_ref, pk_ref, pv_ref, hpr_ref, hpk_ref, hpv_ref, hnr_ref, hnk_ref, hnv_ref, lo_ref,
     mu_ref, w0_ref, w2_ref, a0_ref, a2_ref, g2_ref, kk_ref, ka_ref, seg_ref) = refs[:19]
    n_in = 19
    if has_vres:
        vf_ref, v0_ref, v1_ref, v2_ref = refs[19:23]
        n_in = 23
    (r_out, v_out, kk_out, g_out, lwf_out, lwb_out, bf_out, bb_out, kdf_out, kdb_out) = refs[n_in:]

    i = pl.program_id(1)
    row = lax.broadcasted_iota(jnp.int32, (tt, 1), 0)
    gidx = i * tt + row
    no_prev = jnp.logical_or(gidx == 0, gidx == ctx_len)
    no_next = jnp.logical_or(gidx == ctx_len - 1, gidx == ta - 1)

    def shift_mix(p_ref, hp_ref, hn_ref, j):
        p = p_ref[0]
        pp = jnp.where(row == 0, hp_ref[0, 7:8, :], pltpu.roll(p, 1, 0))
        pp = jnp.where(no_prev, 0.0, pp)
        pn = jnp.where(row == tt - 1, hn_ref[0, 0:1, :], pltpu.roll(p, tt - 1, 0))
        pn = jnp.where(no_next, 0.0, pn)
        return p + mu_ref[2 * j:2 * j + 1, :] * (pp - p) + mu_ref[2 * j + 1:2 * j + 2, :] * (pn - p)

    r = shift_mix(pr_ref, hpr_ref, hnr_ref, 0)
    k = shift_mix(pk_ref, hpk_ref, hnk_ref, 1)
    v = shift_mix(pv_ref, hpv_ref, hnv_ref, 2)
    if has_vres:
        mix = jax.nn.sigmoid(v0_ref[...] + _bdot(_bdot(v, v1_ref[...]), v2_ref[...]))
        v = v + (vf_ref[0] - v) * mix
    lo = lo_ref[0]
    gd = lo[:, 2 * DECAY_LORA + 2 * AAA_LORA:2 * DECAY_LORA + 2 * AAA_LORA + GATE_LORA]
    g_out[0] = _bdot(jax.nn.sigmoid(gd), g2_ref[...])
    kk = k * kk_ref[...]
    ss = _dot_exact_rhs(kk * kk, seg_ref[...])
    kk = kk * lax.rsqrt(jnp.maximum(ss, 1e-24))
    r_out[0] = r
    v_out[0] = v
    kk_out[0] = kk
    lw_outs = (lwf_out, lwb_out)
    b_outs = (bf_out, bb_out)
    kd_outs = (kdf_out, kdb_out)
    for d in range(2):
        wd = lo[:, d * DECAY_LORA:(d + 1) * DECAY_LORA]
        wl = w0_ref[d:d + 1, :] + _bdot(jnp.tanh(wd), w2_ref[d])
        lw_outs[d][0] = -jax.nn.sigmoid(wl) * math.exp(-0.5)
        ad = lo[:, 2 * DECAY_LORA + d * AAA_LORA:2 * DECAY_LORA + (d + 1) * AAA_LORA]
        a = jax.nn.sigmoid(a0_ref[d:d + 1, :] + _bdot(ad, a2_ref[d]))
        b_outs[d][0] = a * kk
        kd_outs[d][0] = k * (1.0 + (a - 1.0) * ka_ref[...])


def _rwkv_prepare(pr, pk, pv, lora, mu, w0, w2, a0, a2, g2, k_k, k_a, seg, vres, ctx_len, tt):
    b, ta, w = pr.shape
    nb8 = ta // 8
    tok = pl.BlockSpec((1, tt, w), lambda bb, i: (bb, i, 0))
    prev = pl.BlockSpec((1, 8, w), lambda bb, i: (bb, jnp.maximum(i * (tt // 8) - 1, 0), 0))
    nxt = pl.BlockSpec((1, 8, w), lambda bb, i: (bb, jnp.minimum((i + 1) * (tt // 8), nb8 - 1), 0))
    full = lambda a: pl.BlockSpec(a.shape, lambda bb, i, _n=a.ndim: (0,) * _n)
    consts = [mu.reshape(6, w), w0, w2, a0, a2, g2, k_k.reshape(1, w), k_a.reshape(1, w), seg]
    ins = [pr, pk, pv, pr, pk, pv, pr, pk, pv, lora] + consts
    specs = [tok] * 3 + [prev] * 3 + [nxt] * 3 \
        + [pl.BlockSpec((1, tt, LORA_PAD), lambda bb, i: (bb, i, 0))] + [full(a) for a in consts]
    if vres is not None:
        v_first, v0, v1, v2 = vres
        extra = [v0.reshape(1, w), v1, v2]
        ins += [v_first] + extra
        specs += [tok] + [full(a) for a in extra]
    return pl.pallas_call(
        functools.partial(_prep_kernel, tt=tt, ctx_len=ctx_len, ta=ta, has_vres=vres is not None),
        grid=(b, ta // tt),
        in_specs=specs,
        out_specs=[tok] * 10,
        out_shape=[jax.ShapeDtypeStruct((b, ta, w), F32)] * 10,
        compiler_params=_cparams(("parallel", "parallel")),
        name="rwkv_prepare",
    )(*ins)


_NN = (((2,), (1,)), ((0,), (0,)))
_NT = (((2,), (2,)), ((0,), (0,)))
_TN = (((1,), (1,)), ((0,), (0,)))


def _bmm(a, b, dims=_NN):
    return lax.dot_general(a.astype(BF16), b.astype(BF16), dims, preferred_element_type=F32)


def _bmm3(a, b):
    ah, al = _split(a)
    bh, bl = _split(b)
    mm = lambda x, y: lax.dot_general(x, y, _NN, preferred_element_type=F32)
    return mm(ah, bh) + mm(al, bh) + mm(ah, bl)


def _scan_kernel(rf_ref, vf_ref, kkf_ref, lwf_ref, btf_ref, kdf_ref,
                 rb_ref, vb_ref, kkb_ref, lwb_ref, btb_ref, kdb_ref,
                 yf_ref, yb_ref, st_ref):
    s = pl.program_id(1)
    c = CHUNK
    n = RWKV_HEAD

    @pl.when(s == 0)
    def _init():
        st_ref[...] = jnp.zeros(st_ref.shape, F32)

    t_i = lax.broadcasted_iota(jnp.int32, (c, c), 0)
    j_i = lax.broadcasted_iota(jnp.int32, (c, c), 1)
    t2 = lax.broadcasted_iota(jnp.int32, (2 * c, 2 * c), 0)
    j2 = lax.broadcasted_iota(jnp.int32, (2 * c, 2 * c), 1)
    tt2 = t2 & (c - 1)
    jj2 = j2 & (c - 1)
    eye = jnp.where(t_i == j_i, 1.0, 0.0)
    blk_bits = int(math.log2(INV_BLOCK))
    same_blk = (t_i >> blk_bits) == (j_i >> blk_bits)
    heads = lambda z: [z[:, h * n:(h + 1) * n] for h in range(RWKV_HEADS)]
    dirs = ((rf_ref, vf_ref, kkf_ref, lwf_ref, btf_ref, kdf_ref, False),
            (rb_ref, vb_ref, kkb_ref, lwb_ref, btb_ref, kdb_ref, True))
    at_l, rt_l, v_l, bk_l, pt_l, m_l = [], [], [], [], [], []
    for r_ref, v_ref, kk_ref, lw_ref, bt_ref, kd_ref, rev in dirs:
        incl = (j_i >= t_i) if rev else (j_i <= t_i)
        tri = jnp.where(incl, 1.0, 0.0).astype(BF16)
        before = (jj2 > tt2) if rev else (jj2 < tt2)
        mask_m = jnp.logical_or(before, jnp.logical_and(t2 >= c, jj2 == tt2))
        lw = lw_ref[0]
        cs = _dot_exact_lhs(tri, lw)
        tot = cs[0:1, :] if rev else cs[c - 1:c, :]
        p_inv = jnp.exp(-cs)
        p_rest = jnp.exp(tot - cs)
        bt = bt_ref[0]
        kd = kd_ref[0]
        rt = heads(r_ref[0] * jnp.exp(cs))
        at = heads(-kk_ref[0] * jnp.exp(cs - lw))
        bti = heads(bt * p_inv)
        kti = heads(kd * p_inv)
        bp = heads(bt * p_rest)
        kp = heads(kd * p_rest)
        x1 = jnp.stack([jnp.concatenate([a, b], axis=0) for a, b in zip(at, rt)])
        x2 = jnp.stack([jnp.concatenate([a, b], axis=0) for a, b in zip(bti, kti)])
        m_l.append(jnp.where(mask_m[None], _bmm(x1, x2, _NT), 0.0))
        at_l += at
        rt_l += rt
        v_l += heads(v_ref[0])
        bk_l += [jnp.concatenate([a, b], axis=0) for a, b in zip(bp, kp)]
        pt_l += heads(jnp.exp(tot))
    m = jnp.concatenate(m_l, axis=0)
    at_s = jnp.stack(at_l)
    rt_s = jnp.stack(rt_l)
    v_s = jnp.stack(v_l)
    bk_s = jnp.stack(bk_l)
    pt_s = jnp.stack(pt_l)

    a_full = m[:, 0:c, 0:c]
    d_pow = jnp.where(same_blk[None], a_full, 0.0)
    l_off = a_full - d_pow
    t_diag = eye[None] + d_pow
    for _ in range(blk_bits - 1):
        d_pow = _bmm3(d_pow, d_pow)
        t_diag = t_diag + _bmm3(d_pow, t_diag)
    n_mat = _bmm3(t_diag, l_off)
    x_mat = t_diag + _bmm3(n_mat, t_diag)
    tinv = x_mat + _bmm3(_bmm3(n_mat, n_mat), x_mat)

    st = st_ref[...]
    u = _bmm(tinv, _bmm(m[:, 0:c, c:2 * c], v_s) + _bmm(at_s, st, _NT))
    uv = jnp.concatenate([u, v_s], axis=1)
    y = _bmm(m[:, c:2 * c, :], uv) + _bmm(rt_s, st, _NT)
    st_ref[...] = pt_s * st + _bmm(uv, bk_s, _TN)
    for h in range(RWKV_HEADS):
        yf_ref[0, :, h * n:(h + 1) * n] = y[h]
        yb_ref[0, :, h * n:(h + 1) * n] = y[RWKV_HEADS + h]


def _rwkv_scan(r, v, kk, lw_f, lw_b, bt_f, bt_b, kd_f, kd_b, ctx_len):
    b, ta, w = r.shape
    nc = ta // CHUNK
    nctx = ctx_len // CHUNK

    def rev_idx(s):
        return jnp.where(s < nctx, nctx - 1 - s, nc - 1 - s + nctx)

    fwd = pl.BlockSpec((1, CHUNK, w), lambda bb, s: (bb, s, 0))
    bwd = pl.BlockSpec((1, CHUNK, w), lambda bb, s: (bb, rev_idx(s), 0))
    return pl.pallas_call(
        _scan_kernel,
        grid=(b, nc),
        in_specs=[fwd] * 6 + [bwd] * 6,
        out_specs=[fwd, bwd],
        out_shape=[jax.ShapeDtypeStruct((b, ta, w), F32)] * 2,
        scratch_shapes=[pltpu.VMEM((2 * RWKV_HEADS, RWKV_HEAD, RWKV_HEAD), F32)],
        compiler_params=_cparams(("parallel", "arbitrary")),
        name="rwkv_scan",
    )(r, v, kk, lw_f, bt_f, kd_f, r, v, kk, lw_b, bt_b, kd_b)


def _postmix_kernel(x_ref, att_ref, yf_ref, yb_ref, r_ref, v_ref, kdf_ref, kdb_ref, g_ref,
                    bgt_ref, cgt_ref, wo_ref, lnw_ref, lnb_ref, rk_ref, seg_ref, o_ref, *, tm, ctx_len):
    i = pl.program_id(1)
    is_ctx = _is_ctx_rows(i, tm, ctx_len)
    seg = seg_ref[...]
    y = yf_ref[0] + yb_ref[0]
    inv_n = 1.0 / RWKV_HEAD
    mean = _dot_exact_rhs(y, seg) * inv_n
    dy = y - mean
    var = _dot_exact_rhs(dy * dy, seg) * inv_n
    o = dy * lax.rsqrt(var + LNX_EPS) * lnw_ref[...] + lnb_ref[...]
    r = r_ref[0]
    v = v_ref[0]
    rrk = r * rk_ref[...]
    o = o + _dot_exact_rhs(rrk * kdf_ref[0], seg) * v
    o = o + _dot_exact_rhs(rrk * kdb_ref[0], seg) * v
    rw = o * g_ref[0]
    mixed = (jnp.dot(att_ref[0], wo_ref[0:ATTN_WIDTH, :], preferred_element_type=F32)
             + jnp.dot(rw.astype(BF16), wo_ref[ATTN_WIDTH:, :], preferred_element_type=F32))
    gt = jnp.where(is_ctx, cgt_ref[...], bgt_ref[0])
    o_ref[0] = x_ref[0] + gt * mixed


def _post_mix(xa, att, y_f, y_b, r, v, kd_f, kd_b, g, modb, modc, w_out, lnw, lnb, r_k, seg, ctx_len, tm):
    b, ta, d = xa.shape
    w = RWKV_WIDTH
    tok = lambda ww: pl.BlockSpec((1, tm, ww), lambda bb, i: (bb, i, 0))
    full = lambda a: pl.BlockSpec(a.shape, lambda bb, i, _n=a.ndim: (0,) * _n)
    consts = [w_out, lnw.reshape(1, w), lnb.reshape(1, w), r_k.reshape(1, w), seg]
    return pl.pallas_call(
        functools.partial(_postmix_kernel, tm=tm, ctx_len=ctx_len),
        grid=(b, ta // tm),
        in_specs=[tok(d), tok(ATTN_WIDTH)] + [tok(w)] * 7
        + [pl.BlockSpec((1, 1, d), lambda bb, i: (bb, 0, 2)), pl.BlockSpec((1, d), lambda bb, i: (0, 2))]
        + [full(a) for a in consts],
        out_specs=tok(d),
        out_shape=jax.ShapeDtypeStruct((b, ta, d), F32),
        compiler_params=_cparams(("parallel", "parallel")),
        name="rwkv_finish_out_proj",
    )(xa, att, y_f, y_b, r, v, kd_f, kd_b, g, modb, modc, *consts)


def _ffn_kernel(x_ref, g_ref, bsh_ref, bsc_ref, csh_ref, csc_ref, bgt_ref, cgt_ref,
                w1_ref, w3_ref, w2_ref, o_ref, *, tm, ctx_len):
    i = pl.program_id(1)
    is_ctx = _is_ctx_rows(i, tm, ctx_len)
    x = x_ref[0]
    h = _modnorm(x, g_ref[...], bsc_ref[0], bsh_ref[0], csc_ref[...], csh_ref[...], is_ctx).astype(BF16)
    a = jnp.dot(h, w1_ref[...], preferred_element_type=F32)
    bb = jnp.dot(h, w3_ref[...], preferred_element_type=F32)
    act = (a * jax.nn.sigmoid(a) * bb).astype(BF16)
    out = jnp.dot(act, w2_ref[...], preferred_element_type=F32)
    gt = jnp.where(is_ctx, cgt_ref[...], bgt_ref[0])
    o_ref[0] = x + gt * out


def _dense_ffn(xa, g2, modb, modc, w1, w3, w2, ctx_len, tm):
    b, ta, d = xa.shape
    tok = pl.BlockSpec((1, tm, d), lambda bb, i: (bb, i, 0))
    full = lambda a: pl.BlockSpec(a.shape, lambda bb, i, _n=a.ndim: (0,) * _n,
                                  pipeline_mode=pl.Buffered(1))
    return pl.pallas_call(
        functools.partial(_ffn_kernel, tm=tm, ctx_len=ctx_len),
        grid=(b, ta // tm),
        in_specs=[tok, pl.BlockSpec((1, d), lambda bb, i: (0, 0))] + _mod_specs(d, 3, 4)
        + [pl.BlockSpec((1, 1, d), lambda bb, i: (bb, 0, 5)), pl.BlockSpec((1, d), lambda bb, i: (0, 5))]
        + [full(w1), full(w3), full(w2)],
        out_specs=tok,
        out_shape=jax.ShapeDtypeStruct((b, ta, d), F32),
        compiler_params=_cparams(("parallel", "parallel")),
        name="norm2_dense_swiglu",
    )(xa, g2, modb, modb, modc, modc, modb, modc, w1, w3, w2)


def _moe_kernel(x_ref, g_ref, bsh_ref, bsc_ref, csh_ref, csc_ref, bgt_ref, cgt_ref, rw_ref,
                w1_ref, w3_ref, w2_ref, o_ref, h_ref, comb_ref, acc_ref, *, tm, ctx_len):
    i = pl.program_id(1)
    e = pl.program_id(2)
    f = pl.program_id(3)
    last = jnp.logical_and(e == pl.num_programs(2) - 1, f == pl.num_programs(3) - 1)

    @pl.when(jnp.logical_and(e == 0, f == 0))
    def _route():
        is_ctx = _is_ctx_rows(i, tm, ctx_len)
        h = _modnorm(x_ref[0], g_ref[...], bsc_ref[0], bsh_ref[0], csc_ref[...], csh_ref[...], is_ctx)
        h_ref[...] = h.astype(BF16)
        hh, hl = _split(h)
        wh, wl = _split(rw_ref[...])
        logits = (jnp.dot(hh, wh, preferred_element_type=F32) + jnp.dot(hl, wh, preferred_element_type=F32)
                  + jnp.dot(hh, wl, preferred_element_type=F32))
        lane = lax.broadcasted_iota(jnp.int32, logits.shape, 1).astype(F32)
        neg = -jnp.inf
        logits = jnp.where(lane < N_EXPERTS, logits, neg)
        v1 = jnp.max(logits, axis=1, keepdims=True)
        i1 = jnp.min(jnp.where(logits == v1, lane, float(LANES)), axis=1, keepdims=True)
        rest = jnp.where(lane == i1, neg, logits)
        v2 = jnp.max(rest, axis=1, keepdims=True)
        i2 = jnp.min(jnp.where(rest == v2, lane, float(LANES)), axis=1, keepdims=True)
        e2 = jnp.exp(v2 - v1)
        den = 1.0 + e2
        comb_ref[...] = jnp.where(lane == i1, 1.0 / den, 0.0) + jnp.where(lane == i2, e2 / den, 0.0)
        acc_ref[...] = jnp.zeros(acc_ref.shape, F32)

    h = h_ref[...]
    lane = lax.broadcasted_iota(jnp.int32, comb_ref.shape, 1)
    gate = jnp.sum(jnp.where(lane == e, comb_ref[...], 0.0), axis=1, keepdims=True)
    a = jnp.dot(h, w1_ref[0], preferred_element_type=F32)
    bb = jnp.dot(h, w3_ref[0], preferred_element_type=F32)
    act = (a * jax.nn.sigmoid(a) * bb * gate).astype(BF16)
    acc_ref[...] += jnp.dot(act, w2_ref[0], preferred_element_type=F32)

    @pl.when(last)
    def _finish():
        is_ctx = _is_ctx_rows(i, tm, ctx_len)
        gt = jnp.where(is_ctx, cgt_ref[...], bgt_ref[0])
        o_ref[0] = x_ref[0] + gt * acc_ref[...]


def _moe_ffn(xa, g2, modb, modc, router_p, w1, w3, w2, ctx_len, tm, nf):
    b, ta, d = xa.shape
    ne, _, ff = w1.shape
    tf = ff // nf
    tok = pl.BlockSpec((1, tm, d), lambda bb, i, e, f: (bb, i, 0))
    return pl.pallas_call(
        functools.partial(_moe_kernel, tm=tm, ctx_len=ctx_len),
        grid=(b, ta // tm, ne, nf),
        in_specs=[tok, pl.BlockSpec((1, d), lambda bb, i, e, f: (0, 0))] + _mod_specs(d, 3, 4)
        + [pl.BlockSpec((1, 1, d), lambda bb, i, e, f: (bb, 0, 5)),
           pl.BlockSpec((1, d), lambda bb, i, e, f: (0, 5)),
           pl.BlockSpec((d, LANES), lambda bb, i, e, f: (0, 0)),
           pl.BlockSpec((1, d, tf), lambda bb, i, e, f: (e, 0, f)),
           pl.BlockSpec((1, d, tf), lambda bb, i, e, f: (e, 0, f)),
           pl.BlockSpec((1, tf, d), lambda bb, i, e, f: (e, f, 0))],
        out_specs=tok,
        out_shape=jax.ShapeDtypeStruct((b, ta, d), F32),
        scratch_shapes=[pltpu.VMEM((tm, d), BF16), pltpu.VMEM((tm, LANES), F32), pltpu.VMEM((tm, d), F32)],
        compiler_params=_cparams(("parallel", "parallel", "arbitrary", "arbitrary")),
        name="norm2_moe_swiglu",
    )(xa, g2, modb, modb, modc, modc, modb, modc, router_p, w1, w3, w2)


def _final_kernel(x_ref, g_ref, o_ref):
    x = x_ref[0]
    ms = jnp.mean(x * x, axis=-1, keepdims=True)
    o_ref[0] = x * lax.rsqrt(ms + NORM_EPS) * g_ref[...]


def _final_norm(xa, g, ctx_len, seq, tm):
    b, _, d = xa.shape
    off = ctx_len // tm
    return pl.pallas_call(
        _final_kernel,
        grid=(b, seq // tm),
        in_specs=[pl.BlockSpec((1, tm, d), lambda bb, i: (bb, i + off, 0)),
                  pl.BlockSpec((1, d), lambda bb, i: (0, 0))],
        out_specs=pl.BlockSpec((1, tm, d), lambda bb, i: (bb, i, 0)),
        out_shape=jax.ShapeDtypeStruct((b, seq, d), F32),
        compiler_params=_cparams(("parallel", "parallel")),
        name="final_norm",
    )(xa, g)


def _rope_tables(seq, ctx_len):
    axis_dim = ATTN_QK_DIM // 2
    inv_freq = ROPE_THETA ** (-jnp.arange(0, axis_dim, 2, dtype=F32) / axis_dim)
    t = jnp.arange(seq)
    pos = jnp.stack([(t // GRID_W).astype(F32), (t % GRID_W).astype(F32)], axis=1)
    lane = np.arange(LANES)
    axis = (lane % ATTN_QK_DIM) // axis_dim
    freq = lane % (axis_dim // 2)
    sign = np.where((lane % axis_dim) < axis_dim // 2, -1.0, 1.0).astype(np.float32)
    ang = pos[:, axis] * inv_freq[freq][None, :]
    cos_t = jnp.concatenate([jnp.ones((ctx_len, LANES), F32), jnp.cos(ang)], axis=0)
    sin_t = jnp.concatenate([jnp.zeros((ctx_len, LANES), F32), jnp.sin(ang) * sign[None, :]], axis=0)
    return cos_t, sin_t


def _pick_tile(n, candidates):
    for c in candidates:
        if n % c == 0:
            return c
    raise ValueError(f"no tile in {candidates} divides {n}")


def kernel(x, c, ctx, c_ctx, ada_w, ada_b, norm1_g, norm2_g, w_in, w_out, diff_lambda, subln_g, rwkv_mu, rwkv_w0, rwkv_w2, rwkv_a0, rwkv_a2, rwkv_v0, rwkv_v1, rwkv_v2, rwkv_g2, rwkv_k_k, rwkv_k_a, rwkv_r_k, lnx_w, lnx_b, ffn_w1, ffn_w3, ffn_w2, router_w, exp_w1, exp_w3, exp_w2, final_g):
    b, seq, d = x.shape
    ctx_len = ctx.shape[1]
    depth = ada_w.shape[0]
    ta = ctx_len + seq
    assert b + 1 <= 8 and seq % GRID_W == 0 and ctx_len % CHUNK == 0 and ta % CHUNK == 0

    tm = _pick_tile(math.gcd(ctx_len, ta), (256, 128))
    tq = tm
    tk = _pick_tile(ta, (1280, 640, 256, 128))
    assert ctx_len <= tk
    tm_moe = _pick_tile(ta, (640, 256, 128))

    xa = jnp.concatenate([ctx, x], axis=1)
    cvec = jnp.zeros((8, d), F32).at[:b].set(c).at[b].set(c_ctx)
    mod = _modulation(cvec, ada_w, ada_b)
    cos_t, sin_t = _rope_tables(seq, ctx_len)
    seg = jnp.asarray(np.kron(np.eye(RWKV_HEADS), np.ones((RWKV_HEAD, RWKV_HEAD))), BF16)
    n_in = w_in.shape[2]
    n_in_p = 3 * ATTN_WIDTH + 3 * RWKV_WIDTH + LORA_PAD
    router_p = jnp.pad(router_w, ((0, 0), (0, 0), (0, LANES - router_w.shape[2])))

    v_first = None
    for l in range(depth):
        lam_init = 0.8 - 0.6 * math.exp(-0.3 * l)
        modb = mod[l, :b].reshape(b, 1, 6 * d)
        modc = mod[l, b:b + 1]
        w_in_p = jnp.pad(w_in[l], ((0, 0), (0, n_in_p - n_in))).astype(BF16)
        q, k, v, pr, pk, pv, lora = _in_proj(xa, norm1_g[l].reshape(1, d), modb, modc, w_in_p,
                                             cos_t, sin_t, ctx_len, tm)
        att = _attention(q, k, v, diff_lambda[l], subln_g[l].reshape(1, ATTN_V_DIM), lam_init,
                         ctx_len, tq, tk)
        vres = None if l == 0 else (v_first, rwkv_v0[l - 1], rwkv_v1[l - 1], rwkv_v2[l - 1])
        r, vv, kk, g, lw_f, lw_b, bt_f, bt_b, kd_f, kd_b = _rwkv_prepare(
            pr, pk, pv, lora, rwkv_mu[l], rwkv_w0[l], rwkv_w2[l], rwkv_a0[l], rwkv_a2[l], rwkv_g2[l],
            rwkv_k_k[l], rwkv_k_a[l], seg, vres, ctx_len, tm)
        if l == 0:
            v_first = vv
        y_f, y_b = _rwkv_scan(r, vv, kk, lw_f, lw_b, bt_f, bt_b, kd_f, kd_b, ctx_len)
        xa = _post_mix(xa, att, y_f, y_b, r, vv, kd_f, kd_b, g, modb, modc, w_out[l].astype(BF16),
                       lnx_w[l], lnx_b[l], rwkv_r_k[l], seg, ctx_len, tm)
        i = l // 2
        if l % 2 == 0:
            xa = _dense_ffn(xa, norm2_g[l].reshape(1, d), modb, modc, ffn_w1[i].astype(BF16),
                            ffn_w3[i].astype(BF16), ffn_w2[i].astype(BF16), ctx_len, tm)
        else:
            xa = _moe_ffn(xa, norm2_g[l].reshape(1, d), modb, modc, router_p[i],
                          exp_w1[i].astype(BF16), exp_w3[i].astype(BF16), exp_w2[i].astype(BF16),
                          ctx_len, tm_moe, 2)
    return _final_norm(xa, final_g.reshape(1, d), ctx_len, seq, tm)
```

```python
import functools
import math

import numpy as np
import jax
import jax.numpy as jnp
from jax import lax
from jax.experimental import pallas as pl
from jax.experimental.pallas import tpu as pltpu

F32 = jnp.float32
BF16 = jnp.bfloat16

ATTN_HEADS = 4
ATTN_QK_DIM = 64
ATTN_V_DIM = 128
ATTN_WIDTH = ATTN_HEADS * ATTN_V_DIM
RWKV_HEADS = 8
RWKV_HEAD = 64
RWKV_WIDTH = RWKV_HEADS * RWKV_HEAD
DECAY_LORA = 32
AAA_LORA = 32
GATE_LORA = 96
LORA_PAD = 256
N_EXPERTS = 8
GRID_W = 64
ROPE_THETA = 10000.0
NORM_EPS = 1e-6
SUBLN_EPS = 1e-5
LNX_EPS = 64e-5
LANES = 128
CHUNK = 64
INV_BLOCK = 16
QK_LOOKAHEAD = 2
V_ONES = 16
V_EXT = ATTN_V_DIM + V_ONES
VMEM_LIMIT = 56 * 1024 * 1024
Q_SCALE = ATTN_QK_DIM ** -0.5 * math.log2(math.e)


def _cparams(sem, vmem=VMEM_LIMIT, flags=None):
    return pltpu.CompilerParams(dimension_semantics=sem, vmem_limit_bytes=vmem, flags=flags)


def _bdot(a, b):
    return jnp.dot(a.astype(BF16), b.astype(BF16), preferred_element_type=F32)


def _bdot_nt(a, b):
    return lax.dot_general(a.astype(BF16), b.astype(BF16), (((1,), (1,)), ((), ())),
                           preferred_element_type=F32)


def _bdot_tn(a, b):
    return lax.dot_general(a.astype(BF16), b.astype(BF16), (((0,), (0,)), ((), ())),
                           preferred_element_type=F32)


def _split(a):
    hi = a.astype(BF16)
    lo = (a - hi.astype(F32)).astype(BF16)
    return hi, lo


def _dot3(a, b):
    ah, al = _split(a)
    bh, bl = _split(b)
    return (jnp.dot(ah, bh, preferred_element_type=F32) + jnp.dot(al, bh, preferred_element_type=F32)
            + jnp.dot(ah, bl, preferred_element_type=F32))


def _dot_exact_rhs(a, b_bf16):
    hi, lo = _split(a)
    return (jnp.dot(hi, b_bf16, preferred_element_type=F32)
            + jnp.dot(lo, b_bf16, preferred_element_type=F32))


def _dot_exact_lhs(a_bf16, b):
    hi, lo = _split(b)
    return (jnp.dot(a_bf16, hi, preferred_element_type=F32)
            + jnp.dot(a_bf16, lo, preferred_element_type=F32))


def _modnorm(x, g, sc_b, sh_b, sc_c, sh_c, is_ctx):
    ms = jnp.mean(x * x, axis=-1, keepdims=True)
    y = x * lax.rsqrt(ms + NORM_EPS) * g
    sc = jnp.where(is_ctx, sc_c, sc_b)
    sh = jnp.where(is_ctx, sh_c, sh_b)
    return y * (1.0 + sc) + sh


def _is_ctx_rows(tile_idx, tm, n_lat):
    row = tile_idx * tm + lax.broadcasted_iota(jnp.int32, (tm, 1), 0)
    return row >= n_lat


def _mod_specs(d, k_sh, k_sc):
    return [
        pl.BlockSpec((1, 1, d), lambda b, i, *_: (b, 0, k_sh)),
        pl.BlockSpec((1, 1, d), lambda b, i, *_: (b, 0, k_sc)),
        pl.BlockSpec((1, d), lambda b, i, *_: (0, k_sh)),
        pl.BlockSpec((1, d), lambda b, i, *_: (0, k_sc)),
    ]


def _mod_kernel(c_ref, w_ref, b_ref, o_ref):
    c = c_ref[...]
    s = c * jax.nn.sigmoid(c)
    o_ref[0] = _bdot(s, w_ref[0]) + b_ref[0]


def _modulation(cvec, ada_w, ada_b):
    depth, d, n = ada_w.shape
    tn = 2048
    return pl.pallas_call(
        _mod_kernel,
        grid=(depth, n // tn),
        in_specs=[pl.BlockSpec((8, d), lambda l, j: (0, 0)),
                  pl.BlockSpec((1, d, tn), lambda l, j: (l, 0, j)),
                  pl.BlockSpec((1, 1, tn), lambda l, j: (l, 0, j))],
        out_specs=pl.BlockSpec((1, 8, tn), lambda l, j: (l, 0, j)),
        out_shape=jax.ShapeDtypeStruct((depth, 8, n), F32),
        compiler_params=_cparams(("parallel", "parallel")),
        name="adaln_mod",
    )(cvec, ada_w, ada_b.reshape(depth, 1, n))


def _inproj_kernel(x_ref, g_ref, bsh_ref, bsc_ref, csh_ref, csc_ref, w_ref, cos_ref, sin_ref,
                   q_ref, k_ref, v_ref, rr_ref, rk_ref, rv_ref, lo_ref, *, tm, n_lat):
    i = pl.program_id(1)
    is_ctx = _is_ctx_rows(i, tm, n_lat)
    h = _modnorm(x_ref[0], g_ref[...], bsc_ref[0], bsh_ref[0], csc_ref[...], csh_ref[...], is_ctx)
    p = jnp.dot(h.astype(BF16), w_ref[...], preferred_element_type=F32)
    cos_t = cos_ref[...]
    sin_t = sin_ref[...]
    lane = lax.broadcasted_iota(jnp.int32, (tm, LANES), 1)
    first = (lane & 31) < 16

    def rope(z):
        partner = jnp.where(first, pltpu.roll(z, LANES - 16, 1), pltpu.roll(z, 16, 1))
        return z * cos_t + partner * sin_t

    for hh in range(ATTN_HEADS):
        lo = LANES * hh
        q_ref[0, lo:lo + LANES, :] = jnp.transpose(rope(p[:, lo:lo + LANES]) * Q_SCALE).astype(BF16)
        k_ref[0, :, lo:lo + LANES] = rope(p[:, ATTN_WIDTH + lo:ATTN_WIDTH + lo + LANES]).astype(BF16)
        vo = 2 * ATTN_WIDTH + lo
        ve = V_EXT * hh
        v_ref[0, ve:ve + ATTN_V_DIM, :] = jnp.transpose(p[:, vo:vo + LANES]).astype(BF16)
        v_ref[0, ve + ATTN_V_DIM:ve + V_EXT, :] = jnp.ones((V_ONES, tm), BF16)
    o = 3 * ATTN_WIDTH
    rr_ref[0] = p[:, o:o + RWKV_WIDTH]
    rk_ref[0] = p[:, o + RWKV_WIDTH:o + 2 * RWKV_WIDTH]
    rv_ref[0] = p[:, o + 2 * RWKV_WIDTH:o + 3 * RWKV_WIDTH]
    o += 3 * RWKV_WIDTH
    lo_ref[0] = p[:, o:o + LORA_PAD]


def _in_proj(xa, g1, modb, modc, w_in_p, cos_t, sin_t, n_lat, tm):
    b, ta, d = xa.shape
    n = w_in_p.shape[1]
    tok = lambda w: pl.BlockSpec((1, tm, w), lambda bb, i: (bb, i, 0))
    feat = pl.BlockSpec((1, ATTN_WIDTH, tm), lambda bb, i: (bb, 0, i))
    feat_shape = jax.ShapeDtypeStruct((b, ATTN_WIDTH, ta), BF16)
    vext = pl.BlockSpec((1, ATTN_HEADS * V_EXT, tm), lambda bb, i: (bb, 0, i))
    vext_shape = jax.ShapeDtypeStruct((b, ATTN_HEADS * V_EXT, ta), BF16)
    outs = [feat_shape, jax.ShapeDtypeStruct((b, ta, ATTN_WIDTH), BF16), vext_shape] \
        + [jax.ShapeDtypeStruct((b, ta, RWKV_WIDTH), F32)] * 3 \
        + [jax.ShapeDtypeStruct((b, ta, LORA_PAD), F32)]
    return pl.pallas_call(
        functools.partial(_inproj_kernel, tm=tm, n_lat=n_lat),
        grid=(b, ta // tm),
        in_specs=[tok(d), pl.BlockSpec((1, d), lambda bb, i: (0, 0))] + _mod_specs(d, 0, 1)
        + [pl.BlockSpec((d, n), lambda bb, i: (0, 0)),
           pl.BlockSpec((tm, LANES), lambda bb, i: (i, 0)),
           pl.BlockSpec((tm, LANES), lambda bb, i: (i, 0))],
        out_specs=[feat, tok(ATTN_WIDTH), vext] + [tok(RWKV_WIDTH)] * 3 + [tok(LORA_PAD)],
        out_shape=outs,
        compiler_params=_cparams(("parallel", "parallel")),
        name="norm1_in_proj",
    )(xa, g1, modb, modb, modc, modc, w_in_p, cos_t, sin_t)


def _attn_kernel(lam_ref, sg_ref, q_ref, k_ref, v_ref, o_ref, qs_ref, m_ref, acc_ref,
                 *, tq, tk, ts, lam_init):
    ki = pl.program_id(3)
    nk = pl.num_programs(3)

    @pl.when(ki == 0)
    def _init():
        q = q_ref[0]
        feat = lax.broadcasted_iota(jnp.int32, (LANES, tq), 0)
        zero = jnp.zeros_like(q)
        qs_ref[:, 0:tq] = jnp.where(feat < ATTN_QK_DIM, q, zero)
        qs_ref[:, tq:2 * tq] = jnp.where(feat >= ATTN_QK_DIM, q, zero)
        m_ref[...] = jnp.full(m_ref.shape, -jnp.inf, F32)
        acc_ref[...] = jnp.zeros(acc_ref.shape, F32)

    n_sub = tk // ts
    qs = qs_ref[...]
    qk = lambda j: jnp.dot(k_ref[0, j * ts:(j + 1) * ts, :], qs, preferred_element_type=F32)
    m_run = m_ref[...]
    acc = acc_ref[...]
    pending = [qk(j) for j in range(min(QK_LOOKAHEAD, n_sub))]
    for j in range(n_sub):
        if j + QK_LOOKAHEAD < n_sub:
            pending.append(qk(j + QK_LOOKAHEAD))
        s_cur = pending.pop(0)
        m_new = jnp.maximum(m_run, jnp.max(s_cur, axis=0, keepdims=True))
        alpha = jnp.exp2(m_run - m_new)
        p = jnp.exp2(s_cur - m_new).astype(BF16)
        acc = alpha * acc + jnp.dot(v_ref[0, :, j * ts:(j + 1) * ts], p, preferred_element_type=F32)
        m_run = m_new
    m_ref[...] = m_run
    acc_ref[...] = acc

    @pl.when(ki == nk - 1)
    def _finish():
        lp = lam_ref[...]
        lam = (jnp.exp(jnp.sum(lp[0:1] * lp[1:2], axis=1, keepdims=True))
               - jnp.exp(jnp.sum(lp[2:3] * lp[3:4], axis=1, keepdims=True)) + lam_init)
        l = acc_ref[ATTN_V_DIM:ATTN_V_DIM + 1, :]
        o1 = acc_ref[0:ATTN_V_DIM, 0:tq] / l[:, 0:tq]
        o2 = acc_ref[0:ATTN_V_DIM, tq:2 * tq] / l[:, tq:2 * tq]
        o = o1 - lam * o2
        ms = jnp.mean(o * o, axis=0, keepdims=True)
        y = o * lax.rsqrt(ms + SUBLN_EPS) * sg_ref[...]
        o_ref[0] = (y * (1.0 - lam_init)).astype(BF16)


def _attention(q_t, k, v_t, lam_p, subln_g, lam_init, q0, nq, k0, nkeys, tq, tk):
    b = k.shape[0]
    qo, ko = q0 // tq, k0 // tk
    return pl.pallas_call(
        functools.partial(_attn_kernel, tq=tq, tk=tk, ts=_pick_tile(tk, (256, LANES)), lam_init=lam_init),
        grid=(b, ATTN_HEADS, nq // tq, nkeys // tk),
        in_specs=[pl.BlockSpec((4, ATTN_QK_DIM), lambda bb, h, i, j: (0, 0)),
                  pl.BlockSpec((ATTN_V_DIM, 1), lambda bb, h, i, j: (0, 0)),
                  pl.BlockSpec((1, LANES, tq), lambda bb, h, i, j: (bb, h, i + qo)),
                  pl.BlockSpec((1, tk, LANES), lambda bb, h, i, j: (bb, j + ko, h)),
                  pl.BlockSpec((1, V_EXT, tk), lambda bb, h, i, j: (bb, h, j + ko))],
        out_specs=pl.BlockSpec((1, ATTN_V_DIM, tq), lambda bb, h, i, j: (bb, h, i)),
        out_shape=jax.ShapeDtypeStruct((b, ATTN_WIDTH, nq), BF16),
        scratch_shapes=[pltpu.VMEM((LANES, 2 * tq), BF16),
                        pltpu.VMEM((1, 2 * tq), F32),
                        pltpu.VMEM((V_EXT, 2 * tq), F32)],
        compiler_params=_cparams(("parallel", "parallel", "parallel", "arbitrary")),
        name="diff_attention",
    )(lam_p, subln_g, q_t, k, v_t)


def _prep_kernel(*refs, tt, n_lat, ta, has_vres):
    (pr_ref, pk_ref, pv_ref, hpr_ref, hpk_ref, hpv_ref, hnr_ref, hnk_ref, hnv_ref, lo_ref,
     mu_ref, w0_ref, w2_ref, a0_ref, a2_ref, g2_ref, kk_ref, ka_ref, seg_ref) = refs[:19]
    n_in = 19
    if has_vres:
        vf_ref, v0_ref, v1_ref, v2_ref = refs[19:23]
        n_in = 23
    (r_out, v_out, kk_out, g_out, lwf_out, lwb_out, bf_out, bb_out, kdf_out, kdb_out) = refs[n_in:]

    i = pl.program_id(1)
    row = lax.broadcasted_iota(jnp.int32, (tt, 1), 0)
    gidx = i * tt + row
    no_prev = jnp.logical_or(gidx == 0, gidx == n_lat)
    no_next = jnp.logical_or(gidx == n_lat - 1, gidx == ta - 1)

    def shift_mix(p_ref, hp_ref, hn_ref, j):
        p = p_ref[0]
        pp = jnp.where(row == 0, hp_ref[0, 7:8, :], pltpu.roll(p, 1, 0))
        pp = jnp.where(no_prev, 0.0, pp)
        pn = jnp.where(row == tt - 1, hn_ref[0, 0:1, :], pltpu.roll(p, tt - 1, 0))
        pn = jnp.where(no_next, 0.0, pn)
        return p + mu_ref[2 * j:2 * j + 1, :] * (pp - p) + mu_ref[2 * j + 1:2 * j + 2, :] * (pn - p)

    r = shift_mix(pr_ref, hpr_ref, hnr_ref, 0)
    k = shift_mix(pk_ref, hpk_ref, hnk_ref, 1)
    v = shift_mix(pv_ref, hpv_ref, hnv_ref, 2)
    if has_vres:
        mix = jax.nn.sigmoid(v0_ref[...] + _bdot(_bdot(v, v1_ref[...]), v2_ref[...]))
        v = v + (vf_ref[0] - v) * mix
    lo = lo_ref[0]
    gd = lo[:, 2 * DECAY_LORA + 2 * AAA_LORA:2 * DECAY_LORA + 2 * AAA_LORA + GATE_LORA]
    g_out[0] = _bdot(jax.nn.sigmoid(gd), g2_ref[...])
    kk = k * kk_ref[...]
    ss = _dot_exact_rhs(kk * kk, seg_ref[...])
    kk = kk * lax.rsqrt(jnp.maximum(ss, 1e-24))
    r_out[0] = r
    v_out[0] = v
    kk_out[0] = kk
    lw_outs = (lwf_out, lwb_out)
    b_outs = (bf_out, bb_out)
    kd_outs = (kdf_out, kdb_out)
    for d in range(2):
        wd = lo[:, d * DECAY_LORA:(d + 1) * DECAY_LORA]
        wl = w0_ref[d:d + 1, :] + _bdot(jnp.tanh(wd), w2_ref[d])
        lw_outs[d][0] = -jax.nn.sigmoid(wl) * math.exp(-0.5)
        ad = lo[:, 2 * DECAY_LORA + d * AAA_LORA:2 * DECAY_LORA + (d + 1) * AAA_LORA]
        a = jax.nn.sigmoid(a0_ref[d:d + 1, :] + _bdot(ad, a2_ref[d]))
        b_outs[d][0] = a * kk
        kd_outs[d][0] = k * (1.0 + (a - 1.0) * ka_ref[...])


def _rwkv_prepare(pr, pk, pv, lora, mu, w0, w2, a0, a2, g2, k_k, k_a, seg, vres, n_lat, tt):
    b, ta, w = pr.shape
    nb8 = ta // 8
    tok = pl.BlockSpec((1, tt, w), lambda bb, i: (bb, i, 0))
    prev = pl.BlockSpec((1, 8, w), lambda bb, i: (bb, jnp.maximum(i * (tt // 8) - 1, 0), 0))
    nxt = pl.BlockSpec((1, 8, w), lambda bb, i: (bb, jnp.minimum((i + 1) * (tt // 8), nb8 - 1), 0))
    full = lambda a: pl.BlockSpec(a.shape, lambda bb, i, _n=a.ndim: (0,) * _n)
    consts = [mu.reshape(6, w), w0, w2, a0, a2, g2, k_k.reshape(1, w), k_a.reshape(1, w), seg]
    ins = [pr, pk, pv, pr, pk, pv, pr, pk, pv, lora] + consts
    specs = [tok] * 3 + [prev] * 3 + [nxt] * 3 \
        + [pl.BlockSpec((1, tt, LORA_PAD), lambda bb, i: (bb, i, 0))] + [full(a) for a in consts]
    if vres is not None:
        v_first, v0, v1, v2 = vres
        extra = [v0.reshape(1, w), v1, v2]
        ins += [v_first] + extra
        specs += [tok] + [full(a) for a in extra]
    return pl.pallas_call(
        functools.partial(_prep_kernel, tt=tt, n_lat=n_lat, ta=ta, has_vres=vres is not None),
        grid=(b, ta // tt),
        in_specs=specs,
        out_specs=[tok] * 10,
        out_shape=[jax.ShapeDtypeStruct((b, ta, w), F32)] * 10,
        compiler_params=_cparams(("parallel", "parallel")),
        name="rwkv_prepare",
    )(*ins)


_NN = (((2,), (1,)), ((0,), (0,)))
_NT = (((2,), (2,)), ((0,), (0,)))
_TN = (((1,), (1,)), ((0,), (0,)))


def _bmm(a, b, dims=_NN):
    return lax.dot_general(a.astype(BF16), b.astype(BF16), dims, preferred_element_type=F32)


def _bmm3(a, b):
    ah, al = _split(a)
    bh, bl = _split(b)
    mm = lambda x, y: lax.dot_general(x, y, _NN, preferred_element_type=F32)
    return mm(ah, bh) + mm(al, bh) + mm(ah, bl)


def _scan_kernel(rf_ref, vf_ref, kkf_ref, lwf_ref, btf_ref, kdf_ref,
                 rb_ref, vb_ref, kkb_ref, lwb_ref, btb_ref, kdb_ref,
                 yf_ref, yb_ref, st_ref):
    s = pl.program_id(1)
    c = CHUNK
    n = RWKV_HEAD

    @pl.when(s == 0)
    def _init():
        st_ref[...] = jnp.zeros(st_ref.shape, F32)

    t_i = lax.broadcasted_iota(jnp.int32, (c, c), 0)
    j_i = lax.broadcasted_iota(jnp.int32, (c, c), 1)
    t2 = lax.broadcasted_iota(jnp.int32, (2 * c, 2 * c), 0)
    j2 = lax.broadcasted_iota(jnp.int32, (2 * c, 2 * c), 1)
    tt2 = t2 & (c - 1)
    jj2 = j2 & (c - 1)
    eye = jnp.where(t_i == j_i, 1.0, 0.0)
    blk_bits = int(math.log2(INV_BLOCK))
    same_blk = (t_i >> blk_bits) == (j_i >> blk_bits)
    heads = lambda z: [z[:, h * n:(h + 1) * n] for h in range(RWKV_HEADS)]
    dirs = ((rf_ref, vf_ref, kkf_ref, lwf_ref, btf_ref, kdf_ref, False),
            (rb_ref, vb_ref, kkb_ref, lwb_ref, btb_ref, kdb_ref, True))
    at_l, rt_l, v_l, bk_l, pt_l, m_l = [], [], [], [], [], []
    for r_ref, v_ref, kk_ref, lw_ref, bt_ref, kd_ref, rev in dirs:
        incl = (j_i >= t_i) if rev else (j_i <= t_i)
        tri = jnp.where(incl, 1.0, 0.0).astype(BF16)
        before = (jj2 > tt2) if rev else (jj2 < tt2)
        mask_m = jnp.logical_or(before, jnp.logical_and(t2 >= c, jj2 == tt2))
        lw = lw_ref[0]
        cs = _dot_exact_lhs(tri, lw)
        tot = cs[0:1, :] if rev else cs[c - 1:c, :]
        p_inv = jnp.exp(-cs)
        p_rest = jnp.exp(tot - cs)
        bt = bt_ref[0]
        kd = kd_ref[0]
        rt = heads(r_ref[0] * jnp.exp(cs))
        at = heads(-kk_ref[0] * jnp.exp(cs - lw))
        bti = heads(bt * p_inv)
        kti = heads(kd * p_inv)
        bp = heads(bt * p_rest)
        kp = heads(kd * p_rest)
        x1 = jnp.stack([jnp.concatenate([a, b], axis=0) for a, b in zip(at, rt)])
        x2 = jnp.stack([jnp.concatenate([a, b], axis=0) for a, b in zip(bti, kti)])
        m_l.append(jnp.where(mask_m[None], _bmm(x1, x2, _NT), 0.0))
        at_l += at
        rt_l += rt
        v_l += heads(v_ref[0])
        bk_l += [jnp.concatenate([a, b], axis=0) for a, b in zip(bp, kp)]
        pt_l += heads(jnp.exp(tot))
    m = jnp.concatenate(m_l, axis=0)
    at_s = jnp.stack(at_l)
    rt_s = jnp.stack(rt_l)
    v_s = jnp.stack(v_l)
    bk_s = jnp.stack(bk_l)
    pt_s = jnp.stack(pt_l)

    a_full = m[:, 0:c, 0:c]
    d_pow = jnp.where(same_blk[None], a_full, 0.0)
    l_off = a_full - d_pow
    t_diag = eye[None] + d_pow
    for _ in range(blk_bits - 1):
        d_pow = _bmm3(d_pow, d_pow)
        t_diag = t_diag + _bmm3(d_pow, t_diag)
    n_mat = _bmm3(t_diag, l_off)
    x_mat = t_diag + _bmm3(n_mat, t_diag)
    tinv = x_mat + _bmm3(_bmm3(n_mat, n_mat), x_mat)

    st = st_ref[...]
    u = _bmm(tinv, _bmm(m[:, 0:c, c:2 * c], v_s) + _bmm(at_s, st, _NT))
    uv = jnp.concatenate([u, v_s], axis=1)
    y = _bmm(m[:, c:2 * c, :], uv) + _bmm(rt_s, st, _NT)
    st_ref[...] = pt_s * st + _bmm(uv, bk_s, _TN)
    for h in range(RWKV_HEADS):
        yf_ref[0, :, h * n:(h + 1) * n] = y[h]
        yb_ref[0, :, h * n:(h + 1) * n] = y[RWKV_HEADS + h]


def _rwkv_scan(r, v, kk, lw_f, lw_b, bt_f, bt_b, kd_f, kd_b, n_lat):
    b, ta, w = r.shape
    nc = ta // CHUNK
    nlat = n_lat // CHUNK
    nctx = nc - nlat

    def fwd_idx(s):
        return jnp.where(s < nctx, nlat + s, s - nctx)

    def rev_idx(s):
        return nc - 1 - s

    fwd = pl.BlockSpec((1, CHUNK, w), lambda bb, s: (bb, fwd_idx(s), 0))
    bwd = pl.BlockSpec((1, CHUNK, w), lambda bb, s: (bb, rev_idx(s), 0))
    return pl.pallas_call(
        _scan_kernel,
        grid=(b, nc),
        in_specs=[fwd] * 6 + [bwd] * 6,
        out_specs=[fwd, bwd],
        out_shape=[jax.ShapeDtypeStruct((b, ta, w), F32)] * 2,
        scratch_shapes=[pltpu.VMEM((2 * RWKV_HEADS, RWKV_HEAD, RWKV_HEAD), F32)],
        compiler_params=_cparams(("parallel", "arbitrary")),
        name="rwkv_scan",
    )(r, v, kk, lw_f, bt_f, kd_f, r, v, kk, lw_b, bt_b, kd_b)


def _postmix_kernel(x_ref, att_ref, yf_ref, yb_ref, r_ref, v_ref, kdf_ref, kdb_ref, g_ref,
                    bgt_ref, cgt_ref, wo_ref, lnw_ref, lnb_ref, rk_ref, seg_ref, o_ref, *, tm, n_lat):
    i = pl.program_id(1)
    is_ctx = _is_ctx_rows(i, tm, n_lat)
    seg = seg_ref[...]
    y = yf_ref[0] + yb_ref[0]
    inv_n = 1.0 / RWKV_HEAD
    mean = _dot_exact_rhs(y, seg) * inv_n
    dy = y - mean
    var = _dot_exact_rhs(dy * dy, seg) * inv_n
    o = dy * lax.rsqrt(var + LNX_EPS) * lnw_ref[...] + lnb_ref[...]
    r = r_ref[0]
    v = v_ref[0]
    rrk = r * rk_ref[...]
    o = o + _dot_exact_rhs(rrk * kdf_ref[0], seg) * v
    o = o + _dot_exact_rhs(rrk * kdb_ref[0], seg) * v
    rw = o * g_ref[0]
    mixed = (lax.dot_general(att_ref[0], wo_ref[0:ATTN_WIDTH, :], (((0,), (0,)), ((), ())),
                             preferred_element_type=F32)
             + jnp.dot(rw.astype(BF16), wo_ref[ATTN_WIDTH:, :], preferred_element_type=F32))
    gt = jnp.where(is_ctx, cgt_ref[...], bgt_ref[0])
    o_ref[0] = x_ref[0] + gt * mixed


def _post_mix(xa, att, y_f, y_b, r, v, kd_f, kd_b, g, modb, modc, w_out, lnw, lnb, r_k, seg, n_lat, tm):
    b, ta, d = xa.shape
    w = RWKV_WIDTH
    tok = lambda ww: pl.BlockSpec((1, tm, ww), lambda bb, i: (bb, i, 0))
    full = lambda a: pl.BlockSpec(a.shape, lambda bb, i, _n=a.ndim: (0,) * _n)
    consts = [w_out, lnw.reshape(1, w), lnb.reshape(1, w), r_k.reshape(1, w), seg]
    return pl.pallas_call(
        functools.partial(_postmix_kernel, tm=tm, n_lat=n_lat),
        grid=(b, ta // tm),
        in_specs=[tok(d), pl.BlockSpec((1, ATTN_WIDTH, tm), lambda bb, i: (bb, 0, i))] + [tok(w)] * 7
        + [pl.BlockSpec((1, 1, d), lambda bb, i: (bb, 0, 2)), pl.BlockSpec((1, d), lambda bb, i: (0, 2))]
        + [full(a) for a in consts],
        out_specs=tok(d),
        out_shape=jax.ShapeDtypeStruct((b, ta, d), F32),
        compiler_params=_cparams(("parallel", "parallel")),
        name="rwkv_finish_out_proj",
    )(xa, att, y_f, y_b, r, v, kd_f, kd_b, g, modb, modc, *consts)


def _ffn_kernel(x_ref, g_ref, bsh_ref, bsc_ref, csh_ref, csc_ref, bgt_ref, cgt_ref,
                w1_ref, w3_ref, w2_ref, o_ref, *, tm, n_lat):
    i = pl.program_id(1)
    is_ctx = _is_ctx_rows(i, tm, n_lat)
    x = x_ref[0]
    h = _modnorm(x, g_ref[...], bsc_ref[0], bsh_ref[0], csc_ref[...], csh_ref[...], is_ctx).astype(BF16)
    a = jnp.dot(h, w1_ref[...], preferred_element_type=F32)
    bb = jnp.dot(h, w3_ref[...], preferred_element_type=F32)
    act = (a * jax.nn.sigmoid(a) * bb).astype(BF16)
    out = jnp.dot(act, w2_ref[...], preferred_element_type=F32)
    gt = jnp.where(is_ctx, cgt_ref[...], bgt_ref[0])
    o_ref[0] = x + gt * out


def _dense_ffn(xa, g2, modb, modc, w1, w3, w2, n_lat, tm):
    b, ta, d = xa.shape
    tok = pl.BlockSpec((1, tm, d), lambda bb, i: (bb, i, 0))
    full = lambda a: pl.BlockSpec(a.shape, lambda bb, i, _n=a.ndim: (0,) * _n,
                                  pipeline_mode=pl.Buffered(1))
    return pl.pallas_call(
        functools.partial(_ffn_kernel, tm=tm, n_lat=n_lat),
        grid=(b, ta // tm),
        in_specs=[tok, pl.BlockSpec((1, d), lambda bb, i: (0, 0))] + _mod_specs(d, 3, 4)
        + [pl.BlockSpec((1, 1, d), lambda bb, i: (bb, 0, 5)), pl.BlockSpec((1, d), lambda bb, i: (0, 5))]
        + [full(w1), full(w3), full(w2)],
        out_specs=tok,
        out_shape=jax.ShapeDtypeStruct((b, ta, d), F32),
        compiler_params=_cparams(("parallel", "parallel")),
        name="norm2_dense_swiglu",
    )(xa, g2, modb, modb, modc, modc, modb, modc, w1, w3, w2)


def _moe_kernel(x_ref, g_ref, bsh_ref, bsc_ref, csh_ref, csc_ref, bgt_ref, cgt_ref, rw_ref,
                w1_ref, w3_ref, w2_ref, o_ref, h_ref, comb_ref, acc_ref, *, tm, n_lat):
    i = pl.program_id(1)
    e = pl.program_id(2)
    f = pl.program_id(3)
    last = jnp.logical_and(e == pl.num_programs(2) - 1, f == pl.num_programs(3) - 1)

    @pl.when(jnp.logical_and(e == 0, f == 0))
    def _route():
        is_ctx = _is_ctx_rows(i, tm, n_lat)
        h = _modnorm(x_ref[0], g_ref[...], bsc_ref[0], bsh_ref[0], csc_ref[...], csh_ref[...], is_ctx)
        h_ref[...] = h.astype(BF16)
        hh, hl = _split(h)
        wh, wl = _split(rw_ref[...])
        logits = (jnp.dot(hh, wh, preferred_element_type=F32) + jnp.dot(hl, wh, preferred_element_type=F32)
                  + jnp.dot(hh, wl, preferred_element_type=F32))
        lane = lax.broadcasted_iota(jnp.int32, logits.shape, 1).astype(F32)
        neg = -jnp.inf
        logits = jnp.where(lane < N_EXPERTS, logits, neg)
        v1 = jnp.max(logits, axis=1, keepdims=True)
        i1 = jnp.min(jnp.where(logits == v1, lane, float(LANES)), axis=1, keepdims=True)
        rest = jnp.where(lane == i1, neg, logits)
        v2 = jnp.max(rest, axis=1, keepdims=True)
        i2 = jnp.min(jnp.where(rest == v2, lane, float(LANES)), axis=1, keepdims=True)
        e2 = jnp.exp(v2 - v1)
        den = 1.0 + e2
        comb_ref[...] = jnp.where(lane == i1, 1.0 / den, 0.0) + jnp.where(lane == i2, e2 / den, 0.0)
        acc_ref[...] = jnp.zeros(acc_ref.shape, F32)

    h = h_ref[...]
    lane = lax.broadcasted_iota(jnp.int32, comb_ref.shape, 1)
    gate = jnp.sum(jnp.where(lane == e, comb_ref[...], 0.0), axis=1, keepdims=True)
    a = jnp.dot(h, w1_ref[0], preferred_element_type=F32)
    bb = jnp.dot(h, w3_ref[0], preferred_element_type=F32)
    act = (a * jax.nn.sigmoid(a) * bb * gate).astype(BF16)
    acc_ref[...] += jnp.dot(act, w2_ref[0], preferred_element_type=F32)

    @pl.when(last)
    def _finish():
        is_ctx = _is_ctx_rows(i, tm, n_lat)
        gt = jnp.where(is_ctx, cgt_ref[...], bgt_ref[0])
        o_ref[0] = x_ref[0] + gt * acc_ref[...]


def _moe_ffn(xa, g2, modb, modc, router_p, w1, w3, w2, n_lat, tm, nf):
    b, ta, d = xa.shape
    ne, _, ff = w1.shape
    tf = ff // nf
    tok = pl.BlockSpec((1, tm, d), lambda bb, i, e, f: (bb, i, 0))
    return pl.pallas_call(
        functools.partial(_moe_kernel, tm=tm, n_lat=n_lat),
        grid=(b, ta // tm, ne, nf),
        in_specs=[tok, pl.BlockSpec((1, d), lambda bb, i, e, f: (0, 0))] + _mod_specs(d, 3, 4)
        + [pl.BlockSpec((1, 1, d), lambda bb, i, e, f: (bb, 0, 5)),
           pl.BlockSpec((1, d), lambda bb, i, e, f: (0, 5)),
           pl.BlockSpec((d, LANES), lambda bb, i, e, f: (0, 0)),
           pl.BlockSpec((1, d, tf), lambda bb, i, e, f: (e, 0, f)),
           pl.BlockSpec((1, d, tf), lambda bb, i, e, f: (e, 0, f)),
           pl.BlockSpec((1, tf, d), lambda bb, i, e, f: (e, f, 0))],
        out_specs=tok,
        out_shape=jax.ShapeDtypeStruct((b, ta, d), F32),
        scratch_shapes=[pltpu.VMEM((tm, d), BF16), pltpu.VMEM((tm, LANES), F32), pltpu.VMEM((tm, d), F32)],
        compiler_params=_cparams(("parallel", "parallel", "arbitrary", "arbitrary")),
        name="norm2_moe_swiglu",
    )(xa, g2, modb, modb, modc, modc, modb, modc, router_p, w1, w3, w2)


def _final_kernel(x_ref, g_ref, o_ref):
    x = x_ref[0]
    ms = jnp.mean(x * x, axis=-1, keepdims=True)
    o_ref[0] = x * lax.rsqrt(ms + NORM_EPS) * g_ref[...]


def _final_norm(xa, g, seq, tm):
    b, _, d = xa.shape
    return pl.pallas_call(
        _final_kernel,
        grid=(b, seq // tm),
        in_specs=[pl.BlockSpec((1, tm, d), lambda bb, i: (bb, i, 0)),
                  pl.BlockSpec((1, d), lambda bb, i: (0, 0))],
        out_specs=pl.BlockSpec((1, tm, d), lambda bb, i: (bb, i, 0)),
        out_shape=jax.ShapeDtypeStruct((b, seq, d), F32),
        compiler_params=_cparams(("parallel", "parallel")),
        name="final_norm",
    )(xa, g)


def _rope_tables(seq, ctx_len):
    axis_dim = ATTN_QK_DIM // 2
    inv_freq = ROPE_THETA ** (-jnp.arange(0, axis_dim, 2, dtype=F32) / axis_dim)
    t = jnp.arange(seq)
    pos = jnp.stack([(t // GRID_W).astype(F32), (t % GRID_W).astype(F32)], axis=1)
    lane = np.arange(LANES)
    axis = (lane % ATTN_QK_DIM) // axis_dim
    freq = lane % (axis_dim // 2)
    sign = np.where((lane % axis_dim) < axis_dim // 2, -1.0, 1.0).astype(np.float32)
    ang = pos[:, axis] * inv_freq[freq][None, :]
    cos_t = jnp.concatenate([jnp.cos(ang), jnp.ones((ctx_len, LANES), F32)], axis=0)
    sin_t = jnp.concatenate([jnp.sin(ang) * sign[None, :], jnp.zeros((ctx_len, LANES), F32)], axis=0)
    return cos_t, sin_t


def _pick_tile(n, candidates):
    for c in candidates:
        if n % c == 0:
            return c
    raise ValueError(f"no tile in {candidates} divides {n}")


def kernel(x, c, ctx, c_ctx, ada_w, ada_b, norm1_g, norm2_g, w_in, w_out, diff_lambda, subln_g, rwkv_mu, rwkv_w0, rwkv_w2, rwkv_a0, rwkv_a2, rwkv_v0, rwkv_v1, rwkv_v2, rwkv_g2, rwkv_k_k, rwkv_k_a, rwkv_r_k, lnx_w, lnx_b, ffn_w1, ffn_w3, ffn_w2, router_w, exp_w1, exp_w3, exp_w2, final_g):
    b, seq, d = x.shape
    n_lat = seq
    ctx_len = ctx.shape[1]
    depth = ada_w.shape[0]
    ta = n_lat + ctx_len
    assert b + 1 <= 8 and seq % GRID_W == 0 and ctx_len % CHUNK == 0 and seq % CHUNK == 0

    tm = _pick_tile(math.gcd(ctx_len, seq), (256, 128))
    tq = _pick_tile(seq, (512, 256, 128))
    tk = _pick_tile(ta, (3328, 1280, 640, 256, 128))
    tm_moe = _pick_tile(ta, (640, 256, 128))

    xa = jnp.concatenate([x, ctx], axis=1)
    cvec = jnp.zeros((8, d), F32).at[:b].set(c).at[b].set(c_ctx)
    mod = _modulation(cvec, ada_w, ada_b)
    cos_t, sin_t = _rope_tables(seq, ctx_len)
    seg = jnp.asarray(np.kron(np.eye(RWKV_HEADS), np.ones((RWKV_HEAD, RWKV_HEAD))), BF16)
    n_in = w_in.shape[2]
    n_in_p = 3 * ATTN_WIDTH + 3 * RWKV_WIDTH + LORA_PAD
    router_p = jnp.pad(router_w, ((0, 0), (0, 0), (0, LANES - router_w.shape[2])))

    v_first = None
    for l in range(depth):
        lam_init = 0.8 - 0.6 * math.exp(-0.3 * l)
        modb = mod[l, :b].reshape(b, 1, 6 * d)
        modc = mod[l, b:b + 1]
        w_in_p = jnp.pad(w_in[l], ((0, 0), (0, n_in_p - n_in))).astype(BF16)
        q, k, v, pr, pk, pv, lora = _in_proj(xa, norm1_g[l].reshape(1, d), modb, modc, w_in_p,
                                             cos_t, sin_t, n_lat, tm)
        sg = subln_g[l].reshape(ATTN_V_DIM, 1)
        att_lat = _attention(q, k, v, diff_lambda[l], sg, lam_init, 0, seq, 0, ta, tq, tk)
        att_ctx = _attention(q, k, v, diff_lambda[l], sg, lam_init, seq, ctx_len, seq, ctx_len,
                             ctx_len, ctx_len)
        att = jnp.concatenate([att_lat, att_ctx], axis=2)
        vres = None if l == 0 else (v_first, rwkv_v0[l - 1], rwkv_v1[l - 1], rwkv_v2[l - 1])
        r, vv, kk, g, lw_f, lw_b, bt_f, bt_b, kd_f, kd_b = _rwkv_prepare(
            pr, pk, pv, lora, rwkv_mu[l], rwkv_w0[l], rwkv_w2[l], rwkv_a0[l], rwkv_a2[l], rwkv_g2[l],
            rwkv_k_k[l], rwkv_k_a[l], seg, vres, n_lat, tm)
        if l == 0:
            v_first = vv
        y_f, y_b = _rwkv_scan(r, vv, kk, lw_f, lw_b, bt_f, bt_b, kd_f, kd_b, n_lat)
        xa = _post_mix(xa, att, y_f, y_b, r, vv, kd_f, kd_b, g, modb, modc, w_out[l].astype(BF16),
                       lnx_w[l], lnx_b[l], rwkv_r_k[l], seg, n_lat, tm)
        i = l // 2
        if l % 2 == 0:
            xa = _dense_ffn(xa, norm2_g[l].reshape(1, d), modb, modc, ffn_w1[i].astype(BF16),
                            ffn_w3[i].astype(BF16), ffn_w2[i].astype(BF16), n_lat, tm)
        else:
            xa = _moe_ffn(xa, norm2_g[l].reshape(1, d), modb, modc, router_p[i],
                          exp_w1[i].astype(BF16), exp_w3[i].astype(BF16), exp_w2[i].astype(BF16),
                          n_lat, tm_moe, 2)
    return _final_norm(xa, final_g.reshape(1, d), seq, tm)
```

```python
import functools
import math

import numpy as np
import jax
import jax.numpy as jnp
from jax import lax
from jax.experimental import pallas as pl
from jax.experimental.pallas import tpu as pltpu

F32 = jnp.float32
BF16 = jnp.bfloat16

ATTN_HEADS = 4
ATTN_QK_DIM = 64
ATTN_V_DIM = 128
ATTN_WIDTH = ATTN_HEADS * ATTN_V_DIM
RWKV_HEADS = 8
RWKV_HEAD = 64
RWKV_WIDTH = RWKV_HEADS * RWKV_HEAD
DECAY_LORA = 32
AAA_LORA = 32
GATE_LORA = 96
LORA_PAD = 256
N_EXPERTS = 8
GRID_W = 64
ROPE_THETA = 10000.0
NORM_EPS = 1e-6
SUBLN_EPS = 1e-5
LNX_EPS = 64e-5
LANES = 128
CHUNK = 64
INV_BLOCK = 16
SCAN_CHUNKS = 2
QK_LOOKAHEAD = 2
V_ONES = 16
V_EXT = ATTN_V_DIM + V_ONES
VMEM_LIMIT = 56 * 1024 * 1024
Q_SCALE = ATTN_QK_DIM ** -0.5 * math.log2(math.e)


def _cparams(sem, vmem=VMEM_LIMIT, flags=None):
    return pltpu.CompilerParams(dimension_semantics=sem, vmem_limit_bytes=vmem, flags=flags)


def _bdot(a, b):
    return jnp.dot(a.astype(BF16), b.astype(BF16), preferred_element_type=F32)


def _bdot_nt(a, b):
    return lax.dot_general(a.astype(BF16), b.astype(BF16), (((1,), (1,)), ((), ())),
                           preferred_element_type=F32)


def _bdot_tn(a, b):
    return lax.dot_general(a.astype(BF16), b.astype(BF16), (((0,), (0,)), ((), ())),
                           preferred_element_type=F32)


def _split(a):
    hi = a.astype(BF16)
    lo = (a - hi.astype(F32)).astype(BF16)
    return hi, lo


def _dot_exact_rhs(a, b_bf16):
    hi, lo = _split(a)
    return (jnp.dot(hi, b_bf16, preferred_element_type=F32)
            + jnp.dot(lo, b_bf16, preferred_element_type=F32))


def _dot_exact_lhs(a_bf16, b):
    hi, lo = _split(b)
    return (jnp.dot(a_bf16, hi, preferred_element_type=F32)
            + jnp.dot(a_bf16, lo, preferred_element_type=F32))


def _modnorm(x, g, sc_b, sh_b, sc_c, sh_c, is_ctx):
    ms = jnp.mean(x * x, axis=-1, keepdims=True)
    y = x * lax.rsqrt(ms + NORM_EPS) * g
    sc = jnp.where(is_ctx, sc_c, sc_b)
    sh = jnp.where(is_ctx, sh_c, sh_b)
    return y * (1.0 + sc) + sh


def _is_ctx_rows(tile_idx, tm, n_lat):
    row = tile_idx * tm + lax.broadcasted_iota(jnp.int32, (tm, 1), 0)
    return row >= n_lat


def _mod_specs(d, k_sh, k_sc):
    return [
        pl.BlockSpec((1, 1, d), lambda b, i, *_: (b, 0, k_sh)),
        pl.BlockSpec((1, 1, d), lambda b, i, *_: (b, 0, k_sc)),
        pl.BlockSpec((1, d), lambda b, i, *_: (0, k_sh)),
        pl.BlockSpec((1, d), lambda b, i, *_: (0, k_sc)),
    ]


def _mod_kernel(c_ref, w_ref, b_ref, o_ref):
    c = c_ref[...]
    s = c * jax.nn.sigmoid(c)
    o_ref[0] = _bdot(s, w_ref[0]) + b_ref[0]


def _modulation(cvec, ada_w, ada_b):
    depth, d, n = ada_w.shape
    tn = 2048
    return pl.pallas_call(
        _mod_kernel,
        grid=(depth, n // tn),
        in_specs=[pl.BlockSpec((8, d), lambda l, j: (0, 0)),
                  pl.BlockSpec((1, d, tn), lambda l, j: (l, 0, j)),
                  pl.BlockSpec((1, 1, tn), lambda l, j: (l, 0, j))],
        out_specs=pl.BlockSpec((1, 8, tn), lambda l, j: (l, 0, j)),
        out_shape=jax.ShapeDtypeStruct((depth, 8, n), F32),
        compiler_params=_cparams(("parallel", "parallel")),
        name="adaln_mod",
    )(cvec, ada_w, ada_b.reshape(depth, 1, n))


def _inproj_kernel(x_ref, g_ref, bsh_ref, bsc_ref, csh_ref, csc_ref, w_ref, cos_ref, sin_ref,
                   q_ref, k_ref, v_ref, rr_ref, rk_ref, rv_ref, lo_ref, *, tm, n_lat):
    i = pl.program_id(1)
    is_ctx = _is_ctx_rows(i, tm, n_lat)
    h = _modnorm(x_ref[0], g_ref[...], bsc_ref[0], bsh_ref[0], csc_ref[...], csh_ref[...], is_ctx)
    p = jnp.dot(h.astype(BF16), w_ref[...], preferred_element_type=F32)
    cos_t = cos_ref[...]
    sin_t = sin_ref[...]
    lane = lax.broadcasted_iota(jnp.int32, (tm, LANES), 1)
    first = (lane & 31) < 16

    def rope(z):
        partner = jnp.where(first, pltpu.roll(z, LANES - 16, 1), pltpu.roll(z, 16, 1))
        return z * cos_t + partner * sin_t

    for hh in range(ATTN_HEADS):
        lo = LANES * hh
        q_ref[0, lo:lo + LANES, :] = jnp.transpose(rope(p[:, lo:lo + LANES]) * Q_SCALE).astype(BF16)
        k_ref[0, :, lo:lo + LANES] = rope(p[:, ATTN_WIDTH + lo:ATTN_WIDTH + lo + LANES]).astype(BF16)
        vo = 2 * ATTN_WIDTH + lo
        ve = V_EXT * hh
        v_ref[0, ve:ve + ATTN_V_DIM, :] = jnp.transpose(p[:, vo:vo + LANES]).astype(BF16)
        v_ref[0, ve + ATTN_V_DIM:ve + V_EXT, :] = jnp.ones((V_ONES, tm), BF16)
    o = 3 * ATTN_WIDTH
    rr_ref[0] = p[:, o:o + RWKV_WIDTH]
    rk_ref[0] = p[:, o + RWKV_WIDTH:o + 2 * RWKV_WIDTH]
    rv_ref[0] = p[:, o + 2 * RWKV_WIDTH:o + 3 * RWKV_WIDTH]
    o += 3 * RWKV_WIDTH
    lo_ref[0] = p[:, o:o + LORA_PAD]


def _in_proj(xa, g1, modb, modc, w_in_p, cos_t, sin_t, n_lat, tm):
    b, ta, d = xa.shape
    n = w_in_p.shape[1]
    tok = lambda w: pl.BlockSpec((1, tm, w), lambda bb, i: (bb, i, 0))
    feat = pl.BlockSpec((1, ATTN_WIDTH, tm), lambda bb, i: (bb, 0, i))
    feat_shape = jax.ShapeDtypeStruct((b, ATTN_WIDTH, ta), BF16)
    vext = pl.BlockSpec((1, ATTN_HEADS * V_EXT, tm), lambda bb, i: (bb, 0, i))
    vext_shape = jax.ShapeDtypeStruct((b, ATTN_HEADS * V_EXT, ta), BF16)
    outs = [feat_shape, jax.ShapeDtypeStruct((b, ta, ATTN_WIDTH), BF16), vext_shape] \
        + [jax.ShapeDtypeStruct((b, ta, RWKV_WIDTH), F32)] * 3 \
        + [jax.ShapeDtypeStruct((b, ta, LORA_PAD), F32)]
    return pl.pallas_call(
        functools.partial(_inproj_kernel, tm=tm, n_lat=n_lat),
        grid=(b, ta // tm),
        in_specs=[tok(d), pl.BlockSpec((1, d), lambda bb, i: (0, 0))] + _mod_specs(d, 0, 1)
        + [pl.BlockSpec((d, n), lambda bb, i: (0, 0)),
           pl.BlockSpec((tm, LANES), lambda bb, i: (i, 0)),
           pl.BlockSpec((tm, LANES), lambda bb, i: (i, 0))],
        out_specs=[feat, tok(ATTN_WIDTH), vext] + [tok(RWKV_WIDTH)] * 3 + [tok(LORA_PAD)],
        out_shape=outs,
        compiler_params=_cparams(("parallel", "parallel")),
        name="norm1_in_proj",
    )(xa, g1, modb, modb, modc, modc, w_in_p, cos_t, sin_t)


def _attn_kernel(lam_ref, sg_ref, q_ref, k_ref, v_ref, o_ref, qs_ref, m_ref, acc_ref,
                 *, tq, tk, ts, lam_init):
    ki = pl.program_id(3)
    nk = pl.num_programs(3)

    @pl.when(ki == 0)
    def _init():
        q = q_ref[0]
        feat = lax.broadcasted_iota(jnp.int32, (LANES, tq), 0)
        zero = jnp.zeros_like(q)
        qs_ref[:, 0:tq] = jnp.where(feat < ATTN_QK_DIM, q, zero)
        qs_ref[:, tq:2 * tq] = jnp.where(feat >= ATTN_QK_DIM, q, zero)
        m_ref[...] = jnp.full(m_ref.shape, -jnp.inf, F32)
        acc_ref[...] = jnp.zeros(acc_ref.shape, F32)

    n_sub = tk // ts
    qs = qs_ref[...]
    qk = lambda j: jnp.dot(k_ref[0, j * ts:(j + 1) * ts, :], qs, preferred_element_type=F32)
    m_run = m_ref[...]
    acc = acc_ref[...]
    pending = [qk(j) for j in range(min(QK_LOOKAHEAD, n_sub))]
    for j in range(n_sub):
        if j + QK_LOOKAHEAD < n_sub:
            pending.append(qk(j + QK_LOOKAHEAD))
        s_cur = pending.pop(0)
        m_new = jnp.maximum(m_run, jnp.max(s_cur, axis=0, keepdims=True))
        alpha = jnp.exp2(m_run - m_new)
        p = jnp.exp2(s_cur - m_new).astype(BF16)
        acc = alpha * acc + jnp.dot(v_ref[0, :, j * ts:(j + 1) * ts], p, preferred_element_type=F32)
        m_run = m_new
    m_ref[...] = m_run
    acc_ref[...] = acc

    @pl.when(ki == nk - 1)
    def _finish():
        lp = lam_ref[...]
        lam = (jnp.exp(jnp.sum(lp[0:1] * lp[1:2], axis=1, keepdims=True))
               - jnp.exp(jnp.sum(lp[2:3] * lp[3:4], axis=1, keepdims=True)) + lam_init)
        l = acc_ref[ATTN_V_DIM:ATTN_V_DIM + 1, :]
        o1 = acc_ref[0:ATTN_V_DIM, 0:tq] / l[:, 0:tq]
        o2 = acc_ref[0:ATTN_V_DIM, tq:2 * tq] / l[:, tq:2 * tq]
        o = o1 - lam * o2
        ms = jnp.mean(o * o, axis=0, keepdims=True)
        y = o * lax.rsqrt(ms + SUBLN_EPS) * sg_ref[...]
        o_ref[0] = (y * (1.0 - lam_init)).astype(BF16)


def _attention(q_t, k, v_t, lam_p, subln_g, lam_init, q0, nq, k0, nkeys, tq, tk):
    b = k.shape[0]
    qo, ko = q0 // tq, k0 // tk
    return pl.pallas_call(
        functools.partial(_attn_kernel, tq=tq, tk=tk, ts=_pick_tile(tk, (256, LANES)), lam_init=lam_init),
        grid=(b, ATTN_HEADS, nq // tq, nkeys // tk),
        in_specs=[pl.BlockSpec((4, ATTN_QK_DIM), lambda bb, h, i, j: (0, 0)),
                  pl.BlockSpec((ATTN_V_DIM, 1), lambda bb, h, i, j: (0, 0)),
                  pl.BlockSpec((1, LANES, tq), lambda bb, h, i, j: (bb, h, i + qo)),
                  pl.BlockSpec((1, tk, LANES), lambda bb, h, i, j: (bb, j + ko, h)),
                  pl.BlockSpec((1, V_EXT, tk), lambda bb, h, i, j: (bb, h, j + ko))],
        out_specs=pl.BlockSpec((1, ATTN_V_DIM, tq), lambda bb, h, i, j: (bb, h, i)),
        out_shape=jax.ShapeDtypeStruct((b, ATTN_WIDTH, nq), BF16),
        scratch_shapes=[pltpu.VMEM((LANES, 2 * tq), BF16),
                        pltpu.VMEM((1, 2 * tq), F32),
                        pltpu.VMEM((V_EXT, 2 * tq), F32)],
        compiler_params=_cparams(("parallel", "parallel", "parallel", "arbitrary")),
        name="diff_attention",
    )(lam_p, subln_g, q_t, k, v_t)


def _prep_kernel(*refs, tt, n_lat, ta, has_vres):
    (pr_ref, pk_ref, pv_ref, hpr_ref, hpk_ref, hpv_ref, hnr_ref, hnk_ref, hnv_ref, lo_ref,
     mu_ref, w0_ref, w2_ref, a0_ref, a2_ref, g2_ref, kk_ref, ka_ref, seg_ref) = refs[:19]
    n_in = 19
    if has_vres:
        vf_ref, v0_ref, v1_ref, v2_ref = refs[19:23]
        n_in = 23
    (r_out, v_out, kk_out, g_out, lwf_out, lwb_out, bf_out, bb_out, kdf_out, kdb_out) = refs[n_in:]

    i = pl.program_id(1)
    row = lax.broadcasted_iota(jnp.int32, (tt, 1), 0)
    gidx = i * tt + row
    no_prev = jnp.logical_or(gidx == 0, gidx == n_lat)
    no_next = jnp.logical_or(gidx == n_lat - 1, gidx == ta - 1)

    def shift_mix(p_ref, hp_ref, hn_ref, j):
        p = p_ref[0]
        pp = jnp.where(row == 0, hp_ref[0, 7:8, :], pltpu.roll(p, 1, 0))
        pp = jnp.where(no_prev, 0.0, pp)
        pn = jnp.where(row == tt - 1, hn_ref[0, 0:1, :], pltpu.roll(p, tt - 1, 0))
        pn = jnp.where(no_next, 0.0, pn)
        return p + mu_ref[2 * j:2 * j + 1, :] * (pp - p) + mu_ref[2 * j + 1:2 * j + 2, :] * (pn - p)

    r = shift_mix(pr_ref, hpr_ref, hnr_ref, 0)
    k = shift_mix(pk_ref, hpk_ref, hnk_ref, 1)
    v = shift_mix(pv_ref, hpv_ref, hnv_ref, 2)
    if has_vres:
        mix = jax.nn.sigmoid(v0_ref[...] + _bdot(_bdot(v, v1_ref[...]), v2_ref[...]))
        v = v + (vf_ref[0] - v) * mix
    lo = lo_ref[0]
    gd = lo[:, 2 * DECAY_LORA + 2 * AAA_LORA:2 * DECAY_LORA + 2 * AAA_LORA + GATE_LORA]
    g_out[0] = _bdot(jax.nn.sigmoid(gd), g2_ref[...])
    kk = k * kk_ref[...]
    ss = _dot_exact_rhs(kk * kk, seg_ref[...])
    kk = kk * lax.rsqrt(jnp.maximum(ss, 1e-24))
    r_out[0] = r
    v_out[0] = v
    kk_out[0] = kk
    lw_outs = (lwf_out, lwb_out)
    b_outs = (bf_out, bb_out)
    kd_outs = (kdf_out, kdb_out)
    for d in range(2):
        wd = lo[:, d * DECAY_LORA:(d + 1) * DECAY_LORA]
        wl = w0_ref[d:d + 1, :] + _bdot(jnp.tanh(wd), w2_ref[d])
        lw_outs[d][0] = -jax.nn.sigmoid(wl) * math.exp(-0.5)
        ad = lo[:, 2 * DECAY_LORA + d * AAA_LORA:2 * DECAY_LORA + (d + 1) * AAA_LORA]
        a = jax.nn.sigmoid(a0_ref[d:d + 1, :] + _bdot(ad, a2_ref[d]))
        b_outs[d][0] = a * kk
        kd_outs[d][0] = k * (1.0 + (a - 1.0) * ka_ref[...])


def _rwkv_prepare(pr, pk, pv, lora, mu, w0, w2, a0, a2, g2, k_k, k_a, seg, vres, n_lat, tt):
    b, ta, w = pr.shape
    nb8 = ta // 8
    tok = pl.BlockSpec((1, tt, w), lambda bb, i: (bb, i, 0))
    prev = pl.BlockSpec((1, 8, w), lambda bb, i: (bb, jnp.maximum(i * (tt // 8) - 1, 0), 0))
    nxt = pl.BlockSpec((1, 8, w), lambda bb, i: (bb, jnp.minimum((i + 1) * (tt // 8), nb8 - 1), 0))
    full = lambda a: pl.BlockSpec(a.shape, lambda bb, i, _n=a.ndim: (0,) * _n)
    consts = [mu.reshape(6, w), w0, w2, a0, a2, g2, k_k.reshape(1, w), k_a.reshape(1, w), seg]
    ins = [pr, pk, pv, pr, pk, pv, pr, pk, pv, lora] + consts
    specs = [tok] * 3 + [prev] * 3 + [nxt] * 3 \
        + [pl.BlockSpec((1, tt, LORA_PAD), lambda bb, i: (bb, i, 0))] + [full(a) for a in consts]
    if vres is not None:
        v_first, v0, v1, v2 = vres
        extra = [v0.reshape(1, w), v1, v2]
        ins += [v_first] + extra
        specs += [tok] + [full(a) for a in extra]
    return pl.pallas_call(
        functools.partial(_prep_kernel, tt=tt, n_lat=n_lat, ta=ta, has_vres=vres is not None),
        grid=(b, ta // tt),
        in_specs=specs,
        out_specs=[tok] * 10,
        out_shape=[jax.ShapeDtypeStruct((b, ta, w), F32)] * 10,
        compiler_params=_cparams(("parallel", "parallel")),
        name="rwkv_prepare",
    )(*ins)


_NN = (((2,), (1,)), ((0,), (0,)))
_NT = (((2,), (2,)), ((0,), (0,)))
_TN = (((1,), (1,)), ((0,), (0,)))


def _bmm(a, b, dims=_NN):
    return lax.dot_general(a.astype(BF16), b.astype(BF16), dims, preferred_element_type=F32)


def _scan_kernel(rf_ref, vf_ref, kkf_ref, lwf_ref, btf_ref, kdf_ref,
                 rb_ref, vb_ref, kkb_ref, lwb_ref, btb_ref, kdb_ref,
                 yf_ref, yb_ref, st_ref, *, nsub):
    s = pl.program_id(1)
    c = CHUNK
    n = RWKV_HEAD
    g = 2 * RWKV_HEADS

    @pl.when(s == 0)
    def _init():
        st_ref[...] = jnp.zeros(st_ref.shape, F32)

    t_i = lax.broadcasted_iota(jnp.int32, (c, c), 0)
    j_i = lax.broadcasted_iota(jnp.int32, (c, c), 1)
    t2 = lax.broadcasted_iota(jnp.int32, (2 * c, 2 * c), 0)
    j2 = lax.broadcasted_iota(jnp.int32, (2 * c, 2 * c), 1)
    tt2 = t2 & (c - 1)
    jj2 = j2 & (c - 1)
    eye = jnp.where(t_i == j_i, 1.0, 0.0)
    blk_bits = int(math.log2(INV_BLOCK))
    same_blk = (t_i >> blk_bits) == (j_i >> blk_bits)
    heads = lambda z: [z[:, h * n:(h + 1) * n] for h in range(RWKV_HEADS)]
    dirs = ((rf_ref, vf_ref, kkf_ref, lwf_ref, btf_ref, kdf_ref, False),
            (rb_ref, vb_ref, kkb_ref, lwb_ref, btb_ref, kdb_ref, True))
    at_l, rt_l, v_l, bk_l, pt_l, m_l = [], [], [], [], [], []
    for t in range(nsub):
        for r_ref, v_ref, kk_ref, lw_ref, bt_ref, kd_ref, rev in dirs:
            ci = nsub - 1 - t if rev else t
            rows = slice(ci * c, (ci + 1) * c)
            incl = (j_i >= t_i) if rev else (j_i <= t_i)
            tri = jnp.where(incl, 1.0, 0.0).astype(BF16)
            before = (jj2 > tt2) if rev else (jj2 < tt2)
            mask_m = jnp.logical_or(before, jnp.logical_and(t2 >= c, jj2 == tt2))
            lw = lw_ref[0, rows, :]
            cs = _dot_exact_lhs(tri, lw)
            tot = cs[0:1, :] if rev else cs[c - 1:c, :]
            p_inv = jnp.exp(-cs)
            p_rest = jnp.exp(tot - cs)
            bt = bt_ref[0, rows, :]
            kd = kd_ref[0, rows, :]
            rt = heads(r_ref[0, rows, :] * jnp.exp(cs))
            at = heads(-kk_ref[0, rows, :] * jnp.exp(cs - lw))
            bti = heads(bt * p_inv)
            kti = heads(kd * p_inv)
            bp = heads(bt * p_rest)
            kp = heads(kd * p_rest)
            x1 = jnp.stack([jnp.concatenate([a, b], axis=0) for a, b in zip(at, rt)])
            x2 = jnp.stack([jnp.concatenate([a, b], axis=0) for a, b in zip(bti, kti)])
            m_l.append(jnp.where(mask_m[None], _bmm(x1, x2, _NT), 0.0))
            at_l += at
            rt_l += rt
            v_l += heads(v_ref[0, rows, :])
            bk_l += [jnp.concatenate([a, b], axis=0) for a, b in zip(bp, kp)]
            pt_l += heads(jnp.exp(tot))
    m = jnp.concatenate(m_l, axis=0)
    at_s = jnp.stack(at_l)
    rt_s = jnp.stack(rt_l)
    v_s = jnp.stack(v_l)
    bk_s = jnp.stack(bk_l)
    pt_s = jnp.stack(pt_l)

    a_full = m[:, 0:c, 0:c]
    d_pow = jnp.where(same_blk[None], a_full, 0.0)
    l_off = a_full - d_pow
    t_diag = eye[None] + d_pow
    for _ in range(blk_bits - 1):
        d_pow = _bmm(d_pow, d_pow)
        t_diag = t_diag + _bmm(d_pow, t_diag)
    n_mat = _bmm(t_diag, l_off)
    x_mat = t_diag + _bmm(n_mat, t_diag)
    tinv = x_mat + _bmm(_bmm(n_mat, n_mat), x_mat)

    akv = _bmm(m[:, 0:c, c:2 * c], v_s)

    st = st_ref[...]
    for t in range(nsub):
        sl = slice(t * g, (t + 1) * g)
        u = _bmm(tinv[sl], akv[sl] + _bmm(at_s[sl], st, _NT))
        uv = jnp.concatenate([u, v_s[sl]], axis=1)
        y = _bmm(m[sl, c:2 * c, :], uv) + _bmm(rt_s[sl], st, _NT)
        st = pt_s[sl] * st + _bmm(uv, bk_s[sl], _TN)
        rf = slice(t * c, (t + 1) * c)
        rb = slice((nsub - 1 - t) * c, (nsub - t) * c)
        for h in range(RWKV_HEADS):
            yf_ref[0, rf, h * n:(h + 1) * n] = y[h]
            yb_ref[0, rb, h * n:(h + 1) * n] = y[RWKV_HEADS + h]
    st_ref[...] = st


def _rwkv_scan(r, v, kk, lw_f, lw_b, bt_f, bt_b, kd_f, kd_b, n_lat):
    b, ta, w = r.shape
    nsub = SCAN_CHUNKS
    rows = nsub * CHUNK
    assert n_lat % rows == 0 and ta % rows == 0
    nc = ta // rows
    nlat = n_lat // rows
    nctx = nc - nlat

    def fwd_idx(s):
        return jnp.where(s < nctx, nlat + s, s - nctx)

    def rev_idx(s):
        return nc - 1 - s

    fwd = pl.BlockSpec((1, rows, w), lambda bb, s: (bb, fwd_idx(s), 0))
    bwd = pl.BlockSpec((1, rows, w), lambda bb, s: (bb, rev_idx(s), 0))
    return pl.pallas_call(
        functools.partial(_scan_kernel, nsub=nsub),
        grid=(b, nc),
        in_specs=[fwd] * 6 + [bwd] * 6,
        out_specs=[fwd, bwd],
        out_shape=[jax.ShapeDtypeStruct((b, ta, w), F32)] * 2,
        scratch_shapes=[pltpu.VMEM((2 * RWKV_HEADS, RWKV_HEAD, RWKV_HEAD), F32)],
        compiler_params=_cparams(("parallel", "arbitrary")),
        name="rwkv_scan",
    )(r, v, kk, lw_f, bt_f, kd_f, r, v, kk, lw_b, bt_b, kd_b)


def _postmix_kernel(x_ref, att_ref, yf_ref, yb_ref, r_ref, v_ref, kdf_ref, kdb_ref, g_ref,
                    bgt_ref, cgt_ref, wo_ref, lnw_ref, lnb_ref, rk_ref, seg_ref, o_ref, *, tm, n_lat):
    i = pl.program_id(1)
    is_ctx = _is_ctx_rows(i, tm, n_lat)
    seg = seg_ref[...]
    y = yf_ref[0] + yb_ref[0]
    inv_n = 1.0 / RWKV_HEAD
    mean = _dot_exact_rhs(y, seg) * inv_n
    dy = y - mean
    var = _dot_exact_rhs(dy * dy, seg) * inv_n
    o = dy * lax.rsqrt(var + LNX_EPS) * lnw_ref[...] + lnb_ref[...]
    r = r_ref[0]
    v = v_ref[0]
    rrk = r * rk_ref[...]
    o = o + _dot_exact_rhs(rrk * kdf_ref[0], seg) * v
    o = o + _dot_exact_rhs(rrk * kdb_ref[0], seg) * v
    rw = o * g_ref[0]
    mixed = (lax.dot_general(att_ref[0], wo_ref[0:ATTN_WIDTH, :], (((0,), (0,)), ((), ())),
                             preferred_element_type=F32)
             + jnp.dot(rw.astype(BF16), wo_ref[ATTN_WIDTH:, :], preferred_element_type=F32))
    gt = jnp.where(is_ctx, cgt_ref[...], bgt_ref[0])
    o_ref[0] = x_ref[0] + gt * mixed


def _post_mix(xa, att, y_f, y_b, r, v, kd_f, kd_b, g, modb, modc, w_out, lnw, lnb, r_k, seg, n_lat, tm):
    b, ta, d = xa.shape
    w = RWKV_WIDTH
    tok = lambda ww: pl.BlockSpec((1, tm, ww), lambda bb, i: (bb, i, 0))
    full = lambda a: pl.BlockSpec(a.shape, lambda bb, i, _n=a.ndim: (0,) * _n)
    consts = [w_out, lnw.reshape(1, w), lnb.reshape(1, w), r_k.reshape(1, w), seg]
    return pl.pallas_call(
        functools.partial(_postmix_kernel, tm=tm, n_lat=n_lat),
        grid=(b, ta // tm),
        in_specs=[tok(d), pl.BlockSpec((1, ATTN_WIDTH, tm), lambda bb, i: (bb, 0, i))] + [tok(w)] * 7
        + [pl.BlockSpec((1, 1, d), lambda bb, i: (bb, 0, 2)), pl.BlockSpec((1, d), lambda bb, i: (0, 2))]
        + [full(a) for a in consts],
        out_specs=tok(d),
        out_shape=jax.ShapeDtypeStruct((b, ta, d), F32),
        compiler_params=_cparams(("parallel", "parallel")),
        name="rwkv_finish_out_proj",
    )(xa, att, y_f, y_b, r, v, kd_f, kd_b, g, modb, modc, *consts)


def _ffn_kernel(x_ref, g_ref, bsh_ref, bsc_ref, csh_ref, csc_ref, bgt_ref, cgt_ref,
                w1_ref, w3_ref, w2_ref, o_ref, *, tm, n_lat):
    i = pl.program_id(1)
    is_ctx = _is_ctx_rows(i, tm, n_lat)
    x = x_ref[0]
    h = _modnorm(x, g_ref[...], bsc_ref[0], bsh_ref[0], csc_ref[...], csh_ref[...], is_ctx).astype(BF16)
    a = jnp.dot(h, w1_ref[...], preferred_element_type=F32)
    bb = jnp.dot(h, w3_ref[...], preferred_element_type=F32)
    act = (a * jax.nn.sigmoid(a) * bb).astype(BF16)
    out = jnp.dot(act, w2_ref[...], preferred_element_type=F32)
    gt = jnp.where(is_ctx, cgt_ref[...], bgt_ref[0])
    o_ref[0] = x + gt * out


def _dense_ffn(xa, g2, modb, modc, w1, w3, w2, n_lat, tm):
    b, ta, d = xa.shape
    tok = pl.BlockSpec((1, tm, d), lambda bb, i: (bb, i, 0))
    full = lambda a: pl.BlockSpec(a.shape, lambda bb, i, _n=a.ndim: (0,) * _n,
                                  pipeline_mode=pl.Buffered(1))
    return pl.pallas_call(
        functools.partial(_ffn_kernel, tm=tm, n_lat=n_lat),
        grid=(b, ta // tm),
        in_specs=[tok, pl.BlockSpec((1, d), lambda bb, i: (0, 0))] + _mod_specs(d, 3, 4)
        + [pl.BlockSpec((1, 1, d), lambda bb, i: (bb, 0, 5)), pl.BlockSpec((1, d), lambda bb, i: (0, 5))]
        + [full(w1), full(w3), full(w2)],
        out_specs=tok,
        out_shape=jax.ShapeDtypeStruct((b, ta, d), F32),
        compiler_params=_cparams(("parallel", "parallel")),
        name="norm2_dense_swiglu",
    )(xa, g2, modb, modb, modc, modc, modb, modc, w1, w3, w2)


def _moe_kernel(x_ref, g_ref, bsh_ref, bsc_ref, csh_ref, csc_ref, bgt_ref, cgt_ref, rw_ref,
                w1_ref, w3_ref, w2_ref, o_ref, h_ref, comb_ref, acc_ref, *, tm, n_lat):
    i = pl.program_id(1)
    e = pl.program_id(2)
    f = pl.program_id(3)
    last = jnp.logical_and(e == pl.num_programs(2) - 1, f == pl.num_programs(3) - 1)

    @pl.when(jnp.logical_and(e == 0, f == 0))
    def _route():
        is_ctx = _is_ctx_rows(i, tm, n_lat)
        h = _modnorm(x_ref[0], g_ref[...], bsc_ref[0], bsh_ref[0], csc_ref[...], csh_ref[...], is_ctx)
        h_ref[...] = h.astype(BF16)
        hh, hl = _split(h)
        wh, wl = _split(rw_ref[...])
        logits = (jnp.dot(hh, wh, preferred_element_type=F32) + jnp.dot(hl, wh, preferred_element_type=F32)
                  + jnp.dot(hh, wl, preferred_element_type=F32))
        lane = lax.broadcasted_iota(jnp.int32, logits.shape, 1).astype(F32)
        neg = -jnp.inf
        logits = jnp.where(lane < N_EXPERTS, logits, neg)
        v1 = jnp.max(logits, axis=1, keepdims=True)
        i1 = jnp.min(jnp.where(logits == v1, lane, float(LANES)), axis=1, keepdims=True)
        rest = jnp.where(lane == i1, neg, logits)
        v2 = jnp.max(rest, axis=1, keepdims=True)
        i2 = jnp.min(jnp.where(rest == v2, lane, float(LANES)), axis=1, keepdims=True)
        e2 = jnp.exp(v2 - v1)
        den = 1.0 + e2
        comb_ref[...] = jnp.where(lane == i1, 1.0 / den, 0.0) + jnp.where(lane == i2, e2 / den, 0.0)
        acc_ref[...] = jnp.zeros(acc_ref.shape, F32)

    h = h_ref[...]
    lane = lax.broadcasted_iota(jnp.int32, comb_ref.shape, 1)
    gate = jnp.sum(jnp.where(lane == e, comb_ref[...], 0.0), axis=1, keepdims=True)
    a = jnp.dot(h, w1_ref[0], preferred_element_type=F32)
    bb = jnp.dot(h, w3_ref[0], preferred_element_type=F32)
    act = (a * jax.nn.sigmoid(a) * bb * gate).astype(BF16)
    acc_ref[...] += jnp.dot(act, w2_ref[0], preferred_element_type=F32)

    @pl.when(last)
    def _finish():
        is_ctx = _is_ctx_rows(i, tm, n_lat)
        gt = jnp.where(is_ctx, cgt_ref[...], bgt_ref[0])
        o_ref[0] = x_ref[0] + gt * acc_ref[...]


def _moe_ffn(xa, g2, modb, modc, router_p, w1, w3, w2, n_lat, tm, nf):
    b, ta, d = xa.shape
    ne, _, ff = w1.shape
    tf = ff // nf
    tok = pl.BlockSpec((1, tm, d), lambda bb, i, e, f: (bb, i, 0))
    return pl.pallas_call(
        functools.partial(_moe_kernel, tm=tm, n_lat=n_lat),
        grid=(b, ta // tm, ne, nf),
        in_specs=[tok, pl.BlockSpec((1, d), lambda bb, i, e, f: (0, 0))] + _mod_specs(d, 3, 4)
        + [pl.BlockSpec((1, 1, d), lambda bb, i, e, f: (bb, 0, 5)),
           pl.BlockSpec((1, d), lambda bb, i, e, f: (0, 5)),
           pl.BlockSpec((d, LANES), lambda bb, i, e, f: (0, 0)),
           pl.BlockSpec((1, d, tf), lambda bb, i, e, f: (e, 0, f)),
           pl.BlockSpec((1, d, tf), lambda bb, i, e, f: (e, 0, f)),
           pl.BlockSpec((1, tf, d), lambda bb, i, e, f: (e, f, 0))],
        out_specs=tok,
        out_shape=jax.ShapeDtypeStruct((b, ta, d), F32),
        scratch_shapes=[pltpu.VMEM((tm, d), BF16), pltpu.VMEM((tm, LANES), F32), pltpu.VMEM((tm, d), F32)],
        compiler_params=_cparams(("parallel", "parallel", "arbitrary", "arbitrary")),
        name="norm2_moe_swiglu",
    )(xa, g2, modb, modb, modc, modc, modb, modc, router_p, w1, w3, w2)


def _final_kernel(x_ref, g_ref, o_ref):
    x = x_ref[0]
    ms = jnp.mean(x * x, axis=-1, keepdims=True)
    o_ref[0] = x * lax.rsqrt(ms + NORM_EPS) * g_ref[...]


def _final_norm(xa, g, seq, tm):
    b, _, d = xa.shape
    return pl.pallas_call(
        _final_kernel,
        grid=(b, seq // tm),
        in_specs=[pl.BlockSpec((1, tm, d), lambda bb, i: (bb, i, 0)),
                  pl.BlockSpec((1, d), lambda bb, i: (0, 0))],
        out_specs=pl.BlockSpec((1, tm, d), lambda bb, i: (bb, i, 0)),
        out_shape=jax.ShapeDtypeStruct((b, seq, d), F32),
        compiler_params=_cparams(("parallel", "parallel")),
        name="final_norm",
    )(xa, g)


def _rope_tables(seq, ctx_len):
    axis_dim = ATTN_QK_DIM // 2
    inv_freq = ROPE_THETA ** (-jnp.arange(0, axis_dim, 2, dtype=F32) / axis_dim)
    t = jnp.arange(seq)
    pos = jnp.stack([(t // GRID_W).astype(F32), (t % GRID_W).astype(F32)], axis=1)
    lane = np.arange(LANES)
    axis = (lane % ATTN_QK_DIM) // axis_dim
    freq = lane % (axis_dim // 2)
    sign = np.where((lane % axis_dim) < axis_dim // 2, -1.0, 1.0).astype(np.float32)
    ang = pos[:, axis] * inv_freq[freq][None, :]
    cos_t = jnp.concatenate([jnp.cos(ang), jnp.ones((ctx_len, LANES), F32)], axis=0)
    sin_t = jnp.concatenate([jnp.sin(ang) * sign[None, :], jnp.zeros((ctx_len, LANES), F32)], axis=0)
    return cos_t, sin_t


def _pick_tile(n, candidates):
    for c in candidates:
        if n % c == 0:
            return c
    raise ValueError(f"no tile in {candidates} divides {n}")


def kernel(x, c, ctx, c_ctx, ada_w, ada_b, norm1_g, norm2_g, w_in, w_out, diff_lambda, subln_g, rwkv_mu, rwkv_w0, rwkv_w2, rwkv_a0, rwkv_a2, rwkv_v0, rwkv_v1, rwkv_v2, rwkv_g2, rwkv_k_k, rwkv_k_a, rwkv_r_k, lnx_w, lnx_b, ffn_w1, ffn_w3, ffn_w2, router_w, exp_w1, exp_w3, exp_w2, final_g):
    b, seq, d = x.shape
    n_lat = seq
    ctx_len = ctx.shape[1]
    depth = ada_w.shape[0]
    ta = n_lat + ctx_len
    assert b + 1 <= 8 and seq % GRID_W == 0 and ctx_len % CHUNK == 0 and seq % CHUNK == 0

    tm = _pick_tile(math.gcd(ctx_len, seq), (256, 128))
    tq = _pick_tile(seq, (512, 256, 128))
    tk = _pick_tile(ta, (3328, 1280, 640, 256, 128))
    tm_moe = _pick_tile(ta, (640, 256, 128))

    xa = jnp.concatenate([x, ctx], axis=1)
    cvec = jnp.zeros((8, d), F32).at[:b].set(c).at[b].set(c_ctx)
    mod = _modulation(cvec, ada_w, ada_b)
    cos_t, sin_t = _rope_tables(seq, ctx_len)
    seg = jnp.asarray(np.kron(np.eye(RWKV_HEADS), np.ones((RWKV_HEAD, RWKV_HEAD))), BF16)
    n_in = w_in.shape[2]
    n_in_p = 3 * ATTN_WIDTH + 3 * RWKV_WIDTH + LORA_PAD
    router_p = jnp.pad(router_w, ((0, 0), (0, 0), (0, LANES - router_w.shape[2])))

    v_first = None
    for l in range(depth):
        lam_init = 0.8 - 0.6 * math.exp(-0.3 * l)
        modb = mod[l, :b].reshape(b, 1, 6 * d)
        modc = mod[l, b:b + 1]
        w_in_p = jnp.pad(w_in[l], ((0, 0), (0, n_in_p - n_in))).astype(BF16)
        q, k, v, pr, pk, pv, lora = _in_proj(xa, norm1_g[l].reshape(1, d), modb, modc, w_in_p,
                                             cos_t, sin_t, n_lat, tm)
        sg = subln_g[l].reshape(ATTN_V_DIM, 1)
        att_lat = _attention(q, k, v, diff_lambda[l], sg, lam_init, 0, seq, 0, ta, tq, tk)
        att_ctx = _attention(q, k, v, diff_lambda[l], sg, lam_init, seq, ctx_len, seq, ctx_len,
                             ctx_len, ctx_len)
        att = jnp.concatenate([att_lat, att_ctx], axis=2)
        vres = None if l == 0 else (v_first, rwkv_v0[l - 1], rwkv_v1[l - 1], rwkv_v2[l - 1])
        r, vv, kk, g, lw_f, lw_b, bt_f, bt_b, kd_f, kd_b = _rwkv_prepare(
            pr, pk, pv, lora, rwkv_mu[l], rwkv_w0[l], rwkv_w2[l], rwkv_a0[l], rwkv_a2[l], rwkv_g2[l],
            rwkv_k_k[l], rwkv_k_a[l], seg, vres, n_lat, tm)
        if l == 0:
            v_first = vv
        y_f, y_b = _rwkv_scan(r, vv, kk, lw_f, lw_b, bt_f, bt_b, kd_f, kd_b, n_lat)
        xa = _post_mix(xa, att, y_f, y_b, r, vv, kd_f, kd_b, g, modb, modc, w_out[l].astype(BF16),
                       lnx_w[l], lnx_b[l], rwkv_r_k[l], seg, n_lat, tm)
        i = l // 2
        if l % 2 == 0:
            xa = _dense_ffn(xa, norm2_g[l].reshape(1, d), modb, modc, ffn_w1[i].astype(BF16),
                            ffn_w3[i].astype(BF16), ffn_w2[i].astype(BF16), n_lat, tm)
        else:
            xa = _moe_ffn(xa, norm2_g[l].reshape(1, d), modb, modc, router_p[i],
                          exp_w1[i].astype(BF16), exp_w3[i].astype(BF16), exp_w2[i].astype(BF16),
                          n_lat, tm_moe, 2)
    return _final_norm(xa, final_g.reshape(1, d), seq, tm)
```

```python
import functools
import math

import numpy as np
import jax
import jax.numpy as jnp
from jax import lax
from jax.experimental import pallas as pl
from jax.experimental.pallas import tpu as pltpu

F32 = jnp.float32
BF16 = jnp.bfloat16

ATTN_HEADS = 4
ATTN_QK_DIM = 64
ATTN_V_DIM = 128
ATTN_WIDTH = ATTN_HEADS * ATTN_V_DIM
RWKV_HEADS = 8
RWKV_HEAD = 64
RWKV_WIDTH = RWKV_HEADS * RWKV_HEAD
DECAY_LORA = 32
AAA_LORA = 32
GATE_LORA = 96
LORA_PAD = 256
N_EXPERTS = 8
GRID_W = 64
ROPE_THETA = 10000.0
NORM_EPS = 1e-6
SUBLN_EPS = 1e-5
LNX_EPS = 64e-5
LANES = 128
CHUNK = 64
INV_BLOCK = 16
MOE_SLOT_BLOCK = 128
SCAN_CHUNKS = 2
QK_LOOKAHEAD = 2
V_ONES = 16
V_EXT = ATTN_V_DIM + V_ONES
VMEM_LIMIT = 56 * 1024 * 1024
Q_SCALE = ATTN_QK_DIM ** -0.5 * math.log2(math.e)


def _cparams(sem, vmem=VMEM_LIMIT, flags=None):
    return pltpu.CompilerParams(dimension_semantics=sem, vmem_limit_bytes=vmem, flags=flags)


def _bdot(a, b):
    return jnp.dot(a.astype(BF16), b.astype(BF16), preferred_element_type=F32)


def _bdot_nt(a, b):
    return lax.dot_general(a.astype(BF16), b.astype(BF16), (((1,), (1,)), ((), ())),
                           preferred_element_type=F32)


def _bdot_tn(a, b):
    return lax.dot_general(a.astype(BF16), b.astype(BF16), (((0,), (0,)), ((), ())),
                           preferred_element_type=F32)


def _split(a):
    hi = a.astype(BF16)
    lo = (a - hi.astype(F32)).astype(BF16)
    return hi, lo


def _dot_exact_rhs(a, b_bf16):
    hi, lo = _split(a)
    return (jnp.dot(hi, b_bf16, preferred_element_type=F32)
            + jnp.dot(lo, b_bf16, preferred_element_type=F32))


def _dot_exact_lhs(a_bf16, b):
    hi, lo = _split(b)
    return (jnp.dot(a_bf16, hi, preferred_element_type=F32)
            + jnp.dot(a_bf16, lo, preferred_element_type=F32))


def _modnorm(x, g, sc_b, sh_b, sc_c, sh_c, is_ctx):
    ms = jnp.mean(x * x, axis=-1, keepdims=True)
    y = x * lax.rsqrt(ms + NORM_EPS) * g
    sc = jnp.where(is_ctx, sc_c, sc_b)
    sh = jnp.where(is_ctx, sh_c, sh_b)
    return y * (1.0 + sc) + sh


def _is_ctx_rows(tile_idx, tm, n_lat):
    row = tile_idx * tm + lax.broadcasted_iota(jnp.int32, (tm, 1), 0)
    return row >= n_lat


def _mod_specs(d, k_sh, k_sc):
    return [
        pl.BlockSpec((1, 1, d), lambda b, i, *_: (b, 0, k_sh)),
        pl.BlockSpec((1, 1, d), lambda b, i, *_: (b, 0, k_sc)),
        pl.BlockSpec((1, d), lambda b, i, *_: (0, k_sh)),
        pl.BlockSpec((1, d), lambda b, i, *_: (0, k_sc)),
    ]


def _mod_kernel(c_ref, w_ref, b_ref, o_ref):
    c = c_ref[...]
    s = c * jax.nn.sigmoid(c)
    o_ref[0] = _bdot(s, w_ref[0]) + b_ref[0]


def _modulation(cvec, ada_w, ada_b):
    depth, d, n = ada_w.shape
    tn = 2048
    return pl.pallas_call(
        _mod_kernel,
        grid=(depth, n // tn),
        in_specs=[pl.BlockSpec((8, d), lambda l, j: (0, 0)),
                  pl.BlockSpec((1, d, tn), lambda l, j: (l, 0, j)),
                  pl.BlockSpec((1, 1, tn), lambda l, j: (l, 0, j))],
        out_specs=pl.BlockSpec((1, 8, tn), lambda l, j: (l, 0, j)),
        out_shape=jax.ShapeDtypeStruct((depth, 8, n), F32),
        compiler_params=_cparams(("parallel", "parallel")),
        name="adaln_mod",
    )(cvec, ada_w, ada_b.reshape(depth, 1, n))


def _inproj_kernel(x_ref, g_ref, bsh_ref, bsc_ref, csh_ref, csc_ref, w_ref, cos_ref, sin_ref,
                   q_ref, k_ref, v_ref, rr_ref, rk_ref, rv_ref, lo_ref, *, tm, n_lat):
    i = pl.program_id(1)
    is_ctx = _is_ctx_rows(i, tm, n_lat)
    h = _modnorm(x_ref[0], g_ref[...], bsc_ref[0], bsh_ref[0], csc_ref[...], csh_ref[...], is_ctx)
    p = jnp.dot(h.astype(BF16), w_ref[...], preferred_element_type=F32)
    cos_t = cos_ref[...]
    sin_t = sin_ref[...]
    lane = lax.broadcasted_iota(jnp.int32, (tm, LANES), 1)
    first = (lane & 31) < 16

    def rope(z):
        partner = jnp.where(first, pltpu.roll(z, LANES - 16, 1), pltpu.roll(z, 16, 1))
        return z * cos_t + partner * sin_t

    for hh in range(ATTN_HEADS):
        lo = LANES * hh
        q_ref[0, lo:lo + LANES, :] = jnp.transpose(rope(p[:, lo:lo + LANES]) * Q_SCALE).astype(BF16)
        k_ref[0, :, lo:lo + LANES] = rope(p[:, ATTN_WIDTH + lo:ATTN_WIDTH + lo + LANES]).astype(BF16)
        vo = 2 * ATTN_WIDTH + lo
        ve = V_EXT * hh
        v_ref[0, ve:ve + ATTN_V_DIM, :] = jnp.transpose(p[:, vo:vo + LANES]).astype(BF16)
        v_ref[0, ve + ATTN_V_DIM:ve + V_EXT, :] = jnp.ones((V_ONES, tm), BF16)
    o = 3 * ATTN_WIDTH
    rr_ref[0] = p[:, o:o + RWKV_WIDTH]
    rk_ref[0] = p[:, o + RWKV_WIDTH:o + 2 * RWKV_WIDTH]
    rv_ref[0] = p[:, o + 2 * RWKV_WIDTH:o + 3 * RWKV_WIDTH]
    o += 3 * RWKV_WIDTH
    lo_ref[0] = p[:, o:o + LORA_PAD]


def _in_proj(xa, g1, modb, modc, w_in_p, cos_t, sin_t, n_lat, tm):
    b, ta, d = xa.shape
    n = w_in_p.shape[1]
    tok = lambda w: pl.BlockSpec((1, tm, w), lambda bb, i: (bb, i, 0))
    feat = pl.BlockSpec((1, ATTN_WIDTH, tm), lambda bb, i: (bb, 0, i))
    feat_shape = jax.ShapeDtypeStruct((b, ATTN_WIDTH, ta), BF16)
    vext = pl.BlockSpec((1, ATTN_HEADS * V_EXT, tm), lambda bb, i: (bb, 0, i))
    vext_shape = jax.ShapeDtypeStruct((b, ATTN_HEADS * V_EXT, ta), BF16)
    outs = [feat_shape, jax.ShapeDtypeStruct((b, ta, ATTN_WIDTH), BF16), vext_shape] \
        + [jax.ShapeDtypeStruct((b, ta, RWKV_WIDTH), F32)] * 3 \
        + [jax.ShapeDtypeStruct((b, ta, LORA_PAD), F32)]
    return pl.pallas_call(
        functools.partial(_inproj_kernel, tm=tm, n_lat=n_lat),
        grid=(b, ta // tm),
        in_specs=[tok(d), pl.BlockSpec((1, d), lambda bb, i: (0, 0))] + _mod_specs(d, 0, 1)
        + [pl.BlockSpec((d, n), lambda bb, i: (0, 0)),
           pl.BlockSpec((tm, LANES), lambda bb, i: (i, 0)),
           pl.BlockSpec((tm, LANES), lambda bb, i: (i, 0))],
        out_specs=[feat, tok(ATTN_WIDTH), vext] + [tok(RWKV_WIDTH)] * 3 + [tok(LORA_PAD)],
        out_shape=outs,
        compiler_params=_cparams(("parallel", "parallel")),
        name="norm1_in_proj",
    )(xa, g1, modb, modb, modc, modc, w_in_p, cos_t, sin_t)


def _attn_kernel(lam_ref, sg_ref, q_ref, k_ref, v_ref, o_ref, qs_ref, m_ref, acc_ref,
                 *, tq, tk, ts, lam_init):
    ki = pl.program_id(3)
    nk = pl.num_programs(3)

    @pl.when(ki == 0)
    def _init():
        q = q_ref[0]
        feat = lax.broadcasted_iota(jnp.int32, (LANES, tq), 0)
        zero = jnp.zeros_like(q)
        qs_ref[:, 0:tq] = jnp.where(feat < ATTN_QK_DIM, q, zero)
        qs_ref[:, tq:2 * tq] = jnp.where(feat >= ATTN_QK_DIM, q, zero)
        m_ref[...] = jnp.full(m_ref.shape, -jnp.inf, F32)
        acc_ref[...] = jnp.zeros(acc_ref.shape, F32)

    n_sub = tk // ts
    qs = qs_ref[...]
    qk = lambda j: jnp.dot(k_ref[0, j * ts:(j + 1) * ts, :], qs, preferred_element_type=F32)
    m_run = m_ref[...]
    acc = acc_ref[...]
    pending = [qk(j) for j in range(min(QK_LOOKAHEAD, n_sub))]
    for j in range(n_sub):
        if j + QK_LOOKAHEAD < n_sub:
            pending.append(qk(j + QK_LOOKAHEAD))
        s_cur = pending.pop(0)
        m_new = jnp.maximum(m_run, jnp.max(s_cur, axis=0, keepdims=True))
        alpha = jnp.exp2(m_run - m_new)
        p = jnp.exp2(s_cur - m_new).astype(BF16)
        acc = alpha * acc + jnp.dot(v_ref[0, :, j * ts:(j + 1) * ts], p, preferred_element_type=F32)
        m_run = m_new
    m_ref[...] = m_run
    acc_ref[...] = acc

    @pl.when(ki == nk - 1)
    def _finish():
        lp = lam_ref[...]
        lam = (jnp.exp(jnp.sum(lp[0:1] * lp[1:2], axis=1, keepdims=True))
               - jnp.exp(jnp.sum(lp[2:3] * lp[3:4], axis=1, keepdims=True)) + lam_init)
        l = acc_ref[ATTN_V_DIM:ATTN_V_DIM + 1, :]
        o1 = acc_ref[0:ATTN_V_DIM, 0:tq] / l[:, 0:tq]
        o2 = acc_ref[0:ATTN_V_DIM, tq:2 * tq] / l[:, tq:2 * tq]
        o = o1 - lam * o2
        ms = jnp.mean(o * o, axis=0, keepdims=True)
        y = o * lax.rsqrt(ms + SUBLN_EPS) * sg_ref[...]
        o_ref[0] = (y * (1.0 - lam_init)).astype(BF16)


def _attention(q_t, k, v_t, lam_p, subln_g, lam_init, q0, nq, k0, nkeys, tq, tk):
    b = k.shape[0]
    qo, ko = q0 // tq, k0 // tk
    return pl.pallas_call(
        functools.partial(_attn_kernel, tq=tq, tk=tk, ts=_pick_tile(tk, (256, LANES)), lam_init=lam_init),
        grid=(b, ATTN_HEADS, nq // tq, nkeys // tk),
        in_specs=[pl.BlockSpec((4, ATTN_QK_DIM), lambda bb, h, i, j: (0, 0)),
                  pl.BlockSpec((ATTN_V_DIM, 1), lambda bb, h, i, j: (0, 0)),
                  pl.BlockSpec((1, LANES, tq), lambda bb, h, i, j: (bb, h, i + qo)),
                  pl.BlockSpec((1, tk, LANES), lambda bb, h, i, j: (bb, j + ko, h)),
                  pl.BlockSpec((1, V_EXT, tk), lambda bb, h, i, j: (bb, h, j + ko))],
        out_specs=pl.BlockSpec((1, ATTN_V_DIM, tq), lambda bb, h, i, j: (bb, h, i)),
        out_shape=jax.ShapeDtypeStruct((b, ATTN_WIDTH, nq), BF16),
        scratch_shapes=[pltpu.VMEM((LANES, 2 * tq), BF16),
                        pltpu.VMEM((1, 2 * tq), F32),
                        pltpu.VMEM((V_EXT, 2 * tq), F32)],
        compiler_params=_cparams(("parallel", "parallel", "parallel", "arbitrary")),
        name="diff_attention",
    )(lam_p, subln_g, q_t, k, v_t)


def _prep_kernel(*refs, tt, n_lat, ta, has_vres):
    (pr_ref, pk_ref, pv_ref, hpr_ref, hpk_ref, hpv_ref, hnr_ref, hnk_ref, hnv_ref, lo_ref,
     mu_ref, w0_ref, w2_ref, a0_ref, a2_ref, g2_ref, kk_ref, ka_ref, seg_ref) = refs[:19]
    n_in = 19
    if has_vres:
        vf_ref, v0_ref, v1_ref, v2_ref = refs[19:23]
        n_in = 23
    (r_out, v_out, kk_out, g_out, lwf_out, lwb_out, bf_out, bb_out, kdf_out, kdb_out) = refs[n_in:]

    i = pl.program_id(1)
    row = lax.broadcasted_iota(jnp.int32, (tt, 1), 0)
    gidx = i * tt + row
    no_prev = jnp.logical_or(gidx == 0, gidx == n_lat)
    no_next = jnp.logical_or(gidx == n_lat - 1, gidx == ta - 1)

    def shift_mix(p_ref, hp_ref, hn_ref, j):
        p = p_ref[0]
        pp = jnp.where(row == 0, hp_ref[0, 7:8, :], pltpu.roll(p, 1, 0))
        pp = jnp.where(no_prev, 0.0, pp)
        pn = jnp.where(row == tt - 1, hn_ref[0, 0:1, :], pltpu.roll(p, tt - 1, 0))
        pn = jnp.where(no_next, 0.0, pn)
        return p + mu_ref[2 * j:2 * j + 1, :] * (pp - p) + mu_ref[2 * j + 1:2 * j + 2, :] * (pn - p)

    r = shift_mix(pr_ref, hpr_ref, hnr_ref, 0)
    k = shift_mix(pk_ref, hpk_ref, hnk_ref, 1)
    v = shift_mix(pv_ref, hpv_ref, hnv_ref, 2)
    if has_vres:
        mix = jax.nn.sigmoid(v0_ref[...] + _bdot(_bdot(v, v1_ref[...]), v2_ref[...]))
        v = v + (vf_ref[0] - v) * mix
    lo = lo_ref[0]
    gd = lo[:, 2 * DECAY_LORA + 2 * AAA_LORA:2 * DECAY_LORA + 2 * AAA_LORA + GATE_LORA]
    g_out[0] = _bdot(jax.nn.sigmoid(gd), g2_ref[...])
    kk = k * kk_ref[...]
    ss = _dot_exact_rhs(kk * kk, seg_ref[...])
    kk = kk * lax.rsqrt(jnp.maximum(ss, 1e-24))
    r_out[0] = r
    v_out[0] = v
    kk_out[0] = kk
    lw_outs = (lwf_out, lwb_out)
    b_outs = (bf_out, bb_out)
    kd_outs = (kdf_out, kdb_out)
    for d in range(2):
        wd = lo[:, d * DECAY_LORA:(d + 1) * DECAY_LORA]
        wl = w0_ref[d:d + 1, :] + _bdot(jnp.tanh(wd), w2_ref[d])
        lw_outs[d][0] = -jax.nn.sigmoid(wl) * math.exp(-0.5)
        ad = lo[:, 2 * DECAY_LORA + d * AAA_LORA:2 * DECAY_LORA + (d + 1) * AAA_LORA]
        a = jax.nn.sigmoid(a0_ref[d:d + 1, :] + _bdot(ad, a2_ref[d]))
        b_outs[d][0] = a * kk
        kd_outs[d][0] = k * (1.0 + (a - 1.0) * ka_ref[...])


def _rwkv_prepare(pr, pk, pv, lora, mu, w0, w2, a0, a2, g2, k_k, k_a, seg, vres, n_lat, tt):
    b, ta, w = pr.shape
    nb8 = ta // 8
    tok = pl.BlockSpec((1, tt, w), lambda bb, i: (bb, i, 0))
    prev = pl.BlockSpec((1, 8, w), lambda bb, i: (bb, jnp.maximum(i * (tt // 8) - 1, 0), 0))
    nxt = pl.BlockSpec((1, 8, w), lambda bb, i: (bb, jnp.minimum((i + 1) * (tt // 8), nb8 - 1), 0))
    full = lambda a: pl.BlockSpec(a.shape, lambda bb, i, _n=a.ndim: (0,) * _n)
    consts = [mu.reshape(6, w), w0, w2, a0, a2, g2, k_k.reshape(1, w), k_a.reshape(1, w), seg]
    ins = [pr, pk, pv, pr, pk, pv, pr, pk, pv, lora] + consts
    specs = [tok] * 3 + [prev] * 3 + [nxt] * 3 \
        + [pl.BlockSpec((1, tt, LORA_PAD), lambda bb, i: (bb, i, 0))] + [full(a) for a in consts]
    if vres is not None:
        v_first, v0, v1, v2 = vres
        extra = [v0.reshape(1, w), v1, v2]
        ins += [v_first] + extra
        specs += [tok] + [full(a) for a in extra]
    return pl.pallas_call(
        functools.partial(_prep_kernel, tt=tt, n_lat=n_lat, ta=ta, has_vres=vres is not None),
        grid=(b, ta // tt),
        in_specs=specs,
        out_specs=[tok] * 10,
        out_shape=[jax.ShapeDtypeStruct((b, ta, w), F32)] * 10,
        compiler_params=_cparams(("parallel", "parallel")),
        name="rwkv_prepare",
    )(*ins)


_NN = (((2,), (1,)), ((0,), (0,)))
_NT = (((2,), (2,)), ((0,), (0,)))
_TN = (((1,), (1,)), ((0,), (0,)))


def _bmm(a, b, dims=_NN):
    return lax.dot_general(a.astype(BF16), b.astype(BF16), dims, preferred_element_type=F32)


def _scan_kernel(rf_ref, vf_ref, kkf_ref, lwf_ref, btf_ref, kdf_ref,
                 rb_ref, vb_ref, kkb_ref, lwb_ref, btb_ref, kdb_ref,
                 yf_ref, yb_ref, st_ref, *, nsub):
    s = pl.program_id(1)
    c = CHUNK
    n = RWKV_HEAD
    g = 2 * RWKV_HEADS

    @pl.when(s == 0)
    def _init():
        st_ref[...] = jnp.zeros(st_ref.shape, F32)

    t_i = lax.broadcasted_iota(jnp.int32, (c, c), 0)
    j_i = lax.broadcasted_iota(jnp.int32, (c, c), 1)
    t2 = lax.broadcasted_iota(jnp.int32, (2 * c, 2 * c), 0)
    j2 = lax.broadcasted_iota(jnp.int32, (2 * c, 2 * c), 1)
    tt2 = t2 & (c - 1)
    jj2 = j2 & (c - 1)
    eye = jnp.where(t_i == j_i, 1.0, 0.0)
    blk_bits = int(math.log2(INV_BLOCK))
    same_blk = (t_i >> blk_bits) == (j_i >> blk_bits)
    heads = lambda z: [z[:, h * n:(h + 1) * n] for h in range(RWKV_HEADS)]
    dirs = ((rf_ref, vf_ref, kkf_ref, lwf_ref, btf_ref, kdf_ref, False),
            (rb_ref, vb_ref, kkb_ref, lwb_ref, btb_ref, kdb_ref, True))
    at_l, rt_l, v_l, bk_l, pt_l, m_l = [], [], [], [], [], []
    for t in range(nsub):
        for r_ref, v_ref, kk_ref, lw_ref, bt_ref, kd_ref, rev in dirs:
            ci = nsub - 1 - t if rev else t
            rows = slice(ci * c, (ci + 1) * c)
            incl = (j_i >= t_i) if rev else (j_i <= t_i)
            tri = jnp.where(incl, 1.0, 0.0).astype(BF16)
            before = (jj2 > tt2) if rev else (jj2 < tt2)
            mask_m = jnp.logical_or(before, jnp.logical_and(t2 >= c, jj2 == tt2))
            lw = lw_ref[0, rows, :]
            cs = _dot_exact_lhs(tri, lw)
            tot = cs[0:1, :] if rev else cs[c - 1:c, :]
            p_inv = jnp.exp(-cs)
            p_rest = jnp.exp(tot - cs)
            bt = bt_ref[0, rows, :]
            kd = kd_ref[0, rows, :]
            rt = heads(r_ref[0, rows, :] * jnp.exp(cs))
            at = heads(-kk_ref[0, rows, :] * jnp.exp(cs - lw))
            bti = heads(bt * p_inv)
            kti = heads(kd * p_inv)
            bp = heads(bt * p_rest)
            kp = heads(kd * p_rest)
            x1 = jnp.stack([jnp.concatenate([a, b], axis=0) for a, b in zip(at, rt)])
            x2 = jnp.stack([jnp.concatenate([a, b], axis=0) for a, b in zip(bti, kti)])
            m_l.append(jnp.where(mask_m[None], _bmm(x1, x2, _NT), 0.0))
            at_l += at
            rt_l += rt
            v_l += heads(v_ref[0, rows, :])
            bk_l += [jnp.concatenate([a, b], axis=0) for a, b in zip(bp, kp)]
            pt_l += heads(jnp.exp(tot))
    m = jnp.concatenate(m_l, axis=0)
    at_s = jnp.stack(at_l)
    rt_s = jnp.stack(rt_l)
    v_s = jnp.stack(v_l)
    bk_s = jnp.stack(bk_l)
    pt_s = jnp.stack(pt_l)

    a_full = m[:, 0:c, 0:c]
    d_pow = jnp.where(same_blk[None], a_full, 0.0)
    l_off = a_full - d_pow
    t_diag = eye[None] + d_pow
    for _ in range(blk_bits - 1):
        d_pow = _bmm(d_pow, d_pow)
        t_diag = t_diag + _bmm(d_pow, t_diag)
    n_mat = _bmm(t_diag, l_off)
    x_mat = t_diag + _bmm(n_mat, t_diag)
    tinv = x_mat + _bmm(_bmm(n_mat, n_mat), x_mat)

    akv = _bmm(m[:, 0:c, c:2 * c], v_s)

    st = st_ref[...]
    for t in range(nsub):
        sl = slice(t * g, (t + 1) * g)
        u = _bmm(tinv[sl], akv[sl] + _bmm(at_s[sl], st, _NT))
        uv = jnp.concatenate([u, v_s[sl]], axis=1)
        y = _bmm(m[sl, c:2 * c, :], uv) + _bmm(rt_s[sl], st, _NT)
        st = pt_s[sl] * st + _bmm(uv, bk_s[sl], _TN)
        rf = slice(t * c, (t + 1) * c)
        rb = slice((nsub - 1 - t) * c, (nsub - t) * c)
        for h in range(RWKV_HEADS):
            yf_ref[0, rf, h * n:(h + 1) * n] = y[h]
            yb_ref[0, rb, h * n:(h + 1) * n] = y[RWKV_HEADS + h]
    st_ref[...] = st


def _rwkv_scan(r, v, kk, lw_f, lw_b, bt_f, bt_b, kd_f, kd_b, n_lat):
    b, ta, w = r.shape
    nsub = SCAN_CHUNKS
    rows = nsub * CHUNK
    assert n_lat % rows == 0 and ta % rows == 0
    nc = ta // rows
    nlat = n_lat // rows
    nctx = nc - nlat

    def fwd_idx(s):
        return jnp.where(s < nctx, nlat + s, s - nctx)

    def rev_idx(s):
        return nc - 1 - s

    fwd = pl.BlockSpec((1, rows, w), lambda bb, s: (bb, fwd_idx(s), 0))
    bwd = pl.BlockSpec((1, rows, w), lambda bb, s: (bb, rev_idx(s), 0))
    return pl.pallas_call(
        functools.partial(_scan_kernel, nsub=nsub),
        grid=(b, nc),
        in_specs=[fwd] * 6 + [bwd] * 6,
        out_specs=[fwd, bwd],
        out_shape=[jax.ShapeDtypeStruct((b, ta, w), F32)] * 2,
        scratch_shapes=[pltpu.VMEM((2 * RWKV_HEADS, RWKV_HEAD, RWKV_HEAD), F32)],
        compiler_params=_cparams(("parallel", "arbitrary")),
        name="rwkv_scan",
    )(r, v, kk, lw_f, bt_f, kd_f, r, v, kk, lw_b, bt_b, kd_b)


def _postmix_kernel(x_ref, att_ref, yf_ref, yb_ref, r_ref, v_ref, kdf_ref, kdb_ref, g_ref,
                    bgt_ref, cgt_ref, wo_ref, lnw_ref, lnb_ref, rk_ref, seg_ref, o_ref, *, tm, n_lat):
    i = pl.program_id(1)
    is_ctx = _is_ctx_rows(i, tm, n_lat)
    seg = seg_ref[...]
    y = yf_ref[0] + yb_ref[0]
    inv_n = 1.0 / RWKV_HEAD
    mean = _dot_exact_rhs(y, seg) * inv_n
    dy = y - mean
    var = _dot_exact_rhs(dy * dy, seg) * inv_n
    o = dy * lax.rsqrt(var + LNX_EPS) * lnw_ref[...] + lnb_ref[...]
    r = r_ref[0]
    v = v_ref[0]
    rrk = r * rk_ref[...]
    o = o + _dot_exact_rhs(rrk * kdf_ref[0], seg) * v
    o = o + _dot_exact_rhs(rrk * kdb_ref[0], seg) * v
    rw = o * g_ref[0]
    mixed = (lax.dot_general(att_ref[0], wo_ref[0:ATTN_WIDTH, :], (((0,), (0,)), ((), ())),
                             preferred_element_type=F32)
             + jnp.dot(rw.astype(BF16), wo_ref[ATTN_WIDTH:, :], preferred_element_type=F32))
    gt = jnp.where(is_ctx, cgt_ref[...], bgt_ref[0])
    o_ref[0] = x_ref[0] + gt * mixed


def _post_mix(xa, att, y_f, y_b, r, v, kd_f, kd_b, g, modb, modc, w_out, lnw, lnb, r_k, seg, n_lat, tm):
    b, ta, d = xa.shape
    w = RWKV_WIDTH
    tok = lambda ww: pl.BlockSpec((1, tm, ww), lambda bb, i: (bb, i, 0))
    full = lambda a: pl.BlockSpec(a.shape, lambda bb, i, _n=a.ndim: (0,) * _n)
    consts = [w_out, lnw.reshape(1, w), lnb.reshape(1, w), r_k.reshape(1, w), seg]
    return pl.pallas_call(
        functools.partial(_postmix_kernel, tm=tm, n_lat=n_lat),
        grid=(b, ta // tm),
        in_specs=[tok(d), pl.BlockSpec((1, ATTN_WIDTH, tm), lambda bb, i: (bb, 0, i))] + [tok(w)] * 7
        + [pl.BlockSpec((1, 1, d), lambda bb, i: (bb, 0, 2)), pl.BlockSpec((1, d), lambda bb, i: (0, 2))]
        + [full(a) for a in consts],
        out_specs=tok(d),
        out_shape=jax.ShapeDtypeStruct((b, ta, d), F32),
        compiler_params=_cparams(("parallel", "parallel")),
        name="rwkv_finish_out_proj",
    )(xa, att, y_f, y_b, r, v, kd_f, kd_b, g, modb, modc, *consts)


def _ffn_kernel(x_ref, g_ref, bsh_ref, bsc_ref, csh_ref, csc_ref, bgt_ref, cgt_ref,
                w1_ref, w3_ref, w2_ref, o_ref, *, tm, n_lat):
    i = pl.program_id(1)
    is_ctx = _is_ctx_rows(i, tm, n_lat)
    x = x_ref[0]
    h = _modnorm(x, g_ref[...], bsc_ref[0], bsh_ref[0], csc_ref[...], csh_ref[...], is_ctx).astype(BF16)
    a = jnp.dot(h, w1_ref[...], preferred_element_type=F32)
    bb = jnp.dot(h, w3_ref[...], preferred_element_type=F32)
    act = (a * jax.nn.sigmoid(a) * bb).astype(BF16)
    out = jnp.dot(act, w2_ref[...], preferred_element_type=F32)
    gt = jnp.where(is_ctx, cgt_ref[...], bgt_ref[0])
    o_ref[0] = x + gt * out


def _dense_ffn(xa, g2, modb, modc, w1, w3, w2, n_lat, tm):
    b, ta, d = xa.shape
    tok = pl.BlockSpec((1, tm, d), lambda bb, i: (bb, i, 0))
    full = lambda a: pl.BlockSpec(a.shape, lambda bb, i, _n=a.ndim: (0,) * _n,
                                  pipeline_mode=pl.Buffered(1))
    return pl.pallas_call(
        functools.partial(_ffn_kernel, tm=tm, n_lat=n_lat),
        grid=(b, ta // tm),
        in_specs=[tok, pl.BlockSpec((1, d), lambda bb, i: (0, 0))] + _mod_specs(d, 3, 4)
        + [pl.BlockSpec((1, 1, d), lambda bb, i: (bb, 0, 5)), pl.BlockSpec((1, d), lambda bb, i: (0, 5))]
        + [full(w1), full(w3), full(w2)],
        out_specs=tok,
        out_shape=jax.ShapeDtypeStruct((b, ta, d), F32),
        compiler_params=_cparams(("parallel", "parallel")),
        name="norm2_dense_swiglu",
    )(xa, g2, modb, modb, modc, modc, modb, modc, w1, w3, w2)


_NT2 = (((1,), (1,)), ((), ()))
_TN2 = (((0,), (0,)), ((), ()))


def _route_kernel(x_ref, g_ref, bsh_ref, bsc_ref, csh_ref, csc_ref, rw_ref, tri_ref,
                  h_ref, gate_ref, pos_ref, cnt_ref, *, tm, n_lat):
    i = pl.program_id(1)
    is_ctx = _is_ctx_rows(i, tm, n_lat)
    h = _modnorm(x_ref[0], g_ref[...], bsc_ref[0], bsh_ref[0], csc_ref[...], csh_ref[...], is_ctx)
    h_ref[0] = h.astype(BF16)
    hh, hl = _split(h)
    wh, wl = _split(rw_ref[...])
    dg = lambda a, b: lax.dot_general(a, b, _NT2, preferred_element_type=F32)
    logits = dg(wh, hh) + dg(wh, hl) + dg(wl, hh)
    eidx = lax.broadcasted_iota(jnp.int32, logits.shape, 0).astype(F32)
    neg = -jnp.inf
    v1 = jnp.max(logits, axis=0, keepdims=True)
    i1 = jnp.min(jnp.where(logits == v1, eidx, float(N_EXPERTS)), axis=0, keepdims=True)
    m1 = eidx == i1
    rest = jnp.where(m1, neg, logits)
    v2 = jnp.max(rest, axis=0, keepdims=True)
    i2 = jnp.min(jnp.where(rest == v2, eidx, float(N_EXPERTS)), axis=0, keepdims=True)
    m2 = eidx == i2
    e2 = jnp.exp(v2 - v1)
    den = 1.0 + e2
    gate_ref[0] = jnp.where(m1, 1.0 / den, 0.0) + jnp.where(m2, e2 / den, 0.0)
    sel = jnp.logical_or(m1, m2)
    self = jnp.where(sel, 1.0, 0.0)
    rank = jnp.dot(self.astype(BF16), tri_ref[...], preferred_element_type=F32)
    pos_ref[0] = jnp.where(sel, rank, -1.0)
    cnt = jnp.sum(self, axis=1, keepdims=True)
    cnt_ref[0, 0] = jnp.broadcast_to(cnt, (N_EXPERTS, LANES)).astype(jnp.int32)


def _expert_kernel(cnt_ref, x_ref, h_ref, gate_ref, pos_ref, bgt_ref, cgt_ref, w1_ref, w3_ref, w2_ref,
                   o_ref, xg_ref, gs_ref, og_ref, *, tm, n_lat, blk):
    bi = pl.program_id(0)
    ti = pl.program_id(1)
    e = pl.program_id(2)
    f = pl.program_id(3)
    nt = pl.num_programs(1)
    ne = pl.num_programs(2)
    nf = pl.num_programs(3)
    cnt = cnt_ref[(bi * nt + ti) * ne + e]
    nblk = (cnt + blk - 1) // blk
    pos_row = pos_ref[0, pl.ds(e, 1), :]
    gate_row = gate_ref[0, pl.ds(e, 1), :]
    slot = lax.broadcasted_iota(jnp.int32, (blk, tm), 0).astype(F32)

    def one_hot(j):
        return pos_row == slot + (j * blk).astype(F32)

    def rows_of(j):
        return pl.ds(pl.multiple_of(j * blk, blk), blk)

    @pl.when(jnp.logical_and(e == 0, f == 0))
    def _zero():
        o_ref[0] = jnp.zeros(o_ref.shape[1:], F32)

    @pl.when(f == 0)
    def _gather():
        def body(j, carry):
            hit = one_hot(j)
            g = jnp.where(hit, 1.0, 0.0).astype(BF16)
            xg_ref[rows_of(j), :] = jnp.dot(g, h_ref[0], preferred_element_type=F32).astype(BF16)
            gsl = jnp.sum(jnp.where(hit, gate_row, 0.0), axis=1, keepdims=True)
            gs_ref[rows_of(j), :] = jnp.broadcast_to(gsl, (blk, LANES))
            og_ref[rows_of(j), :] = jnp.zeros((blk, og_ref.shape[1]), F32)
            return carry
        lax.fori_loop(0, nblk, body, 0)

    def expert(j, carry):
        xb = xg_ref[rows_of(j), :]
        a = jnp.dot(xb, w1_ref[0], preferred_element_type=F32)
        bb = jnp.dot(xb, w3_ref[0], preferred_element_type=F32)
        act = (a * jax.nn.sigmoid(a) * bb * gs_ref[rows_of(j), 0:1]).astype(BF16)
        og_ref[rows_of(j), :] += jnp.dot(act, w2_ref[0], preferred_element_type=F32)
        return carry
    lax.fori_loop(0, nblk, expert, 0)

    @pl.when(f == nf - 1)
    def _scatter():
        d = o_ref.shape[2]
        def body(j, carry):
            g = jnp.where(one_hot(j), 1.0, 0.0).astype(BF16)
            for c0 in range(0, d, d // 2):
                o = og_ref[rows_of(j), c0:c0 + d // 2].astype(BF16)
                o_ref[0, :, c0:c0 + d // 2] += lax.dot_general(g, o, _TN2, preferred_element_type=F32)
            return carry
        lax.fori_loop(0, nblk, body, 0)

    @pl.when(jnp.logical_and(e == ne - 1, f == nf - 1))
    def _finish():
        is_ctx = _is_ctx_rows(ti, tm, n_lat)
        gt = jnp.where(is_ctx, cgt_ref[...], bgt_ref[0])
        o_ref[0] = x_ref[0] + gt * o_ref[0]


def _moe_ffn(xa, g2, modb, modc, router_w, w1, w3, w2, n_lat, tm, nf):
    b, ta, d = xa.shape
    ne, _, ff = w1.shape
    tf = ff // nf
    nt = ta // tm
    tok = lambda dt: pl.BlockSpec((1, tm, d), lambda bb, i, *_: (bb, i, 0))
    emaj = pl.BlockSpec((1, ne, tm), lambda bb, i, *_: (bb, 0, i))
    tri = jnp.triu(jnp.ones((tm, tm), BF16), 1)
    h, gate, pos, cnt = pl.pallas_call(
        functools.partial(_route_kernel, tm=tm, n_lat=n_lat),
        grid=(b, nt),
        in_specs=[tok(F32), pl.BlockSpec((1, d), lambda bb, i: (0, 0))] + _mod_specs(d, 3, 4)
        + [pl.BlockSpec((ne, d), lambda bb, i: (0, 0)), pl.BlockSpec((tm, tm), lambda bb, i: (0, 0))],
        out_specs=[tok(BF16), emaj, emaj, pl.BlockSpec((1, 1, ne, LANES), lambda bb, i: (bb, i, 0, 0))],
        out_shape=[jax.ShapeDtypeStruct((b, ta, d), BF16), jax.ShapeDtypeStruct((b, ne, ta), F32),
                   jax.ShapeDtypeStruct((b, ne, ta), F32), jax.ShapeDtypeStruct((b, nt, ne, LANES), jnp.int32)],
        compiler_params=_cparams(("parallel", "parallel")),
        name="norm2_moe_route",
    )(xa, g2, modb, modb, modc, modc, router_w.T, tri)
    counts = cnt[:, :, :, 0].reshape(-1)
    once = dict(pipeline_mode=pl.Buffered(1))
    grid_spec = pltpu.PrefetchScalarGridSpec(
        num_scalar_prefetch=1,
        grid=(b, nt, ne, nf),
        in_specs=[pl.BlockSpec((1, tm, d), lambda bb, i, e, f, c: (bb, i, 0), **once),
                  pl.BlockSpec((1, tm, d), lambda bb, i, e, f, c: (bb, i, 0), **once),
                  emaj, emaj,
                  pl.BlockSpec((1, 1, d), lambda bb, i, e, f, c: (bb, 0, 5)),
                  pl.BlockSpec((1, d), lambda bb, i, e, f, c: (0, 5)),
                  pl.BlockSpec((1, d, tf), lambda bb, i, e, f, c: (e, 0, f)),
                  pl.BlockSpec((1, d, tf), lambda bb, i, e, f, c: (e, 0, f)),
                  pl.BlockSpec((1, tf, d), lambda bb, i, e, f, c: (e, f, 0))],
        out_specs=pl.BlockSpec((1, tm, d), lambda bb, i, e, f, c: (bb, i, 0)),
        scratch_shapes=[pltpu.VMEM((tm, d), BF16), pltpu.VMEM((tm, LANES), F32), pltpu.VMEM((tm, d), F32)])
    return pl.pallas_call(
        functools.partial(_expert_kernel, tm=tm, n_lat=n_lat, blk=MOE_SLOT_BLOCK),
        grid_spec=grid_spec,
        out_shape=jax.ShapeDtypeStruct((b, ta, d), F32),
        compiler_params=_cparams(("parallel", "parallel", "arbitrary", "arbitrary")),
        name="moe_experts",
    )(counts, xa, h, gate, pos, modb, modc, w1, w3, w2)


def _final_kernel(x_ref, g_ref, o_ref):
    x = x_ref[0]
    ms = jnp.mean(x * x, axis=-1, keepdims=True)
    o_ref[0] = x * lax.rsqrt(ms + NORM_EPS) * g_ref[...]


def _final_norm(xa, g, seq, tm):
    b, _, d = xa.shape
    return pl.pallas_call(
        _final_kernel,
        grid=(b, seq // tm),
        in_specs=[pl.BlockSpec((1, tm, d), lambda bb, i: (bb, i, 0)),
                  pl.BlockSpec((1, d), lambda bb, i: (0, 0))],
        out_specs=pl.BlockSpec((1, tm, d), lambda bb, i: (bb, i, 0)),
        out_shape=jax.ShapeDtypeStruct((b, seq, d), F32),
        compiler_params=_cparams(("parallel", "parallel")),
        name="final_norm",
    )(xa, g)


def _rope_tables(seq, ctx_len):
    axis_dim = ATTN_QK_DIM // 2
    inv_freq = ROPE_THETA ** (-jnp.arange(0, axis_dim, 2, dtype=F32) / axis_dim)
    t = jnp.arange(seq)
    pos = jnp.stack([(t // GRID_W).astype(F32), (t % GRID_W).astype(F32)], axis=1)
    lane = np.arange(LANES)
    axis = (lane % ATTN_QK_DIM) // axis_dim
    freq = lane % (axis_dim // 2)
    sign = np.where((lane % axis_dim) < axis_dim // 2, -1.0, 1.0).astype(np.float32)
    ang = pos[:, axis] * inv_freq[freq][None, :]
    cos_t = jnp.concatenate([jnp.cos(ang), jnp.ones((ctx_len, LANES), F32)], axis=0)
    sin_t = jnp.concatenate([jnp.sin(ang) * sign[None, :], jnp.zeros((ctx_len, LANES), F32)], axis=0)
    return cos_t, sin_t


def _pick_tile(n, candidates):
    for c in candidates:
        if n % c == 0:
            return c
    raise ValueError(f"no tile in {candidates} divides {n}")


def kernel(x, c, ctx, c_ctx, ada_w, ada_b, norm1_g, norm2_g, w_in, w_out, diff_lambda, subln_g, rwkv_mu, rwkv_w0, rwkv_w2, rwkv_a0, rwkv_a2, rwkv_v0, rwkv_v1, rwkv_v2, rwkv_g2, rwkv_k_k, rwkv_k_a, rwkv_r_k, lnx_w, lnx_b, ffn_w1, ffn_w3, ffn_w2, router_w, exp_w1, exp_w3, exp_w2, final_g):
    b, seq, d = x.shape
    n_lat = seq
    ctx_len = ctx.shape[1]
    depth = ada_w.shape[0]
    ta = n_lat + ctx_len
    assert b + 1 <= 8 and seq % GRID_W == 0 and ctx_len % CHUNK == 0 and seq % CHUNK == 0

    tm = _pick_tile(math.gcd(ctx_len, seq), (256, 128))
    tq = _pick_tile(seq, (512, 256, 128))
    tk = _pick_tile(ta, (3328, 1280, 640, 256, 128))
    tm_moe = _pick_tile(ta, (1280, 640, 256, 128))

    xa = jnp.concatenate([x, ctx], axis=1)
    cvec = jnp.zeros((8, d), F32).at[:b].set(c).at[b].set(c_ctx)
    mod = _modulation(cvec, ada_w, ada_b)
    cos_t, sin_t = _rope_tables(seq, ctx_len)
    seg = jnp.asarray(np.kron(np.eye(RWKV_HEADS), np.ones((RWKV_HEAD, RWKV_HEAD))), BF16)
    n_in = w_in.shape[2]
    n_in_p = 3 * ATTN_WIDTH + 3 * RWKV_WIDTH + LORA_PAD

    v_first = None
    for l in range(depth):
        lam_init = 0.8 - 0.6 * math.exp(-0.3 * l)
        modb = mod[l, :b].reshape(b, 1, 6 * d)
        modc = mod[l, b:b + 1]
        w_in_p = jnp.pad(w_in[l], ((0, 0), (0, n_in_p - n_in))).astype(BF16)
        q, k, v, pr, pk, pv, lora = _in_proj(xa, norm1_g[l].reshape(1, d), modb, modc, w_in_p,
                                             cos_t, sin_t, n_lat, tm)
        sg = subln_g[l].reshape(ATTN_V_DIM, 1)
        att_lat = _attention(q, k, v, diff_lambda[l], sg, lam_init, 0, seq, 0, ta, tq, tk)
        att_ctx = _attention(q, k, v, diff_lambda[l], sg, lam_init, seq, ctx_len, seq, ctx_len,
                             ctx_len, ctx_len)
        att = jnp.concatenate([att_lat, att_ctx], axis=2)
        vres = None if l == 0 else (v_first, rwkv_v0[l - 1], rwkv_v1[l - 1], rwkv_v2[l - 1])
        r, vv, kk, g, lw_f, lw_b, bt_f, bt_b, kd_f, kd_b = _rwkv_prepare(
            pr, pk, pv, lora, rwkv_mu[l], rwkv_w0[l], rwkv_w2[l], rwkv_a0[l], rwkv_a2[l], rwkv_g2[l],
            rwkv_k_k[l], rwkv_k_a[l], seg, vres, n_lat, tm)
        if l == 0:
            v_first = vv
        y_f, y_b = _rwkv_scan(r, vv, kk, lw_f, lw_b, bt_f, bt_b, kd_f, kd_b, n_lat)
        xa = _post_mix(xa, att, y_f, y_b, r, vv, kd_f, kd_b, g, modb, modc, w_out[l].astype(BF16),
                       lnx_w[l], lnx_b[l], rwkv_r_k[l], seg, n_lat, tm)
        i = l // 2
        if l % 2 == 0:
            xa = _dense_ffn(xa, norm2_g[l].reshape(1, d), modb, modc, ffn_w1[i].astype(BF16),
                            ffn_w3[i].astype(BF16), ffn_w2[i].astype(BF16), n_lat, tm)
        else:
            xa = _moe_ffn(xa, norm2_g[l].reshape(1, d), modb, modc, router_w[i],
                          exp_w1[i].astype(BF16), exp_w3[i].astype(BF16), exp_w2[i].astype(BF16),
                          n_lat, tm_moe, 4)
    return _final_norm(xa, final_g.reshape(1, d), seq, tm)
```

```python
import functools
import math

import numpy as np
import jax
import jax.numpy as jnp
from jax import lax
from jax.experimental import pallas as pl
from jax.experimental.pallas import tpu as pltpu

F32 = jnp.float32
BF16 = jnp.bfloat16

ATTN_HEADS = 4
ATTN_QK_DIM = 64
ATTN_V_DIM = 128
ATTN_WIDTH = ATTN_HEADS * ATTN_V_DIM
RWKV_HEADS = 8
RWKV_HEAD = 64
RWKV_WIDTH = RWKV_HEADS * RWKV_HEAD
DECAY_LORA = 32
AAA_LORA = 32
GATE_LORA = 96
LORA_PAD = 256
N_EXPERTS = 8
GRID_W = 64
ROPE_THETA = 10000.0
NORM_EPS = 1e-6
SUBLN_EPS = 1e-5
LNX_EPS = 64e-5
LANES = 128
CHUNK = 64
INV_BLOCK = 16
MOE_SLOT_BLOCK = 128
SCAN_CHUNKS = 2
QK_LOOKAHEAD = 2
V_ONES = 16
V_EXT = ATTN_V_DIM + V_ONES
VMEM_LIMIT = 56 * 1024 * 1024
Q_SCALE = ATTN_QK_DIM ** -0.5 * math.log2(math.e)


def _cparams(sem, vmem=VMEM_LIMIT, flags=None):
    return pltpu.CompilerParams(dimension_semantics=sem, vmem_limit_bytes=vmem, flags=flags)


def _bdot(a, b):
    return jnp.dot(a.astype(BF16), b.astype(BF16), preferred_element_type=F32)


def _bdot_nt(a, b):
    return lax.dot_general(a.astype(BF16), b.astype(BF16), (((1,), (1,)), ((), ())),
                           preferred_element_type=F32)


def _bdot_tn(a, b):
    return lax.dot_general(a.astype(BF16), b.astype(BF16), (((0,), (0,)), ((), ())),
                           preferred_element_type=F32)


def _split(a):
    hi = a.astype(BF16)
    lo = (a - hi.astype(F32)).astype(BF16)
    return hi, lo


def _dot_exact_rhs(a, b_bf16):
    hi, lo = _split(a)
    return (jnp.dot(hi, b_bf16, preferred_element_type=F32)
            + jnp.dot(lo, b_bf16, preferred_element_type=F32))


def _dot_exact_lhs(a_bf16, b):
    hi, lo = _split(b)
    return (jnp.dot(a_bf16, hi, preferred_element_type=F32)
            + jnp.dot(a_bf16, lo, preferred_element_type=F32))


def _modnorm(x, g, sc_b, sh_b, sc_c, sh_c, is_ctx):
    ms = jnp.mean(x * x, axis=-1, keepdims=True)
    y = x * lax.rsqrt(ms + NORM_EPS) * g
    sc = jnp.where(is_ctx, sc_c, sc_b)
    sh = jnp.where(is_ctx, sh_c, sh_b)
    return y * (1.0 + sc) + sh


def _is_ctx_rows(tile_idx, tm, n_lat):
    row = tile_idx * tm + lax.broadcasted_iota(jnp.int32, (tm, 1), 0)
    return row >= n_lat


def _mod_specs(d, k_sh, k_sc):
    return [
        pl.BlockSpec((1, 1, d), lambda b, i, *_: (b, 0, k_sh)),
        pl.BlockSpec((1, 1, d), lambda b, i, *_: (b, 0, k_sc)),
        pl.BlockSpec((1, d), lambda b, i, *_: (0, k_sh)),
        pl.BlockSpec((1, d), lambda b, i, *_: (0, k_sc)),
    ]


def _mod_kernel(c_ref, w_ref, b_ref, o_ref):
    c = c_ref[...]
    s = c * jax.nn.sigmoid(c)
    o_ref[0] = _bdot(s, w_ref[0]) + b_ref[0]


def _modulation(cvec, ada_w, ada_b):
    depth, d, n = ada_w.shape
    tn = 2048
    return pl.pallas_call(
        _mod_kernel,
        grid=(depth, n // tn),
        in_specs=[pl.BlockSpec((8, d), lambda l, j: (0, 0)),
                  pl.BlockSpec((1, d, tn), lambda l, j: (l, 0, j)),
                  pl.BlockSpec((1, 1, tn), lambda l, j: (l, 0, j))],
        out_specs=pl.BlockSpec((1, 8, tn), lambda l, j: (l, 0, j)),
        out_shape=jax.ShapeDtypeStruct((depth, 8, n), F32),
        compiler_params=_cparams(("parallel", "parallel")),
        name="adaln_mod",
    )(cvec, ada_w, ada_b.reshape(depth, 1, n))


def _inproj_kernel(x_ref, g_ref, bsh_ref, bsc_ref, csh_ref, csc_ref, w_ref, cos_ref, sin_ref,
                   q_ref, k_ref, v_ref, rr_ref, rk_ref, rv_ref, lo_ref, *, tm, n_lat):
    i = pl.program_id(1)
    is_ctx = _is_ctx_rows(i, tm, n_lat)
    h = _modnorm(x_ref[0], g_ref[...], bsc_ref[0], bsh_ref[0], csc_ref[...], csh_ref[...], is_ctx)
    p = jnp.dot(h.astype(BF16), w_ref[...], preferred_element_type=F32)
    cos_t = cos_ref[...]
    sin_t = sin_ref[...]
    lane = lax.broadcasted_iota(jnp.int32, (tm, LANES), 1)
    first = (lane & 31) < 16

    def rope(z):
        partner = jnp.where(first, pltpu.roll(z, LANES - 16, 1), pltpu.roll(z, 16, 1))
        return z * cos_t + partner * sin_t

    for hh in range(ATTN_HEADS):
        lo = LANES * hh
        q_ref[0, lo:lo + LANES, :] = jnp.transpose(rope(p[:, lo:lo + LANES]) * Q_SCALE).astype(BF16)
        k_ref[0, :, lo:lo + LANES] = rope(p[:, ATTN_WIDTH + lo:ATTN_WIDTH + lo + LANES]).astype(BF16)
        vo = 2 * ATTN_WIDTH + lo
        ve = V_EXT * hh
        v_ref[0, ve:ve + ATTN_V_DIM, :] = jnp.transpose(p[:, vo:vo + LANES]).astype(BF16)
        v_ref[0, ve + ATTN_V_DIM:ve + V_EXT, :] = jnp.ones((V_ONES, tm), BF16)
    o = 3 * ATTN_WIDTH
    rr_ref[0] = p[:, o:o + RWKV_WIDTH]
    rk_ref[0] = p[:, o + RWKV_WIDTH:o + 2 * RWKV_WIDTH]
    rv_ref[0] = p[:, o + 2 * RWKV_WIDTH:o + 3 * RWKV_WIDTH]
    o += 3 * RWKV_WIDTH
    lo_ref[0] = p[:, o:o + LORA_PAD]


def _in_proj(xa, g1, modb, modc, w_in_p, cos_t, sin_t, n_lat, tm):
    b, ta, d = xa.shape
    n = w_in_p.shape[1]
    tok = lambda w: pl.BlockSpec((1, tm, w), lambda bb, i: (bb, i, 0))
    feat = pl.BlockSpec((1, ATTN_WIDTH, tm), lambda bb, i: (bb, 0, i))
    feat_shape = jax.ShapeDtypeStruct((b, ATTN_WIDTH, ta), BF16)
    vext = pl.BlockSpec((1, ATTN_HEADS * V_EXT, tm), lambda bb, i: (bb, 0, i))
    vext_shape = jax.ShapeDtypeStruct((b, ATTN_HEADS * V_EXT, ta), BF16)
    outs = [feat_shape, jax.ShapeDtypeStruct((b, ta, ATTN_WIDTH), BF16), vext_shape] \
        + [jax.ShapeDtypeStruct((b, ta, RWKV_WIDTH), F32)] * 3 \
        + [jax.ShapeDtypeStruct((b, ta, LORA_PAD), F32)]
    return pl.pallas_call(
        functools.partial(_inproj_kernel, tm=tm, n_lat=n_lat),
        grid=(b, ta // tm),
        in_specs=[tok(d), pl.BlockSpec((1, d), lambda bb, i: (0, 0))] + _mod_specs(d, 0, 1)
        + [pl.BlockSpec((d, n), lambda bb, i: (0, 0)),
           pl.BlockSpec((tm, LANES), lambda bb, i: (i, 0)),
           pl.BlockSpec((tm, LANES), lambda bb, i: (i, 0))],
        out_specs=[feat, tok(ATTN_WIDTH), vext] + [tok(RWKV_WIDTH)] * 3 + [tok(LORA_PAD)],
        out_shape=outs,
        compiler_params=_cparams(("parallel", "parallel")),
        name="norm1_in_proj",
    )(xa, g1, modb, modb, modc, modc, w_in_p, cos_t, sin_t)


def _attn_kernel(lam_ref, sg_ref, q_ref, k_ref, v_ref, o_ref, s_ref,
                 *, tq, k0, nkeys, tkk, ts, lam_init):
    q = q_ref[0]
    feat = lax.broadcasted_iota(jnp.int32, (LANES, tq), 0)
    zero = jnp.zeros_like(q)
    qs = jnp.concatenate([jnp.where(feat < ATTN_QK_DIM, q, zero),
                          jnp.where(feat >= ATTN_QK_DIM, q, zero)], axis=1)
    n_tiles = nkeys // tkk
    n_sub = tkk // ts

    def scores(t):
        mx = None
        for j in range(n_sub):
            r0 = k0 + t * tkk + j * ts
            s_j = jnp.dot(k_ref[0, r0:r0 + ts, :], qs, preferred_element_type=F32)
            s_ref[t % 2, j * ts:(j + 1) * ts, :] = s_j
            cm = jnp.max(s_j, axis=0, keepdims=True)
            mx = cm if mx is None else jnp.maximum(mx, cm)
        return mx

    m_run = jnp.full((1, 2 * tq), -jnp.inf, F32)
    acc = jnp.zeros((V_EXT, 2 * tq), F32)
    mx_next = scores(0)
    for t in range(n_tiles):
        mx_cur = mx_next
        if t + 1 < n_tiles:
            mx_next = scores(t + 1)
        m_new = jnp.maximum(m_run, mx_cur)
        alpha = jnp.exp2(m_run - m_new)
        pv = None
        for j in range(n_sub):
            p_j = jnp.exp2(s_ref[t % 2, j * ts:(j + 1) * ts, :] - m_new).astype(BF16)
            c0 = k0 + t * tkk + j * ts
            d = jnp.dot(v_ref[0, :, c0:c0 + ts], p_j, preferred_element_type=F32)
            pv = d if pv is None else pv + d
        acc = alpha * acc + pv
        m_run = m_new

    lp = lam_ref[...]
    lam = (jnp.exp(jnp.sum(lp[0:1] * lp[1:2], axis=1, keepdims=True))
           - jnp.exp(jnp.sum(lp[2:3] * lp[3:4], axis=1, keepdims=True)) + lam_init)
    l = acc[ATTN_V_DIM:ATTN_V_DIM + 1, :]
    o1 = acc[0:ATTN_V_DIM, 0:tq] / l[:, 0:tq]
    o2 = acc[0:ATTN_V_DIM, tq:2 * tq] / l[:, tq:2 * tq]
    o = o1 - lam * o2
    ms = jnp.mean(o * o, axis=0, keepdims=True)
    y = o * lax.rsqrt(ms + SUBLN_EPS) * sg_ref[...]
    o_ref[0] = (y * (1.0 - lam_init)).astype(BF16)


def _attention(q_t, k, v_t, lam_p, subln_g, lam_init, q0, nq, k0, nkeys, tq, tkk):
    b, ta, _ = k.shape
    qo = q0 // tq
    return pl.pallas_call(
        functools.partial(_attn_kernel, tq=tq, k0=k0, nkeys=nkeys, tkk=tkk, ts=_pick_tile(tkk, (256, LANES)),
                          lam_init=lam_init),
        grid=(b, ATTN_HEADS, nq // tq),
        in_specs=[pl.BlockSpec((4, ATTN_QK_DIM), lambda bb, h, i: (0, 0)),
                  pl.BlockSpec((ATTN_V_DIM, 1), lambda bb, h, i: (0, 0)),
                  pl.BlockSpec((1, LANES, tq), lambda bb, h, i: (bb, h, i + qo)),
                  pl.BlockSpec((1, ta, LANES), lambda bb, h, i: (bb, 0, h)),
                  pl.BlockSpec((1, V_EXT, ta), lambda bb, h, i: (bb, h, 0))],
        out_specs=pl.BlockSpec((1, ATTN_V_DIM, tq), lambda bb, h, i: (bb, h, i)),
        out_shape=jax.ShapeDtypeStruct((b, ATTN_WIDTH, nq), BF16),
        scratch_shapes=[pltpu.VMEM((2, tkk, 2 * tq), F32)],
        compiler_params=_cparams(("parallel", "parallel", "parallel")),
        name="diff_attention",
    )(lam_p, subln_g, q_t, k, v_t)


def _prep_kernel(*refs, tt, n_lat, ta, has_vres):
    (pr_ref, pk_ref, pv_ref, hpr_ref, hpk_ref, hpv_ref, hnr_ref, hnk_ref, hnv_ref, lo_ref,
     mu_ref, w0_ref, w2_ref, a0_ref, a2_ref, g2_ref, kk_ref, ka_ref, seg_ref) = refs[:19]
    n_in = 19
    if has_vres:
        vf_ref, v0_ref, v1_ref, v2_ref = refs[19:23]
        n_in = 23
    (r_out, v_out, kk_out, g_out, lwf_out, lwb_out, bf_out, bb_out, kdf_out, kdb_out) = refs[n_in:]

    i = pl.program_id(1)
    row = lax.broadcasted_iota(jnp.int32, (tt, 1), 0)
    gidx = i * tt + row
    no_prev = jnp.logical_or(gidx == 0, gidx == n_lat)
    no_next = jnp.logical_or(gidx == n_lat - 1, gidx == ta - 1)

    def shift_mix(p_ref, hp_ref, hn_ref, j):
        p = p_ref[0]
        pp = jnp.where(row == 0, hp_ref[0, 7:8, :], pltpu.roll(p, 1, 0))
        pp = jnp.where(no_prev, 0.0, pp)
        pn = jnp.where(row == tt - 1, hn_ref[0, 0:1, :], pltpu.roll(p, tt - 1, 0))
        pn = jnp.where(no_next, 0.0, pn)
        return p + mu_ref[2 * j:2 * j + 1, :] * (pp - p) + mu_ref[2 * j + 1:2 * j + 2, :] * (pn - p)

    r = shift_mix(pr_ref, hpr_ref, hnr_ref, 0)
    k = shift_mix(pk_ref, hpk_ref, hnk_ref, 1)
    v = shift_mix(pv_ref, hpv_ref, hnv_ref, 2)
    if has_vres:
        mix = jax.nn.sigmoid(v0_ref[...] + _bdot(_bdot(v, v1_ref[...]), v2_ref[...]))
        v = v + (vf_ref[0] - v) * mix
    lo = lo_ref[0]
    gd = lo[:, 2 * DECAY_LORA + 2 * AAA_LORA:2 * DECAY_LORA + 2 * AAA_LORA + GATE_LORA]
    g_out[0] = _bdot(jax.nn.sigmoid(gd), g2_ref[...])
    kk = k * kk_ref[...]
    ss = _dot_exact_rhs(kk * kk, seg_ref[...])
    kk = kk * lax.rsqrt(jnp.maximum(ss, 1e-24))
    r_out[0] = r
    v_out[0] = v
    kk_out[0] = kk
    lw_outs = (lwf_out, lwb_out)
    b_outs = (bf_out, bb_out)
    kd_outs = (kdf_out, kdb_out)
    for d in range(2):
        wd = lo[:, d * DECAY_LORA:(d + 1) * DECAY_LORA]
        wl = w0_ref[d:d + 1, :] + _bdot(jnp.tanh(wd), w2_ref[d])
        lw_outs[d][0] = -jax.nn.sigmoid(wl) * math.exp(-0.5)
        ad = lo[:, 2 * DECAY_LORA + d * AAA_LORA:2 * DECAY_LORA + (d + 1) * AAA_LORA]
        a = jax.nn.sigmoid(a0_ref[d:d + 1, :] + _bdot(ad, a2_ref[d]))
        b_outs[d][0] = a * kk
        kd_outs[d][0] = k * (1.0 + (a - 1.0) * ka_ref[...])


def _rwkv_prepare(pr, pk, pv, lora, mu, w0, w2, a0, a2, g2, k_k, k_a, seg, vres, n_lat, tt):
    b, ta, w = pr.shape
    nb8 = ta // 8
    tok = pl.BlockSpec((1, tt, w), lambda bb, i: (bb, i, 0))
    prev = pl.BlockSpec((1, 8, w), lambda bb, i: (bb, jnp.maximum(i * (tt // 8) - 1, 0), 0))
    nxt = pl.BlockSpec((1, 8, w), lambda bb, i: (bb, jnp.minimum((i + 1) * (tt // 8), nb8 - 1), 0))
    full = lambda a: pl.BlockSpec(a.shape, lambda bb, i, _n=a.ndim: (0,) * _n)
    consts = [mu.reshape(6, w), w0, w2, a0, a2, g2, k_k.reshape(1, w), k_a.reshape(1, w), seg]
    ins = [pr, pk, pv, pr, pk, pv, pr, pk, pv, lora] + consts
    specs = [tok] * 3 + [prev] * 3 + [nxt] * 3 \
        + [pl.BlockSpec((1, tt, LORA_PAD), lambda bb, i: (bb, i, 0))] + [full(a) for a in consts]
    if vres is not None:
        v_first, v0, v1, v2 = vres
        extra = [v0.reshape(1, w), v1, v2]
        ins += [v_first] + extra
        specs += [tok] + [full(a) for a in extra]
    return pl.pallas_call(
        functools.partial(_prep_kernel, tt=tt, n_lat=n_lat, ta=ta, has_vres=vres is not None),
        grid=(b, ta // tt),
        in_specs=specs,
        out_specs=[tok] * 10,
        out_shape=[jax.ShapeDtypeStruct((b, ta, w), F32)] * 10,
        compiler_params=_cparams(("parallel", "parallel")),
        name="rwkv_prepare",
    )(*ins)


_NN = (((2,), (1,)), ((0,), (0,)))
_NT = (((2,), (2,)), ((0,), (0,)))
_TN = (((1,), (1,)), ((0,), (0,)))


def _bmm(a, b, dims=_NN):
    return lax.dot_general(a.astype(BF16), b.astype(BF16), dims, preferred_element_type=F32)


def _scan_kernel(rf_ref, vf_ref, kkf_ref, lwf_ref, btf_ref, kdf_ref,
                 rb_ref, vb_ref, kkb_ref, lwb_ref, btb_ref, kdb_ref,
                 yf_ref, yb_ref, st_ref, *, nsub):
    s = pl.program_id(1)
    c = CHUNK
    n = RWKV_HEAD
    g = 2 * RWKV_HEADS

    @pl.when(s == 0)
    def _init():
        st_ref[...] = jnp.zeros(st_ref.shape, F32)

    t_i = lax.broadcasted_iota(jnp.int32, (c, c), 0)
    j_i = lax.broadcasted_iota(jnp.int32, (c, c), 1)
    t2 = lax.broadcasted_iota(jnp.int32, (2 * c, 2 * c), 0)
    j2 = lax.broadcasted_iota(jnp.int32, (2 * c, 2 * c), 1)
    tt2 = t2 & (c - 1)
    jj2 = j2 & (c - 1)
    eye = jnp.where(t_i == j_i, 1.0, 0.0)
    blk_bits = int(math.log2(INV_BLOCK))
    same_blk = (t_i >> blk_bits) == (j_i >> blk_bits)
    heads = lambda z: [z[:, h * n:(h + 1) * n] for h in range(RWKV_HEADS)]
    dirs = ((rf_ref, vf_ref, kkf_ref, lwf_ref, btf_ref, kdf_ref, False),
            (rb_ref, vb_ref, kkb_ref, lwb_ref, btb_ref, kdb_ref, True))
    at_l, rt_l, v_l, bk_l, pt_l, m_l = [], [], [], [], [], []
    for t in range(nsub):
        for r_ref, v_ref, kk_ref, lw_ref, bt_ref, kd_ref, rev in dirs:
            ci = nsub - 1 - t if rev else t
            rows = slice(ci * c, (ci + 1) * c)
            incl = (j_i >= t_i) if rev else (j_i <= t_i)
            tri = jnp.where(incl, 1.0, 0.0).astype(BF16)
            before = (jj2 > tt2) if rev else (jj2 < tt2)
            mask_m = jnp.logical_or(before, jnp.logical_and(t2 >= c, jj2 == tt2))
            lw = lw_ref[0, rows, :]
            cs = _dot_exact_lhs(tri, lw)
            tot = cs[0:1, :] if rev else cs[c - 1:c, :]
            p_inv = jnp.exp(-cs)
            p_rest = jnp.exp(tot - cs)
            bt = bt_ref[0, rows, :]
            kd = kd_ref[0, rows, :]
            rt = heads(r_ref[0, rows, :] * jnp.exp(cs))
            at = heads(-kk_ref[0, rows, :] * jnp.exp(cs - lw))
            bti = heads(bt * p_inv)
            kti = heads(kd * p_inv)
            bp = heads(bt * p_rest)
            kp = heads(kd * p_rest)
            x1 = jnp.stack([jnp.concatenate([a, b], axis=0) for a, b in zip(at, rt)])
            x2 = jnp.stack([jnp.concatenate([a, b], axis=0) for a, b in zip(bti, kti)])
            m_l.append(jnp.where(mask_m[None], _bmm(x1, x2, _NT), 0.0))
            at_l += at
            rt_l += rt
            v_l += heads(v_ref[0, rows, :])
            bk_l += [jnp.concatenate([a, b], axis=0) for a, b in zip(bp, kp)]
            pt_l += heads(jnp.exp(tot))
    m = jnp.concatenate(m_l, axis=0)
    at_s = jnp.stack(at_l)
    rt_s = jnp.stack(rt_l)
    v_s = jnp.stack(v_l)
    bk_s = jnp.stack(bk_l)
    pt_s = jnp.stack(pt_l)

    a_full = m[:, 0:c, 0:c]
    d_pow = jnp.where(same_blk[None], a_full, 0.0)
    l_off = a_full - d_pow
    t_diag = eye[None] + d_pow
    for _ in range(blk_bits - 1):
        d_pow = _bmm(d_pow, d_pow)
        t_diag = t_diag + _bmm(d_pow, t_diag)
    n_mat = _bmm(t_diag, l_off)
    x_mat = t_diag + _bmm(n_mat, t_diag)
    tinv = x_mat + _bmm(_bmm(n_mat, n_mat), x_mat)

    akv = _bmm(m[:, 0:c, c:2 * c], v_s)

    st = st_ref[...]
    for t in range(nsub):
        sl = slice(t * g, (t + 1) * g)
        u = _bmm(tinv[sl], akv[sl] + _bmm(at_s[sl], st, _NT))
        uv = jnp.concatenate([u, v_s[sl]], axis=1)
        y = _bmm(m[sl, c:2 * c, :], uv) + _bmm(rt_s[sl], st, _NT)
        st = pt_s[sl] * st + _bmm(uv, bk_s[sl], _TN)
        rf = slice(t * c, (t + 1) * c)
        rb = slice((nsub - 1 - t) * c, (nsub - t) * c)
        for h in range(RWKV_HEADS):
            yf_ref[0, rf, h * n:(h + 1) * n] = y[h]
            yb_ref[0, rb, h * n:(h + 1) * n] = y[RWKV_HEADS + h]
    st_ref[...] = st


def _rwkv_scan(r, v, kk, lw_f, lw_b, bt_f, bt_b, kd_f, kd_b, n_lat):
    b, ta, w = r.shape
    nsub = SCAN_CHUNKS
    rows = nsub * CHUNK
    assert n_lat % rows == 0 and ta % rows == 0
    nc = ta // rows
    nlat = n_lat // rows
    nctx = nc - nlat

    def fwd_idx(s):
        return jnp.where(s < nctx, nlat + s, s - nctx)

    def rev_idx(s):
        return nc - 1 - s

    fwd = pl.BlockSpec((1, rows, w), lambda bb, s: (bb, fwd_idx(s), 0))
    bwd = pl.BlockSpec((1, rows, w), lambda bb, s: (bb, rev_idx(s), 0))
    return pl.pallas_call(
        functools.partial(_scan_kernel, nsub=nsub),
        grid=(b, nc),
        in_specs=[fwd] * 6 + [bwd] * 6,
        out_specs=[fwd, bwd],
        out_shape=[jax.ShapeDtypeStruct((b, ta, w), F32)] * 2,
        scratch_shapes=[pltpu.VMEM((2 * RWKV_HEADS, RWKV_HEAD, RWKV_HEAD), F32)],
        compiler_params=_cparams(("parallel", "arbitrary")),
        name="rwkv_scan",
    )(r, v, kk, lw_f, bt_f, kd_f, r, v, kk, lw_b, bt_b, kd_b)


def _postmix_kernel(x_ref, att_ref, yf_ref, yb_ref, r_ref, v_ref, kdf_ref, kdb_ref, g_ref,
                    bgt_ref, cgt_ref, wo_ref, lnw_ref, lnb_ref, rk_ref, seg_ref, o_ref, *, tm, n_lat):
    i = pl.program_id(1)
    is_ctx = _is_ctx_rows(i, tm, n_lat)
    seg = seg_ref[...]
    y = yf_ref[0] + yb_ref[0]
    inv_n = 1.0 / RWKV_HEAD
    mean = _dot_exact_rhs(y, seg) * inv_n
    dy = y - mean
    var = _dot_exact_rhs(dy * dy, seg) * inv_n
    o = dy * lax.rsqrt(var + LNX_EPS) * lnw_ref[...] + lnb_ref[...]
    r = r_ref[0]
    v = v_ref[0]
    rrk = r * rk_ref[...]
    o = o + _dot_exact_rhs(rrk * kdf_ref[0], seg) * v
    o = o + _dot_exact_rhs(rrk * kdb_ref[0], seg) * v
    rw = o * g_ref[0]
    mixed = (lax.dot_general(att_ref[0], wo_ref[0:ATTN_WIDTH, :], (((0,), (0,)), ((), ())),
                             preferred_element_type=F32)
             + jnp.dot(rw.astype(BF16), wo_ref[ATTN_WIDTH:, :], preferred_element_type=F32))
    gt = jnp.where(is_ctx, cgt_ref[...], bgt_ref[0])
    o_ref[0] = x_ref[0] + gt * mixed


def _post_mix(xa, att, y_f, y_b, r, v, kd_f, kd_b, g, modb, modc, w_out, lnw, lnb, r_k, seg, n_lat, tm):
    b, ta, d = xa.shape
    w = RWKV_WIDTH
    tok = lambda ww: pl.BlockSpec((1, tm, ww), lambda bb, i: (bb, i, 0))
    full = lambda a: pl.BlockSpec(a.shape, lambda bb, i, _n=a.ndim: (0,) * _n)
    consts = [w_out, lnw.reshape(1, w), lnb.reshape(1, w), r_k.reshape(1, w), seg]
    return pl.pallas_call(
        functools.partial(_postmix_kernel, tm=tm, n_lat=n_lat),
        grid=(b, ta // tm),
        in_specs=[tok(d), pl.BlockSpec((1, ATTN_WIDTH, tm), lambda bb, i: (bb, 0, i))] + [tok(w)] * 7
        + [pl.BlockSpec((1, 1, d), lambda bb, i: (bb, 0, 2)), pl.BlockSpec((1, d), lambda bb, i: (0, 2))]
        + [full(a) for a in consts],
        out_specs=tok(d),
        out_shape=jax.ShapeDtypeStruct((b, ta, d), F32),
        compiler_params=_cparams(("parallel", "parallel")),
        name="rwkv_finish_out_proj",
    )(xa, att, y_f, y_b, r, v, kd_f, kd_b, g, modb, modc, *consts)


def _ffn_kernel(x_ref, g_ref, bsh_ref, bsc_ref, csh_ref, csc_ref, bgt_ref, cgt_ref,
                w1_ref, w3_ref, w2_ref, o_ref, *, tm, n_lat):
    i = pl.program_id(1)
    is_ctx = _is_ctx_rows(i, tm, n_lat)
    x = x_ref[0]
    h = _modnorm(x, g_ref[...], bsc_ref[0], bsh_ref[0], csc_ref[...], csh_ref[...], is_ctx).astype(BF16)
    a = jnp.dot(h, w1_ref[...], preferred_element_type=F32)
    bb = jnp.dot(h, w3_ref[...], preferred_element_type=F32)
    act = (a * jax.nn.sigmoid(a) * bb).astype(BF16)
    out = jnp.dot(act, w2_ref[...], preferred_element_type=F32)
    gt = jnp.where(is_ctx, cgt_ref[...], bgt_ref[0])
    o_ref[0] = x + gt * out


def _dense_ffn(xa, g2, modb, modc, w1, w3, w2, n_lat, tm):
    b, ta, d = xa.shape
    tok = pl.BlockSpec((1, tm, d), lambda bb, i: (bb, i, 0))
    full = lambda a: pl.BlockSpec(a.shape, lambda bb, i, _n=a.ndim: (0,) * _n,
                                  pipeline_mode=pl.Buffered(1))
    return pl.pallas_call(
        functools.partial(_ffn_kernel, tm=tm, n_lat=n_lat),
        grid=(b, ta // tm),
        in_specs=[tok, pl.BlockSpec((1, d), lambda bb, i: (0, 0))] + _mod_specs(d, 3, 4)
        + [pl.BlockSpec((1, 1, d), lambda bb, i: (bb, 0, 5)), pl.BlockSpec((1, d), lambda bb, i: (0, 5))]
        + [full(w1), full(w3), full(w2)],
        out_specs=tok,
        out_shape=jax.ShapeDtypeStruct((b, ta, d), F32),
        compiler_params=_cparams(("parallel", "parallel")),
        name="norm2_dense_swiglu",
    )(xa, g2, modb, modb, modc, modc, modb, modc, w1, w3, w2)


_NT2 = (((1,), (1,)), ((), ()))
_TN2 = (((0,), (0,)), ((), ()))


def _route_kernel(x_ref, g_ref, bsh_ref, bsc_ref, csh_ref, csc_ref, rw_ref, tri_ref,
                  h_ref, gate_ref, pos_ref, cnt_ref, *, tm, n_lat):
    i = pl.program_id(1)
    is_ctx = _is_ctx_rows(i, tm, n_lat)
    h = _modnorm(x_ref[0], g_ref[...], bsc_ref[0], bsh_ref[0], csc_ref[...], csh_ref[...], is_ctx)
    h_ref[0] = h.astype(BF16)
    hh, hl = _split(h)
    wh, wl = _split(rw_ref[...])
    dg = lambda a, b: lax.dot_general(a, b, _NT2, preferred_element_type=F32)
    logits = dg(wh, hh) + dg(wh, hl) + dg(wl, hh)
    eidx = lax.broadcasted_iota(jnp.int32, logits.shape, 0).astype(F32)
    neg = -jnp.inf
    v1 = jnp.max(logits, axis=0, keepdims=True)
    i1 = jnp.min(jnp.where(logits == v1, eidx, float(N_EXPERTS)), axis=0, keepdims=True)
    m1 = eidx == i1
    rest = jnp.where(m1, neg, logits)
    v2 = jnp.max(rest, axis=0, keepdims=True)
    i2 = jnp.min(jnp.where(rest == v2, eidx, float(N_EXPERTS)), axis=0, keepdims=True)
    m2 = eidx == i2
    e2 = jnp.exp(v2 - v1)
    den = 1.0 + e2
    gate_ref[0] = jnp.where(m1, 1.0 / den, 0.0) + jnp.where(m2, e2 / den, 0.0)
    sel = jnp.logical_or(m1, m2)
    self = jnp.where(sel, 1.0, 0.0)
    rank = jnp.dot(self.astype(BF16), tri_ref[...], preferred_element_type=F32)
    pos_ref[0] = jnp.where(sel, rank, -1.0)
    cnt = jnp.sum(self, axis=1, keepdims=True)
    cnt_ref[0, 0] = jnp.broadcast_to(cnt, (N_EXPERTS, LANES)).astype(jnp.int32)


def _expert_kernel(cnt_ref, x_ref, h_ref, gate_ref, pos_ref, bgt_ref, cgt_ref, w1_ref, w3_ref, w2_ref,
                   o_ref, xg_ref, gs_ref, og_ref, *, tm, n_lat, blk):
    bi = pl.program_id(0)
    ti = pl.program_id(1)
    e = pl.program_id(2)
    f = pl.program_id(3)
    nt = pl.num_programs(1)
    ne = pl.num_programs(2)
    nf = pl.num_programs(3)
    cnt = cnt_ref[(bi * nt + ti) * ne + e]
    nblk = (cnt + blk - 1) // blk
    pos_row = pos_ref[0, pl.ds(e, 1), :]
    gate_row = gate_ref[0, pl.ds(e, 1), :]
    slot = lax.broadcasted_iota(jnp.int32, (blk, tm), 0).astype(F32)

    def one_hot(j):
        return pos_row == slot + (j * blk).astype(F32)

    def rows_of(j):
        return pl.ds(pl.multiple_of(j * blk, blk), blk)

    @pl.when(jnp.logical_and(e == 0, f == 0))
    def _zero():
        o_ref[0] = jnp.zeros(o_ref.shape[1:], F32)

    @pl.when(f == 0)
    def _gather():
        def body(j, carry):
            hit = one_hot(j)
            g = jnp.where(hit, 1.0, 0.0).astype(BF16)
            xg_ref[rows_of(j), :] = jnp.dot(g, h_ref[0], preferred_element_type=F32).astype(BF16)
            gsl = jnp.sum(jnp.where(hit, gate_row, 0.0), axis=1, keepdims=True)
            gs_ref[rows_of(j), :] = jnp.broadcast_to(gsl, (blk, LANES))
            og_ref[rows_of(j), :] = jnp.zeros((blk, og_ref.shape[1]), F32)
            return carry
        lax.fori_loop(0, nblk, body, 0)

    def expert(j, carry):
        xb = xg_ref[rows_of(j), :]
        a = jnp.dot(xb, w1_ref[0], preferred_element_type=F32)
        bb = jnp.dot(xb, w3_ref[0], preferred_element_type=F32)
        act = (a * jax.nn.sigmoid(a) * bb * gs_ref[rows_of(j), 0:1]).astype(BF16)
        og_ref[rows_of(j), :] += jnp.dot(act, w2_ref[0], preferred_element_type=F32)
        return carry
    lax.fori_loop(0, nblk, expert, 0)

    @pl.when(f == nf - 1)
    def _scatter():
        d = o_ref.shape[2]
        def body(j, carry):
            g = jnp.where(one_hot(j), 1.0, 0.0).astype(BF16)
            for c0 in range(0, d, d // 2):
                o = og_ref[rows_of(j), c0:c0 + d // 2].astype(BF16)
                o_ref[0, :, c0:c0 + d // 2] += lax.dot_general(g, o, _TN2, preferred_element_type=F32)
            return carry
        lax.fori_loop(0, nblk, body, 0)

    @pl.when(jnp.logical_and(e == ne - 1, f == nf - 1))
    def _finish():
        is_ctx = _is_ctx_rows(ti, tm, n_lat)
        gt = jnp.where(is_ctx, cgt_ref[...], bgt_ref[0])
        o_ref[0] = x_ref[0] + gt * o_ref[0]


def _moe_ffn(xa, g2, modb, modc, router_w, w1, w3, w2, n_lat, tm, nf):
    b, ta, d = xa.shape
    ne, _, ff = w1.shape
    tf = ff // nf
    nt = ta // tm
    tok = lambda dt: pl.BlockSpec((1, tm, d), lambda bb, i, *_: (bb, i, 0))
    emaj = pl.BlockSpec((1, ne, tm), lambda bb, i, *_: (bb, 0, i))
    tri = jnp.triu(jnp.ones((tm, tm), BF16), 1)
    h, gate, pos, cnt = pl.pallas_call(
        functools.partial(_route_kernel, tm=tm, n_lat=n_lat),
        grid=(b, nt),
        in_specs=[tok(F32), pl.BlockSpec((1, d), lambda bb, i: (0, 0))] + _mod_specs(d, 3, 4)
        + [pl.BlockSpec((ne, d), lambda bb, i: (0, 0)), pl.BlockSpec((tm, tm), lambda bb, i: (0, 0))],
        out_specs=[tok(BF16), emaj, emaj, pl.BlockSpec((1, 1, ne, LANES), lambda bb, i: (bb, i, 0, 0))],
        out_shape=[jax.ShapeDtypeStruct((b, ta, d), BF16), jax.ShapeDtypeStruct((b, ne, ta), F32),
                   jax.ShapeDtypeStruct((b, ne, ta), F32), jax.ShapeDtypeStruct((b, nt, ne, LANES), jnp.int32)],
        compiler_params=_cparams(("parallel", "parallel")),
        name="norm2_moe_route",
    )(xa, g2, modb, modb, modc, modc, router_w.T, tri)
    counts = cnt[:, :, :, 0].reshape(-1)
    once = dict(pipeline_mode=pl.Buffered(1))
    grid_spec = pltpu.PrefetchScalarGridSpec(
        num_scalar_prefetch=1,
        grid=(b, nt, ne, nf),
        in_specs=[pl.BlockSpec((1, tm, d), lambda bb, i, e, f, c: (bb, i, 0), **once),
                  pl.BlockSpec((1, tm, d), lambda bb, i, e, f, c: (bb, i, 0), **once),
                  emaj, emaj,
                  pl.BlockSpec((1, 1, d), lambda bb, i, e, f, c: (bb, 0, 5)),
                  pl.BlockSpec((1, d), lambda bb, i, e, f, c: (0, 5)),
                  pl.BlockSpec((1, d, tf), lambda bb, i, e, f, c: (e, 0, f)),
                  pl.BlockSpec((1, d, tf), lambda bb, i, e, f, c: (e, 0, f)),
                  pl.BlockSpec((1, tf, d), lambda bb, i, e, f, c: (e, f, 0))],
        out_specs=pl.BlockSpec((1, tm, d), lambda bb, i, e, f, c: (bb, i, 0)),
        scratch_shapes=[pltpu.VMEM((tm, d), BF16), pltpu.VMEM((tm, LANES), F32), pltpu.VMEM((tm, d), F32)])
    return pl.pallas_call(
        functools.partial(_expert_kernel, tm=tm, n_lat=n_lat, blk=MOE_SLOT_BLOCK),
        grid_spec=grid_spec,
        out_shape=jax.ShapeDtypeStruct((b, ta, d), F32),
        compiler_params=_cparams(("parallel", "parallel", "arbitrary", "arbitrary")),
        name="moe_experts",
    )(counts, xa, h, gate, pos, modb, modc, w1, w3, w2)


def _final_kernel(x_ref, g_ref, o_ref):
    x = x_ref[0]
    ms = jnp.mean(x * x, axis=-1, keepdims=True)
    o_ref[0] = x * lax.rsqrt(ms + NORM_EPS) * g_ref[...]


def _final_norm(xa, g, seq, tm):
    b, _, d = xa.shape
    return pl.pallas_call(
        _final_kernel,
        grid=(b, seq // tm),
        in_specs=[pl.BlockSpec((1, tm, d), lambda bb, i: (bb, i, 0)),
                  pl.BlockSpec((1, d), lambda bb, i: (0, 0))],
        out_specs=pl.BlockSpec((1, tm, d), lambda bb, i: (bb, i, 0)),
        out_shape=jax.ShapeDtypeStruct((b, seq, d), F32),
        compiler_params=_cparams(("parallel", "parallel")),
        name="final_norm",
    )(xa, g)


def _rope_tables(seq, ctx_len):
    axis_dim = ATTN_QK_DIM // 2
    inv_freq = ROPE_THETA ** (-jnp.arange(0, axis_dim, 2, dtype=F32) / axis_dim)
    t = jnp.arange(seq)
    pos = jnp.stack([(t // GRID_W).astype(F32), (t % GRID_W).astype(F32)], axis=1)
    lane = np.arange(LANES)
    axis = (lane % ATTN_QK_DIM) // axis_dim
    freq = lane % (axis_dim // 2)
    sign = np.where((lane % axis_dim) < axis_dim // 2, -1.0, 1.0).astype(np.float32)
    ang = pos[:, axis] * inv_freq[freq][None, :]
    cos_t = jnp.concatenate([jnp.cos(ang), jnp.ones((ctx_len, LANES), F32)], axis=0)
    sin_t = jnp.concatenate([jnp.sin(ang) * sign[None, :], jnp.zeros((ctx_len, LANES), F32)], axis=0)
    return cos_t, sin_t


def _pick_tile(n, candidates):
    for c in candidates:
        if n % c == 0:
            return c
    raise ValueError(f"no tile in {candidates} divides {n}")


def kernel(x, c, ctx, c_ctx, ada_w, ada_b, norm1_g, norm2_g, w_in, w_out, diff_lambda, subln_g, rwkv_mu, rwkv_w0, rwkv_w2, rwkv_a0, rwkv_a2, rwkv_v0, rwkv_v1, rwkv_v2, rwkv_g2, rwkv_k_k, rwkv_k_a, rwkv_r_k, lnx_w, lnx_b, ffn_w1, ffn_w3, ffn_w2, router_w, exp_w1, exp_w3, exp_w2, final_g):
    b, seq, d = x.shape
    n_lat = seq
    ctx_len = ctx.shape[1]
    depth = ada_w.shape[0]
    ta = n_lat + ctx_len
    assert b + 1 <= 8 and seq % GRID_W == 0 and ctx_len % CHUNK == 0 and seq % CHUNK == 0

    tm = _pick_tile(math.gcd(ctx_len, seq), (256, 128))
    tq = _pick_tile(seq, (512, 256, 128))
    tkk = _pick_tile(ta, (1280, 640, 256, 128))
    tm_moe = _pick_tile(ta, (1280, 640, 256, 128))

    xa = jnp.concatenate([x, ctx], axis=1)
    cvec = jnp.zeros((8, d), F32).at[:b].set(c).at[b].set(c_ctx)
    mod = _modulation(cvec, ada_w, ada_b)
    cos_t, sin_t = _rope_tables(seq, ctx_len)
    seg = jnp.asarray(np.kron(np.eye(RWKV_HEADS), np.ones((RWKV_HEAD, RWKV_HEAD))), BF16)
    n_in = w_in.shape[2]
    n_in_p = 3 * ATTN_WIDTH + 3 * RWKV_WIDTH + LORA_PAD

    v_first = None
    for l in range(depth):
        lam_init = 0.8 - 0.6 * math.exp(-0.3 * l)
        modb = mod[l, :b].reshape(b, 1, 6 * d)
        modc = mod[l, b:b + 1]
        w_in_p = jnp.pad(w_in[l], ((0, 0), (0, n_in_p - n_in))).astype(BF16)
        q, k, v, pr, pk, pv, lora = _in_proj(xa, norm1_g[l].reshape(1, d), modb, modc, w_in_p,
                                             cos_t, sin_t, n_lat, tm)
        sg = subln_g[l].reshape(ATTN_V_DIM, 1)
        att_lat = _attention(q, k, v, diff_lambda[l], sg, lam_init, 0, seq, 0, ta, tq, tkk)
        att_ctx = _attention(q, k, v, diff_lambda[l], sg, lam_init, seq, ctx_len, seq, ctx_len,
                             ctx_len, ctx_len)
        att = jnp.concatenate([att_lat, att_ctx], axis=2)
        vres = None if l == 0 else (v_first, rwkv_v0[l - 1], rwkv_v1[l - 1], rwkv_v2[l - 1])
        r, vv, kk, g, lw_f, lw_b, bt_f, bt_b, kd_f, kd_b = _rwkv_prepare(
            pr, pk, pv, lora, rwkv_mu[l], rwkv_w0[l], rwkv_w2[l], rwkv_a0[l], rwkv_a2[l], rwkv_g2[l],
            rwkv_k_k[l], rwkv_k_a[l], seg, vres, n_lat, tm)
        if l == 0:
            v_first = vv
        y_f, y_b = _rwkv_scan(r, vv, kk, lw_f, lw_b, bt_f, bt_b, kd_f, kd_b, n_lat)
        xa = _post_mix(xa, att, y_f, y_b, r, vv, kd_f, kd_b, g, modb, modc, w_out[l].astype(BF16),
                       lnx_w[l], lnx_b[l], rwkv_r_k[l], seg, n_lat, tm)
        i = l // 2
        if l % 2 == 0:
            xa = _dense_ffn(xa, norm2_g[l].reshape(1, d), modb, modc, ffn_w1[i].astype(BF16),
                            ffn_w3[i].astype(BF16), ffn_w2[i].astype(BF16), n_lat, tm)
        else:
            xa = _moe_ffn(xa, norm2_g[l].reshape(1, d), modb, modc, router_w[i],
                          exp_w1[i].astype(BF16), exp_w3[i].astype(BF16), exp_w2[i].astype(BF16),
                          n_lat, tm_moe, 4)
    return _final_norm(xa, final_g.reshape(1, d), seq, tm)
```

```python
import functools
import math

import numpy as np
import jax
import jax.numpy as jnp
from jax import lax
from jax.experimental import pallas as pl
from jax.experimental.pallas import tpu as pltpu

F32 = jnp.float32
BF16 = jnp.bfloat16

ATTN_HEADS = 4
ATTN_QK_DIM = 64
ATTN_V_DIM = 128
ATTN_WIDTH = ATTN_HEADS * ATTN_V_DIM
RWKV_HEADS = 8
RWKV_HEAD = 64
RWKV_WIDTH = RWKV_HEADS * RWKV_HEAD
DECAY_LORA = 32
AAA_LORA = 32
GATE_LORA = 96
LORA_PAD = 256
N_EXPERTS = 8
GRID_W = 64
ROPE_THETA = 10000.0
NORM_EPS = 1e-6
SUBLN_EPS = 1e-5
LNX_EPS = 64e-5
LANES = 128
CHUNK = 64
INV_BLOCK = 16
MOE_SLOT_BLOCK = 128
SCAN_CHUNKS = 2
QK_LOOKAHEAD = 2
V_ONES = 16
V_EXT = ATTN_V_DIM + V_ONES
VMEM_LIMIT = 56 * 1024 * 1024
Q_SCALE = ATTN_QK_DIM ** -0.5 * math.log2(math.e)


def _cparams(sem, vmem=VMEM_LIMIT, flags=None):
    return pltpu.CompilerParams(dimension_semantics=sem, vmem_limit_bytes=vmem, flags=flags)


def _bdot(a, b):
    return jnp.dot(a.astype(BF16), b.astype(BF16), preferred_element_type=F32)


def _bdot_nt(a, b):
    return lax.dot_general(a.astype(BF16), b.astype(BF16), (((1,), (1,)), ((), ())),
                           preferred_element_type=F32)


def _bdot_tn(a, b):
    return lax.dot_general(a.astype(BF16), b.astype(BF16), (((0,), (0,)), ((), ())),
                           preferred_element_type=F32)


def _split(a):
    hi = a.astype(BF16)
    lo = (a - hi.astype(F32)).astype(BF16)
    return hi, lo


def _dot_exact_rhs(a, b_bf16):
    hi, lo = _split(a)
    return (jnp.dot(hi, b_bf16, preferred_element_type=F32)
            + jnp.dot(lo, b_bf16, preferred_element_type=F32))


def _dot_exact_lhs(a_bf16, b):
    hi, lo = _split(b)
    return (jnp.dot(a_bf16, hi, preferred_element_type=F32)
            + jnp.dot(a_bf16, lo, preferred_element_type=F32))


def _modnorm(x, g, sc_b, sh_b, sc_c, sh_c, is_ctx):
    ms = jnp.mean(x * x, axis=-1, keepdims=True)
    y = x * lax.rsqrt(ms + NORM_EPS) * g
    sc = jnp.where(is_ctx, sc_c, sc_b)
    sh = jnp.where(is_ctx, sh_c, sh_b)
    return y * (1.0 + sc) + sh


def _is_ctx_rows(tile_idx, tm, n_lat):
    row = tile_idx * tm + lax.broadcasted_iota(jnp.int32, (tm, 1), 0)
    return row >= n_lat


def _mod_specs(d, k_sh, k_sc):
    return [
        pl.BlockSpec((1, 1, d), lambda b, i, *_: (b, 0, k_sh)),
        pl.BlockSpec((1, 1, d), lambda b, i, *_: (b, 0, k_sc)),
        pl.BlockSpec((1, d), lambda b, i, *_: (0, k_sh)),
        pl.BlockSpec((1, d), lambda b, i, *_: (0, k_sc)),
    ]


def _mod_kernel(c_ref, w_ref, b_ref, o_ref):
    c = c_ref[...]
    s = c * jax.nn.sigmoid(c)
    o_ref[0] = _bdot(s, w_ref[0]) + b_ref[0]


def _modulation(cvec, ada_w, ada_b):
    depth, d, n = ada_w.shape
    tn = 2048
    return pl.pallas_call(
        _mod_kernel,
        grid=(depth, n // tn),
        in_specs=[pl.BlockSpec((8, d), lambda l, j: (0, 0)),
                  pl.BlockSpec((1, d, tn), lambda l, j: (l, 0, j)),
                  pl.BlockSpec((1, 1, tn), lambda l, j: (l, 0, j))],
        out_specs=pl.BlockSpec((1, 8, tn), lambda l, j: (l, 0, j)),
        out_shape=jax.ShapeDtypeStruct((depth, 8, n), F32),
        compiler_params=_cparams(("parallel", "parallel")),
        name="adaln_mod",
    )(cvec, ada_w, ada_b.reshape(depth, 1, n))


def _inproj_kernel(x_ref, g_ref, bsh_ref, bsc_ref, csh_ref, csc_ref, w_ref, cos_ref, sin_ref,
                   q_ref, k_ref, v_ref, rr_ref, rk_ref, rv_ref, lo_ref, *, tm, n_lat):
    i = pl.program_id(1)
    is_ctx = _is_ctx_rows(i, tm, n_lat)
    h = _modnorm(x_ref[0], g_ref[...], bsc_ref[0], bsh_ref[0], csc_ref[...], csh_ref[...], is_ctx)
    p = jnp.dot(h.astype(BF16), w_ref[...], preferred_element_type=F32)
    cos_t = cos_ref[...]
    sin_t = sin_ref[...]
    lane = lax.broadcasted_iota(jnp.int32, (tm, LANES), 1)
    first = (lane & 31) < 16

    def rope(z):
        partner = jnp.where(first, pltpu.roll(z, LANES - 16, 1), pltpu.roll(z, 16, 1))
        return z * cos_t + partner * sin_t

    for hh in range(ATTN_HEADS):
        lo = LANES * hh
        q_ref[0, lo:lo + LANES, :] = jnp.transpose(rope(p[:, lo:lo + LANES]) * Q_SCALE).astype(BF16)
        k_ref[0, :, lo:lo + LANES] = rope(p[:, ATTN_WIDTH + lo:ATTN_WIDTH + lo + LANES]).astype(BF16)
        vo = 2 * ATTN_WIDTH + lo
        ve = V_EXT * hh
        v_ref[0, ve:ve + ATTN_V_DIM, :] = jnp.transpose(p[:, vo:vo + LANES]).astype(BF16)
        v_ref[0, ve + ATTN_V_DIM:ve + V_EXT, :] = jnp.ones((V_ONES, tm), BF16)
    o = 3 * ATTN_WIDTH
    rr_ref[0] = p[:, o:o + RWKV_WIDTH]
    rk_ref[0] = p[:, o + RWKV_WIDTH:o + 2 * RWKV_WIDTH]
    rv_ref[0] = p[:, o + 2 * RWKV_WIDTH:o + 3 * RWKV_WIDTH]
    o += 3 * RWKV_WIDTH
    lo_ref[0] = p[:, o:o + LORA_PAD]


def _in_proj(xa, g1, modb, modc, w_in_p, cos_t, sin_t, n_lat, tm):
    b, ta, d = xa.shape
    n = w_in_p.shape[1]
    tok = lambda w: pl.BlockSpec((1, tm, w), lambda bb, i: (bb, i, 0))
    feat = pl.BlockSpec((1, ATTN_WIDTH, tm), lambda bb, i: (bb, 0, i))
    feat_shape = jax.ShapeDtypeStruct((b, ATTN_WIDTH, ta), BF16)
    vext = pl.BlockSpec((1, ATTN_HEADS * V_EXT, tm), lambda bb, i: (bb, 0, i))
    vext_shape = jax.ShapeDtypeStruct((b, ATTN_HEADS * V_EXT, ta), BF16)
    outs = [feat_shape, jax.ShapeDtypeStruct((b, ta, ATTN_WIDTH), BF16), vext_shape] \
        + [jax.ShapeDtypeStruct((b, ta, RWKV_WIDTH), F32)] * 3 \
        + [jax.ShapeDtypeStruct((b, ta, LORA_PAD), F32)]
    return pl.pallas_call(
        functools.partial(_inproj_kernel, tm=tm, n_lat=n_lat),
        grid=(b, ta // tm),
        in_specs=[tok(d), pl.BlockSpec((1, d), lambda bb, i: (0, 0))] + _mod_specs(d, 0, 1)
        + [pl.BlockSpec((d, n), lambda bb, i: (0, 0)),
           pl.BlockSpec((tm, LANES), lambda bb, i: (i, 0)),
           pl.BlockSpec((tm, LANES), lambda bb, i: (i, 0))],
        out_specs=[feat, tok(ATTN_WIDTH), vext] + [tok(RWKV_WIDTH)] * 3 + [tok(LORA_PAD)],
        out_shape=outs,
        compiler_params=_cparams(("parallel", "parallel")),
        name="norm1_in_proj",
    )(xa, g1, modb, modb, modc, modc, w_in_p, cos_t, sin_t)


def _attn_kernel(lam_ref, sg_ref, q_ref, k_ref, v_ref, o_ref, s_ref,
                 *, tq, k0, nkeys, tkk, ts, lam_init):
    q = q_ref[0]
    feat = lax.broadcasted_iota(jnp.int32, (LANES, tq), 0)
    zero = jnp.zeros_like(q)
    qs = jnp.concatenate([jnp.where(feat < ATTN_QK_DIM, q, zero),
                          jnp.where(feat >= ATTN_QK_DIM, q, zero)], axis=1)
    n_tiles = nkeys // tkk
    n_sub = tkk // ts

    def scores(t):
        mx = None
        for j in range(n_sub):
            r0 = k0 + t * tkk + j * ts
            s_j = jnp.dot(k_ref[0, r0:r0 + ts, :], qs, preferred_element_type=F32)
            s_ref[t % 2, j * ts:(j + 1) * ts, :] = s_j
            cm = jnp.max(s_j, axis=0, keepdims=True)
            mx = cm if mx is None else jnp.maximum(mx, cm)
        return mx

    m_run = jnp.full((1, 2 * tq), -jnp.inf, F32)
    acc = jnp.zeros((V_EXT, 2 * tq), F32)
    mx_next = scores(0)
    for t in range(n_tiles):
        mx_cur = mx_next
        if t + 1 < n_tiles:
            mx_next = scores(t + 1)
        m_new = jnp.maximum(m_run, mx_cur)
        alpha = jnp.exp2(m_run - m_new)
        pv = None
        for j in range(n_sub):
            p_j = jnp.exp2(s_ref[t % 2, j * ts:(j + 1) * ts, :] - m_new).astype(BF16)
            c0 = k0 + t * tkk + j * ts
            d = jnp.dot(v_ref[0, :, c0:c0 + ts], p_j, preferred_element_type=F32)
            pv = d if pv is None else pv + d
        acc = alpha * acc + pv
        m_run = m_new

    lp = lam_ref[...]
    lam = (jnp.exp(jnp.sum(lp[0:1] * lp[1:2], axis=1, keepdims=True))
           - jnp.exp(jnp.sum(lp[2:3] * lp[3:4], axis=1, keepdims=True)) + lam_init)
    l = acc[ATTN_V_DIM:ATTN_V_DIM + 1, :]
    o1 = acc[0:ATTN_V_DIM, 0:tq] / l[:, 0:tq]
    o2 = acc[0:ATTN_V_DIM, tq:2 * tq] / l[:, tq:2 * tq]
    o = o1 - lam * o2
    ms = jnp.mean(o * o, axis=0, keepdims=True)
    y = o * lax.rsqrt(ms + SUBLN_EPS) * sg_ref[...]
    o_ref[0] = (y * (1.0 - lam_init)).astype(BF16)


def _attention(q_t, k, v_t, lam_p, subln_g, lam_init, q0, nq, k0, nkeys, tq, tkk):
    b, ta, _ = k.shape
    qo = q0 // tq
    return pl.pallas_call(
        functools.partial(_attn_kernel, tq=tq, k0=k0, nkeys=nkeys, tkk=tkk, ts=_pick_tile(tkk, (256, LANES)),
                          lam_init=lam_init),
        grid=(b, ATTN_HEADS, nq // tq),
        in_specs=[pl.BlockSpec((4, ATTN_QK_DIM), lambda bb, h, i: (0, 0)),
                  pl.BlockSpec((ATTN_V_DIM, 1), lambda bb, h, i: (0, 0)),
                  pl.BlockSpec((1, LANES, tq), lambda bb, h, i: (bb, h, i + qo)),
                  pl.BlockSpec((1, ta, LANES), lambda bb, h, i: (bb, 0, h)),
                  pl.BlockSpec((1, V_EXT, ta), lambda bb, h, i: (bb, h, 0))],
        out_specs=pl.BlockSpec((1, ATTN_V_DIM, tq), lambda bb, h, i: (bb, h, i)),
        out_shape=jax.ShapeDtypeStruct((b, ATTN_WIDTH, nq), BF16),
        scratch_shapes=[pltpu.VMEM((2, tkk, 2 * tq), F32)],
        compiler_params=_cparams(("parallel", "parallel", "parallel")),
        name="diff_attention",
    )(lam_p, subln_g, q_t, k, v_t)


def _prep_kernel(*refs, tt, n_lat, ta, has_vres):
    (pr_ref, pk_ref, pv_ref, hpr_ref, hpk_ref, hpv_ref, hnr_ref, hnk_ref, hnv_ref, lo_ref,
     mu_ref, w0_ref, w2_ref, a0_ref, a2_ref, g2_ref, kk_ref, ka_ref, seg_ref) = refs[:19]
    n_in = 19
    if has_vres:
        vf_ref, v0_ref, v1_ref, v2_ref = refs[19:23]
        n_in = 23
    (r_out, v_out, kk_out, g_out, lwf_out, lwb_out, bf_out, bb_out, kdf_out, kdb_out) = refs[n_in:]

    i = pl.program_id(1)
    row = lax.broadcasted_iota(jnp.int32, (tt, 1), 0)
    gidx = i * tt + row
    no_prev = jnp.logical_or(gidx == 0, gidx == n_lat)
    no_next = jnp.logical_or(gidx == n_lat - 1, gidx == ta - 1)

    def shift_mix(p_ref, hp_ref, hn_ref, j):
        p = p_ref[0]
        pp = jnp.where(row == 0, hp_ref[0, 7:8, :], pltpu.roll(p, 1, 0))
        pp = jnp.where(no_prev, 0.0, pp)
        pn = jnp.where(row == tt - 1, hn_ref[0, 0:1, :], pltpu.roll(p, tt - 1, 0))
        pn = jnp.where(no_next, 0.0, pn)
        return p + mu_ref[2 * j:2 * j + 1, :] * (pp - p) + mu_ref[2 * j + 1:2 * j + 2, :] * (pn - p)

    r = shift_mix(pr_ref, hpr_ref, hnr_ref, 0)
    k = shift_mix(pk_ref, hpk_ref, hnk_ref, 1)
    v = shift_mix(pv_ref, hpv_ref, hnv_ref, 2)
    if has_vres:
        mix = jax.nn.sigmoid(v0_ref[...] + _bdot(_bdot(v, v1_ref[...]), v2_ref[...]))
        v = v + (vf_ref[0] - v) * mix
    lo = lo_ref[0]
    gd = lo[:, 2 * DECAY_LORA + 2 * AAA_LORA:2 * DECAY_LORA + 2 * AAA_LORA + GATE_LORA]
    g_out[0] = _bdot(jax.nn.sigmoid(gd), g2_ref[...])
    kk = k * kk_ref[...]
    ss = _dot_exact_rhs(kk * kk, seg_ref[...])
    kk = kk * lax.rsqrt(jnp.maximum(ss, 1e-24))
    r_out[0] = r
    v_out[0] = v
    kk_out[0] = kk
    lw_outs = (lwf_out, lwb_out)
    b_outs = (bf_out, bb_out)
    kd_outs = (kdf_out, kdb_out)
    for d in range(2):
        wd = lo[:, d * DECAY_LORA:(d + 1) * DECAY_LORA]
        wl = w0_ref[d:d + 1, :] + _bdot(jnp.tanh(wd), w2_ref[d])
        lw_outs[d][0] = -jax.nn.sigmoid(wl) * math.exp(-0.5)
        ad = lo[:, 2 * DECAY_LORA + d * AAA_LORA:2 * DECAY_LORA + (d + 1) * AAA_LORA]
        a = jax.nn.sigmoid(a0_ref[d:d + 1, :] + _bdot(ad, a2_ref[d]))
        b_outs[d][0] = a * kk
        kd_outs[d][0] = k * (1.0 + (a - 1.0) * ka_ref[...])


def _rwkv_prepare(pr, pk, pv, lora, mu, w0, w2, a0, a2, g2, k_k, k_a, seg, vres, n_lat, tt):
    b, ta, w = pr.shape
    nb8 = ta // 8
    tok = pl.BlockSpec((1, tt, w), lambda bb, i: (bb, i, 0))
    prev = pl.BlockSpec((1, 8, w), lambda bb, i: (bb, jnp.maximum(i * (tt // 8) - 1, 0), 0))
    nxt = pl.BlockSpec((1, 8, w), lambda bb, i: (bb, jnp.minimum((i + 1) * (tt // 8), nb8 - 1), 0))
    full = lambda a: pl.BlockSpec(a.shape, lambda bb, i, _n=a.ndim: (0,) * _n)
    consts = [mu.reshape(6, w), w0, w2, a0, a2, g2, k_k.reshape(1, w), k_a.reshape(1, w), seg]
    ins = [pr, pk, pv, pr, pk, pv, pr, pk, pv, lora] + consts
    specs = [tok] * 3 + [prev] * 3 + [nxt] * 3 \
        + [pl.BlockSpec((1, tt, LORA_PAD), lambda bb, i: (bb, i, 0))] + [full(a) for a in consts]
    if vres is not None:
        v_first, v0, v1, v2 = vres
        extra = [v0.reshape(1, w), v1, v2]
        ins += [v_first] + extra
        specs += [tok] + [full(a) for a in extra]
    return pl.pallas_call(
        functools.partial(_prep_kernel, tt=tt, n_lat=n_lat, ta=ta, has_vres=vres is not None),
        grid=(b, ta // tt),
        in_specs=specs,
        out_specs=[tok] * 10,
        out_shape=[jax.ShapeDtypeStruct((b, ta, w), F32)] * 10,
        compiler_params=_cparams(("parallel", "parallel")),
        name="rwkv_prepare",
    )(*ins)


_NN = (((2,), (1,)), ((0,), (0,)))
_NT = (((2,), (2,)), ((0,), (0,)))
_TN = (((1,), (1,)), ((0,), (0,)))


def _bmm(a, b, dims=_NN):
    return lax.dot_general(a.astype(BF16), b.astype(BF16), dims, preferred_element_type=F32)


def _scan_kernel(rf_ref, vf_ref, kkf_ref, lwf_ref, btf_ref, kdf_ref,
                 rb_ref, vb_ref, kkb_ref, lwb_ref, btb_ref, kdb_ref,
                 yf_ref, yb_ref, st_ref, *, nsub):
    s = pl.program_id(1)
    c = CHUNK
    n = RWKV_HEAD
    g = 2 * RWKV_HEADS

    @pl.when(s == 0)
    def _init():
        st_ref[...] = jnp.zeros(st_ref.shape, F32)

    t_i = lax.broadcasted_iota(jnp.int32, (c, c), 0)
    j_i = lax.broadcasted_iota(jnp.int32, (c, c), 1)
    t2 = lax.broadcasted_iota(jnp.int32, (2 * c, 2 * c), 0)
    j2 = lax.broadcasted_iota(jnp.int32, (2 * c, 2 * c), 1)
    tt2 = t2 & (c - 1)
    jj2 = j2 & (c - 1)
    eye = jnp.where(t_i == j_i, 1.0, 0.0)
    blk_bits = int(math.log2(INV_BLOCK))
    same_blk = (t_i >> blk_bits) == (j_i >> blk_bits)
    heads = lambda z: [z[:, h * n:(h + 1) * n] for h in range(RWKV_HEADS)]
    dirs = ((rf_ref, vf_ref, kkf_ref, lwf_ref, btf_ref, kdf_ref, False),
            (rb_ref, vb_ref, kkb_ref, lwb_ref, btb_ref, kdb_ref, True))
    at_l, rt_l, v_l, bk_l, pt_l, m_l = [], [], [], [], [], []
    for t in range(nsub):
        for r_ref, v_ref, kk_ref, lw_ref, bt_ref, kd_ref, rev in dirs:
            ci = nsub - 1 - t if rev else t
            rows = slice(ci * c, (ci + 1) * c)
            incl = (j_i >= t_i) if rev else (j_i <= t_i)
            tri = jnp.where(incl, 1.0, 0.0).astype(BF16)
            before = (jj2 > tt2) if rev else (jj2 < tt2)
            mask_m = jnp.logical_or(before, jnp.logical_and(t2 >= c, jj2 == tt2))
            lw = lw_ref[0, rows, :]
            cs = _dot_exact_lhs(tri, lw)
            tot = cs[0:1, :] if rev else cs[c - 1:c, :]
            p_inv = jnp.exp(-cs)
            p_rest = jnp.exp(tot - cs)
            bt = bt_ref[0, rows, :]
            kd = kd_ref[0, rows, :]
            rt = heads(r_ref[0, rows, :] * jnp.exp(cs))
            at = heads(-kk_ref[0, rows, :] * jnp.exp(cs - lw))
            bti = heads(bt * p_inv)
            kti = heads(kd * p_inv)
            bp = heads(bt * p_rest)
            kp = heads(kd * p_rest)
            x1 = jnp.stack([jnp.concatenate([a, b], axis=0) for a, b in zip(at, rt)])
            x2 = jnp.stack([jnp.concatenate([a, b], axis=0) for a, b in zip(bti, kti)])
            m_l.append(jnp.where(mask_m[None], _bmm(x1, x2, _NT), 0.0))
            at_l += at
            rt_l += rt
            v_l += heads(v_ref[0, rows, :])
            bk_l += [jnp.concatenate([a, b], axis=0) for a, b in zip(bp, kp)]
            pt_l += heads(jnp.exp(tot))
    m = jnp.concatenate(m_l, axis=0)
    at_s = jnp.stack(at_l)
    rt_s = jnp.stack(rt_l)
    v_s = jnp.stack(v_l)
    bk_s = jnp.stack(bk_l)
    pt_s = jnp.stack(pt_l)

    a_full = m[:, 0:c, 0:c]
    d_pow = jnp.where(same_blk[None], a_full, 0.0)
    l_off = a_full - d_pow
    t_diag = eye[None] + d_pow
    for _ in range(blk_bits - 1):
        d_pow = _bmm(d_pow, d_pow)
        t_diag = t_diag + _bmm(d_pow, t_diag)
    n_mat = _bmm(t_diag, l_off)
    x_mat = t_diag + _bmm(n_mat, t_diag)
    tinv = x_mat + _bmm(_bmm(n_mat, n_mat), x_mat)

    akv = _bmm(m[:, 0:c, c:2 * c], v_s)

    st = st_ref[...]
    for t in range(nsub):
        sl = slice(t * g, (t + 1) * g)
        u = _bmm(tinv[sl], akv[sl] + _bmm(at_s[sl], st, _NT))
        uv = jnp.concatenate([u, v_s[sl]], axis=1)
        y = _bmm(m[sl, c:2 * c, :], uv) + _bmm(rt_s[sl], st, _NT)
        st = pt_s[sl] * st + _bmm(uv, bk_s[sl], _TN)
        rf = slice(t * c, (t + 1) * c)
        rb = slice((nsub - 1 - t) * c, (nsub - t) * c)
        for h in range(RWKV_HEADS):
            yf_ref[0, rf, h * n:(h + 1) * n] = y[h]
            yb_ref[0, rb, h * n:(h + 1) * n] = y[RWKV_HEADS + h]
    st_ref[...] = st


def _rwkv_scan(r, v, kk, lw_f, lw_b, bt_f, bt_b, kd_f, kd_b, n_lat):
    b, ta, w = r.shape
    nsub = SCAN_CHUNKS
    rows = nsub * CHUNK
    assert n_lat % rows == 0 and ta % rows == 0
    nc = ta // rows
    nlat = n_lat // rows
    nctx = nc - nlat

    def fwd_idx(s):
        return jnp.where(s < nctx, nlat + s, s - nctx)

    def rev_idx(s):
        return nc - 1 - s

    fwd = pl.BlockSpec((1, rows, w), lambda bb, s: (bb, fwd_idx(s), 0))
    bwd = pl.BlockSpec((1, rows, w), lambda bb, s: (bb, rev_idx(s), 0))
    return pl.pallas_call(
        functools.partial(_scan_kernel, nsub=nsub),
        grid=(b, nc),
        in_specs=[fwd] * 6 + [bwd] * 6,
        out_specs=[fwd, bwd],
        out_shape=[jax.ShapeDtypeStruct((b, ta, w), F32)] * 2,
        scratch_shapes=[pltpu.VMEM((2 * RWKV_HEADS, RWKV_HEAD, RWKV_HEAD), F32)],
        compiler_params=_cparams(("parallel", "arbitrary")),
        name="rwkv_scan",
    )(r, v, kk, lw_f, bt_f, kd_f, r, v, kk, lw_b, bt_b, kd_b)


def _postmix_kernel(x_ref, att_ref, yf_ref, yb_ref, r_ref, v_ref, kdf_ref, kdb_ref, g_ref,
                    bgt_ref, cgt_ref, wo_ref, lnw_ref, lnb_ref, rk_ref, seg_ref, o_ref, *, tm, n_lat):
    i = pl.program_id(1)
    is_ctx = _is_ctx_rows(i, tm, n_lat)
    seg = seg_ref[...]
    y = yf_ref[0] + yb_ref[0]
    inv_n = 1.0 / RWKV_HEAD
    mean = _dot_exact_rhs(y, seg) * inv_n
    dy = y - mean
    var = _dot_exact_rhs(dy * dy, seg) * inv_n
    o = dy * lax.rsqrt(var + LNX_EPS) * lnw_ref[...] + lnb_ref[...]
    r = r_ref[0]
    v = v_ref[0]
    rrk = r * rk_ref[...]
    o = o + _dot_exact_rhs(rrk * kdf_ref[0], seg) * v
    o = o + _dot_exact_rhs(rrk * kdb_ref[0], seg) * v
    rw = o * g_ref[0]
    mixed = (lax.dot_general(att_ref[0], wo_ref[0:ATTN_WIDTH, :], (((0,), (0,)), ((), ())),
                             preferred_element_type=F32)
             + jnp.dot(rw.astype(BF16), wo_ref[ATTN_WIDTH:, :], preferred_element_type=F32))
    gt = jnp.where(is_ctx, cgt_ref[...], bgt_ref[0])
    o_ref[0] = x_ref[0] + gt * mixed


def _post_mix(xa, att, y_f, y_b, r, v, kd_f, kd_b, g, modb, modc, w_out, lnw, lnb, r_k, seg, n_lat, tm):
    b, ta, d = xa.shape
    w = RWKV_WIDTH
    tok = lambda ww: pl.BlockSpec((1, tm, ww), lambda bb, i: (bb, i, 0))
    full = lambda a: pl.BlockSpec(a.shape, lambda bb, i, _n=a.ndim: (0,) * _n)
    consts = [w_out, lnw.reshape(1, w), lnb.reshape(1, w), r_k.reshape(1, w), seg]
    return pl.pallas_call(
        functools.partial(_postmix_kernel, tm=tm, n_lat=n_lat),
        grid=(b, ta // tm),
        in_specs=[tok(d), pl.BlockSpec((1, ATTN_WIDTH, tm), lambda bb, i: (bb, 0, i))] + [tok(w)] * 7
        + [pl.BlockSpec((1, 1, d), lambda bb, i: (bb, 0, 2)), pl.BlockSpec((1, d), lambda bb, i: (0, 2))]
        + [full(a) for a in consts],
        out_specs=tok(d),
        out_shape=jax.ShapeDtypeStruct((b, ta, d), F32),
        compiler_params=_cparams(("parallel", "parallel")),
        name="rwkv_finish_out_proj",
    )(xa, att, y_f, y_b, r, v, kd_f, kd_b, g, modb, modc, *consts)


def _ffn_kernel(x_ref, g_ref, bsh_ref, bsc_ref, csh_ref, csc_ref, bgt_ref, cgt_ref,
                w1_ref, w3_ref, w2_ref, o_ref, *, tm, n_lat):
    i = pl.program_id(1)
    is_ctx = _is_ctx_rows(i, tm, n_lat)
    x = x_ref[0]
    h = _modnorm(x, g_ref[...], bsc_ref[0], bsh_ref[0], csc_ref[...], csh_ref[...], is_ctx).astype(BF16)
    a = jnp.dot(h, w1_ref[...], preferred_element_type=F32)
    bb = jnp.dot(h, w3_ref[...], preferred_element_type=F32)
    act = (a * jax.nn.sigmoid(a) * bb).astype(BF16)
    out = jnp.dot(act, w2_ref[...], preferred_element_type=F32)
    gt = jnp.where(is_ctx, cgt_ref[...], bgt_ref[0])
    o_ref[0] = x + gt * out


def _dense_ffn(xa, g2, modb, modc, w1, w3, w2, n_lat, tm):
    b, ta, d = xa.shape
    tok = pl.BlockSpec((1, tm, d), lambda bb, i: (bb, i, 0))
    full = lambda a: pl.BlockSpec(a.shape, lambda bb, i, _n=a.ndim: (0,) * _n,
                                  pipeline_mode=pl.Buffered(1))
    return pl.pallas_call(
        functools.partial(_ffn_kernel, tm=tm, n_lat=n_lat),
        grid=(b, ta // tm),
        in_specs=[tok, pl.BlockSpec((1, d), lambda bb, i: (0, 0))] + _mod_specs(d, 3, 4)
        + [pl.BlockSpec((1, 1, d), lambda bb, i: (bb, 0, 5)), pl.BlockSpec((1, d), lambda bb, i: (0, 5))]
        + [full(w1), full(w3), full(w2)],
        out_specs=tok,
        out_shape=jax.ShapeDtypeStruct((b, ta, d), F32),
        compiler_params=_cparams(("parallel", "parallel")),
        name="norm2_dense_swiglu",
    )(xa, g2, modb, modb, modc, modc, modb, modc, w1, w3, w2)


_NT2 = (((1,), (1,)), ((), ()))
_TN2 = (((0,), (0,)), ((), ()))


def _route_kernel(x_ref, g_ref, bsh_ref, bsc_ref, csh_ref, csc_ref, rw_ref, tri_ref,
                  h_ref, gate_ref, pos_ref, cnt_ref, *, tm, n_lat):
    i = pl.program_id(1)
    is_ctx = _is_ctx_rows(i, tm, n_lat)
    h = _modnorm(x_ref[0], g_ref[...], bsc_ref[0], bsh_ref[0], csc_ref[...], csh_ref[...], is_ctx)
    h_ref[0] = h.astype(BF16)
    hh, hl = _split(h)
    wh, wl = _split(rw_ref[...])
    dg = lambda a, b: lax.dot_general(a, b, _NT2, preferred_element_type=F32)
    logits = dg(wh, hh) + dg(wh, hl) + dg(wl, hh)
    eidx = lax.broadcasted_iota(jnp.int32, logits.shape, 0).astype(F32)
    neg = -jnp.inf
    v1 = jnp.max(logits, axis=0, keepdims=True)
    i1 = jnp.min(jnp.where(logits == v1, eidx, float(N_EXPERTS)), axis=0, keepdims=True)
    m1 = eidx == i1
    rest = jnp.where(m1, neg, logits)
    v2 = jnp.max(rest, axis=0, keepdims=True)
    i2 = jnp.min(jnp.where(rest == v2, eidx, float(N_EXPERTS)), axis=0, keepdims=True)
    m2 = eidx == i2
    e2 = jnp.exp(v2 - v1)
    den = 1.0 + e2
    gate_ref[0] = jnp.where(m1, 1.0 / den, 0.0) + jnp.where(m2, e2 / den, 0.0)
    sel = jnp.logical_or(m1, m2)
    self = jnp.where(sel, 1.0, 0.0)
    rank = jnp.dot(self.astype(BF16), tri_ref[...], preferred_element_type=F32)
    pos_ref[0] = jnp.where(sel, rank, -1.0)
    cnt = jnp.sum(self, axis=1, keepdims=True)
    cnt_ref[0, 0] = jnp.broadcast_to(cnt, (N_EXPERTS, LANES)).astype(jnp.int32)


def _expert_kernel(cnt_ref, x_ref, h_ref, gate_ref, pos_ref, bgt_ref, cgt_ref, w1_ref, w3_ref, w2_ref,
                   o_ref, xg_ref, gs_ref, og_ref, *, tm, n_lat, blk):
    bi = pl.program_id(0)
    ti = pl.program_id(1)
    e = pl.program_id(2)
    f = pl.program_id(3)
    nt = pl.num_programs(1)
    ne = pl.num_programs(2)
    nf = pl.num_programs(3)
    cnt = cnt_ref[(bi * nt + ti) * ne + e]
    nblk = (cnt + blk - 1) // blk
    pos_row = pos_ref[0, pl.ds(e, 1), :]
    gate_row = gate_ref[0, pl.ds(e, 1), :]
    slot = lax.broadcasted_iota(jnp.int32, (blk, tm), 0).astype(F32)

    def one_hot(j):
        return pos_row == slot + (j * blk).astype(F32)

    def rows_of(j):
        return pl.ds(pl.multiple_of(j * blk, blk), blk)

    nblk2 = (cnt + 2 * blk - 1) // (2 * blk)
    slot2 = lax.broadcasted_iota(jnp.int32, (2 * blk, tm), 0).astype(F32)

    def rows2_of(j):
        return pl.ds(pl.multiple_of(j * 2 * blk, 2 * blk), 2 * blk)

    @pl.when(jnp.logical_and(e == 0, f == 0))
    def _zero():
        o_ref[0] = jnp.zeros(o_ref.shape[1:], F32)

    @pl.when(f == 0)
    def _gather():
        def body(j, carry):
            hit = one_hot(j)
            g = jnp.where(hit, 1.0, 0.0).astype(BF16)
            xg_ref[rows_of(j), :] = jnp.dot(g, h_ref[0], preferred_element_type=F32).astype(BF16)
            gsl = jnp.sum(jnp.where(hit, gate_row, 0.0), axis=1, keepdims=True)
            gs_ref[rows_of(j), :] = jnp.broadcast_to(gsl, (blk, LANES))
            return carry
        lax.fori_loop(0, nblk, body, 0)

        def clear(j, carry):
            og_ref[rows2_of(j), :] = jnp.zeros((2 * blk, og_ref.shape[1]), F32)
            return carry
        lax.fori_loop(0, nblk2, clear, 0)

    def expert(j, carry):
        xb = xg_ref[rows_of(j), :]
        a = jnp.dot(xb, w1_ref[0], preferred_element_type=F32)
        bb = jnp.dot(xb, w3_ref[0], preferred_element_type=F32)
        act = (a * jax.nn.sigmoid(a) * bb * gs_ref[rows_of(j), 0:1]).astype(BF16)
        og_ref[rows_of(j), :] += jnp.dot(act, w2_ref[0], preferred_element_type=F32)
        return carry
    lax.fori_loop(0, nblk, expert, 0)

    @pl.when(f == nf - 1)
    def _scatter():
        d = o_ref.shape[2]
        def body(j, carry):
            hit = pos_row == slot2 + (j * 2 * blk).astype(F32)
            g = jnp.where(hit, 1.0, 0.0).astype(BF16)
            for c0 in range(0, d, d // 2):
                o = og_ref[rows2_of(j), c0:c0 + d // 2].astype(BF16)
                o_ref[0, :, c0:c0 + d // 2] += lax.dot_general(g, o, _TN2, preferred_element_type=F32)
            return carry
        lax.fori_loop(0, nblk2, body, 0)

    @pl.when(jnp.logical_and(e == ne - 1, f == nf - 1))
    def _finish():
        is_ctx = _is_ctx_rows(ti, tm, n_lat)
        gt = jnp.where(is_ctx, cgt_ref[...], bgt_ref[0])
        o_ref[0] = x_ref[0] + gt * o_ref[0]


def _moe_ffn(xa, g2, modb, modc, router_w, w1, w3, w2, n_lat, tm, nf):
    b, ta, d = xa.shape
    ne, _, ff = w1.shape
    tf = ff // nf
    nt = ta // tm
    assert tm % (2 * MOE_SLOT_BLOCK) == 0
    tok = lambda dt: pl.BlockSpec((1, tm, d), lambda bb, i, *_: (bb, i, 0))
    emaj = pl.BlockSpec((1, ne, tm), lambda bb, i, *_: (bb, 0, i))
    tri = jnp.triu(jnp.ones((tm, tm), BF16), 1)
    h, gate, pos, cnt = pl.pallas_call(
        functools.partial(_route_kernel, tm=tm, n_lat=n_lat),
        grid=(b, nt),
        in_specs=[tok(F32), pl.BlockSpec((1, d), lambda bb, i: (0, 0))] + _mod_specs(d, 3, 4)
        + [pl.BlockSpec((ne, d), lambda bb, i: (0, 0)), pl.BlockSpec((tm, tm), lambda bb, i: (0, 0))],
        out_specs=[tok(BF16), emaj, emaj, pl.BlockSpec((1, 1, ne, LANES), lambda bb, i: (bb, i, 0, 0))],
        out_shape=[jax.ShapeDtypeStruct((b, ta, d), BF16), jax.ShapeDtypeStruct((b, ne, ta), F32),
                   jax.ShapeDtypeStruct((b, ne, ta), F32), jax.ShapeDtypeStruct((b, nt, ne, LANES), jnp.int32)],
        compiler_params=_cparams(("parallel", "parallel")),
        name="norm2_moe_route",
    )(xa, g2, modb, modb, modc, modc, router_w.T, tri)
    counts = cnt[:, :, :, 0].reshape(-1)
    once = dict(pipeline_mode=pl.Buffered(1))
    grid_spec = pltpu.PrefetchScalarGridSpec(
        num_scalar_prefetch=1,
        grid=(b, nt, ne, nf),
        in_specs=[pl.BlockSpec((1, tm, d), lambda bb, i, e, f, c: (bb, i, 0), **once),
                  pl.BlockSpec((1, tm, d), lambda bb, i, e, f, c: (bb, i, 0), **once),
                  emaj, emaj,
                  pl.BlockSpec((1, 1, d), lambda bb, i, e, f, c: (bb, 0, 5)),
                  pl.BlockSpec((1, d), lambda bb, i, e, f, c: (0, 5)),
                  pl.BlockSpec((1, d, tf), lambda bb, i, e, f, c: (e, 0, f)),
                  pl.BlockSpec((1, d, tf), lambda bb, i, e, f, c: (e, 0, f)),
                  pl.BlockSpec((1, tf, d), lambda bb, i, e, f, c: (e, f, 0))],
        out_specs=pl.BlockSpec((1, tm, d), lambda bb, i, e, f, c: (bb, i, 0)),
        scratch_shapes=[pltpu.VMEM((tm, d), BF16), pltpu.VMEM((tm, LANES), F32), pltpu.VMEM((tm, d), F32)])
    return pl.pallas_call(
        functools.partial(_expert_kernel, tm=tm, n_lat=n_lat, blk=MOE_SLOT_BLOCK),
        grid_spec=grid_spec,
        out_shape=jax.ShapeDtypeStruct((b, ta, d), F32),
        compiler_params=_cparams(("parallel", "parallel", "arbitrary", "arbitrary")),
        name="moe_experts",
    )(counts, xa, h, gate, pos, modb, modc, w1, w3, w2)


def _final_kernel(x_ref, g_ref, o_ref):
    x = x_ref[0]
    ms = jnp.mean(x * x, axis=-1, keepdims=True)
    o_ref[0] = x * lax.rsqrt(ms + NORM_EPS) * g_ref[...]


def _final_norm(xa, g, seq, tm):
    b, _, d = xa.shape
    return pl.pallas_call(
        _final_kernel,
        grid=(b, seq // tm),
        in_specs=[pl.BlockSpec((1, tm, d), lambda bb, i: (bb, i, 0)),
                  pl.BlockSpec((1, d), lambda bb, i: (0, 0))],
        out_specs=pl.BlockSpec((1, tm, d), lambda bb, i: (bb, i, 0)),
        out_shape=jax.ShapeDtypeStruct((b, seq, d), F32),
        compiler_params=_cparams(("parallel", "parallel")),
        name="final_norm",
    )(xa, g)


def _rope_tables(seq, ctx_len):
    axis_dim = ATTN_QK_DIM // 2
    inv_freq = ROPE_THETA ** (-jnp.arange(0, axis_dim, 2, dtype=F32) / axis_dim)
    t = jnp.arange(seq)
    pos = jnp.stack([(t // GRID_W).astype(F32), (t % GRID_W).astype(F32)], axis=1)
    lane = np.arange(LANES)
    axis = (lane % ATTN_QK_DIM) // axis_dim
    freq = lane % (axis_dim // 2)
    sign = np.where((lane % axis_dim) < axis_dim // 2, -1.0, 1.0).astype(np.float32)
    ang = pos[:, axis] * inv_freq[freq][None, :]
    cos_t = jnp.concatenate([jnp.cos(ang), jnp.ones((ctx_len, LANES), F32)], axis=0)
    sin_t = jnp.concatenate([jnp.sin(ang) * sign[None, :], jnp.zeros((ctx_len, LANES), F32)], axis=0)
    return cos_t, sin_t


def _pick_tile(n, candidates):
    for c in candidates:
        if n % c == 0:
            return c
    raise ValueError(f"no tile in {candidates} divides {n}")


def kernel(x, c, ctx, c_ctx, ada_w, ada_b, norm1_g, norm2_g, w_in, w_out, diff_lambda, subln_g, rwkv_mu, rwkv_w0, rwkv_w2, rwkv_a0, rwkv_a2, rwkv_v0, rwkv_v1, rwkv_v2, rwkv_g2, rwkv_k_k, rwkv_k_a, rwkv_r_k, lnx_w, lnx_b, ffn_w1, ffn_w3, ffn_w2, router_w, exp_w1, exp_w3, exp_w2, final_g):
    b, seq, d = x.shape
    n_lat = seq
    ctx_len = ctx.shape[1]
    depth = ada_w.shape[0]
    ta = n_lat + ctx_len
    assert b + 1 <= 8 and seq % GRID_W == 0 and ctx_len % CHUNK == 0 and seq % CHUNK == 0

    tm = _pick_tile(math.gcd(ctx_len, seq), (256, 128))
    tq = _pick_tile(seq, (512, 256, 128))
    tkk = _pick_tile(ta, (3328, 1280, 640, 256, 128))
    tm_moe = _pick_tile(ta, (1280, 640, 256, 128))

    xa = jnp.concatenate([x, ctx], axis=1)
    cvec = jnp.zeros((8, d), F32).at[:b].set(c).at[b].set(c_ctx)
    mod = _modulation(cvec, ada_w, ada_b)
    cos_t, sin_t = _rope_tables(seq, ctx_len)
    seg = jnp.asarray(np.kron(np.eye(RWKV_HEADS), np.ones((RWKV_HEAD, RWKV_HEAD))), BF16)
    n_in = w_in.shape[2]
    n_in_p = 3 * ATTN_WIDTH + 3 * RWKV_WIDTH + LORA_PAD

    v_first = None
    for l in range(depth):
        lam_init = 0.8 - 0.6 * math.exp(-0.3 * l)
        modb = mod[l, :b].reshape(b, 1, 6 * d)
        modc = mod[l, b:b + 1]
        w_in_p = jnp.pad(w_in[l], ((0, 0), (0, n_in_p - n_in))).astype(BF16)
        q, k, v, pr, pk, pv, lora = _in_proj(xa, norm1_g[l].reshape(1, d), modb, modc, w_in_p,
                                             cos_t, sin_t, n_lat, tm)
        sg = subln_g[l].reshape(ATTN_V_DIM, 1)
        att_lat = _attention(q, k, v, diff_lambda[l], sg, lam_init, 0, seq, 0, ta, tq, tkk)
        att_ctx = _attention(q, k, v, diff_lambda[l], sg, lam_init, seq, ctx_len, seq, ctx_len,
                             ctx_len, ctx_len)
        att = jnp.concatenate([att_lat, att_ctx], axis=2)
        vres = None if l == 0 else (v_first, rwkv_v0[l - 1], rwkv_v1[l - 1], rwkv_v2[l - 1])
        r, vv, kk, g, lw_f, lw_b, bt_f, bt_b, kd_f, kd_b = _rwkv_prepare(
            pr, pk, pv, lora, rwkv_mu[l], rwkv_w0[l], rwkv_w2[l], rwkv_a0[l], rwkv_a2[l], rwkv_g2[l],
            rwkv_k_k[l], rwkv_k_a[l], seg, vres, n_lat, tm)
        if l == 0:
            v_first = vv
        y_f, y_b = _rwkv_scan(r, vv, kk, lw_f, lw_b, bt_f, bt_b, kd_f, kd_b, n_lat)
        xa = _post_mix(xa, att, y_f, y_b, r, vv, kd_f, kd_b, g, modb, modc, w_out[l].astype(BF16),
                       lnx_w[l], lnx_b[l], rwkv_r_k[l], seg, n_lat, tm)
        i = l // 2
        if l % 2 == 0:
            xa = _dense_ffn(xa, norm2_g[l].reshape(1, d), modb, modc, ffn_w1[i].astype(BF16),
                            ffn_w3[i].astype(BF16), ffn_w2[i].astype(BF16), n_lat, tm)
        else:
            xa = _moe_ffn(xa, norm2_g[l].reshape(1, d), modb, modc, router_w[i],
                          exp_w1[i].astype(BF16), exp_w3[i].astype(BF16), exp_w2[i].astype(BF16),
                          n_lat, tm_moe, 2)
    return _final_norm(xa, final_g.reshape(1, d), seq, tm)
```

```python
import functools
import itertools
import math

import numpy as np
import jax
import jax.numpy as jnp
from jax import lax
from jax.experimental import pallas as pl
from jax.experimental.pallas import tpu as pltpu

F32 = jnp.float32
BF16 = jnp.bfloat16

ATTN_HEADS = 4
ATTN_QK_DIM = 64
ATTN_V_DIM = 128
ATTN_WIDTH = ATTN_HEADS * ATTN_V_DIM
RWKV_HEADS = 8
RWKV_HEAD = 64
RWKV_WIDTH = RWKV_HEADS * RWKV_HEAD
DECAY_LORA = 32
AAA_LORA = 32
GATE_LORA = 96
LORA_PAD = 256
N_EXPERTS = 8
GRID_W = 64
ROPE_THETA = 10000.0
NORM_EPS = 1e-6
SUBLN_EPS = 1e-5
LNX_EPS = 64e-5
LANES = 128
CHUNK = 64
INV_BLOCK = 16
MOE_SLOT_BLOCK = 128
SCAN_CHUNKS = 2
V_ONES = 16
V_EXT = ATTN_V_DIM + V_ONES
VMEM_LIMIT = 56 * 1024 * 1024
Q_SCALE = ATTN_QK_DIM ** -0.5 * math.log2(math.e)


def _cparams(sem):
    return pltpu.CompilerParams(dimension_semantics=sem, vmem_limit_bytes=VMEM_LIMIT)


def _bdot(a, b):
    return jnp.dot(a.astype(BF16), b.astype(BF16), preferred_element_type=F32)


def _split(a):
    hi = a.astype(BF16)
    lo = (a - hi.astype(F32)).astype(BF16)
    return hi, lo


def _dot_exact_rhs(a, b_bf16):
    hi, lo = _split(a)
    return (jnp.dot(hi, b_bf16, preferred_element_type=F32)
            + jnp.dot(lo, b_bf16, preferred_element_type=F32))


def _dot_exact_lhs(a_bf16, b):
    hi, lo = _split(b)
    return (jnp.dot(a_bf16, hi, preferred_element_type=F32)
            + jnp.dot(a_bf16, lo, preferred_element_type=F32))


def _modnorm(x, g, sc_b, sh_b, sc_c, sh_c, is_ctx):
    ms = jnp.mean(x * x, axis=-1, keepdims=True)
    y = x * lax.rsqrt(ms + NORM_EPS) * g
    sc = jnp.where(is_ctx, sc_c, sc_b)
    sh = jnp.where(is_ctx, sh_c, sh_b)
    return y * (1.0 + sc) + sh


def _is_ctx_rows(tile_idx, tm, n_lat):
    row = tile_idx * tm + lax.broadcasted_iota(jnp.int32, (tm, 1), 0)
    return row >= n_lat


def _mod_specs(d, k_sh, k_sc):
    return [
        pl.BlockSpec((1, 1, d), lambda b, i, *_: (b, 0, k_sh)),
        pl.BlockSpec((1, 1, d), lambda b, i, *_: (b, 0, k_sc)),
        pl.BlockSpec((1, d), lambda b, i, *_: (0, k_sh)),
        pl.BlockSpec((1, d), lambda b, i, *_: (0, k_sc)),
    ]


def _mod_kernel(c_ref, w_ref, b_ref, o_ref):
    c = c_ref[...]
    s = c * jax.nn.sigmoid(c)
    o_ref[0] = _bdot(s, w_ref[0]) + b_ref[0]


def _modulation(cvec, ada_w, ada_b):
    depth, d, n = ada_w.shape
    tn = 2048
    return pl.pallas_call(
        _mod_kernel,
        grid=(depth, n // tn),
        in_specs=[pl.BlockSpec((8, d), lambda l, j: (0, 0)),
                  pl.BlockSpec((1, d, tn), lambda l, j: (l, 0, j)),
                  pl.BlockSpec((1, 1, tn), lambda l, j: (l, 0, j))],
        out_specs=pl.BlockSpec((1, 8, tn), lambda l, j: (l, 0, j)),
        out_shape=jax.ShapeDtypeStruct((depth, 8, n), F32),
        compiler_params=_cparams(("parallel", "parallel")),
        name="adaln_mod",
    )(cvec, ada_w, ada_b.reshape(depth, 1, n))


def _inproj_kernel(x_ref, g_ref, bsh_ref, bsc_ref, csh_ref, csc_ref, w_ref, cos_ref, sin_ref,
                   q_ref, k_ref, v_ref, rr_ref, rk_ref, rv_ref, lo_ref, *, tm, n_lat):
    i = pl.program_id(1)
    is_ctx = _is_ctx_rows(i, tm, n_lat)
    h = _modnorm(x_ref[0], g_ref[...], bsc_ref[0], bsh_ref[0], csc_ref[...], csh_ref[...], is_ctx)
    p = jnp.dot(h.astype(BF16), w_ref[...], preferred_element_type=F32)
    cos_t = cos_ref[...]
    sin_t = sin_ref[...]
    lane = lax.broadcasted_iota(jnp.int32, (tm, LANES), 1)
    first = (lane & 31) < 16

    def rope(z):
        partner = jnp.where(first, pltpu.roll(z, LANES - 16, 1), pltpu.roll(z, 16, 1))
        return z * cos_t + partner * sin_t

    for hh in range(ATTN_HEADS):
        lo = LANES * hh
        q_ref[0, lo:lo + LANES, :] = jnp.transpose(rope(p[:, lo:lo + LANES]) * Q_SCALE).astype(BF16)
        k_ref[0, :, lo:lo + LANES] = rope(p[:, ATTN_WIDTH + lo:ATTN_WIDTH + lo + LANES]).astype(BF16)
        vo = 2 * ATTN_WIDTH + lo
        ve = V_EXT * hh
        v_ref[0, ve:ve + ATTN_V_DIM, :] = jnp.transpose(p[:, vo:vo + LANES]).astype(BF16)
        v_ref[0, ve + ATTN_V_DIM:ve + V_EXT, :] = jnp.ones((V_ONES, tm), BF16)
    o = 3 * ATTN_WIDTH
    rr_ref[0] = p[:, o:o + RWKV_WIDTH]
    rk_ref[0] = p[:, o + RWKV_WIDTH:o + 2 * RWKV_WIDTH]
    rv_ref[0] = p[:, o + 2 * RWKV_WIDTH:o + 3 * RWKV_WIDTH]
    o += 3 * RWKV_WIDTH
    lo_ref[0] = p[:, o:o + LORA_PAD]


def _in_proj(xa, g1, modb, modc, w_in_p, cos_t, sin_t, n_lat, tm):
    b, ta, d = xa.shape
    n = w_in_p.shape[1]
    tok = lambda w: pl.BlockSpec((1, tm, w), lambda bb, i: (bb, i, 0))
    feat = pl.BlockSpec((1, ATTN_WIDTH, tm), lambda bb, i: (bb, 0, i))
    feat_shape = jax.ShapeDtypeStruct((b, ATTN_WIDTH, ta), BF16)
    vext = pl.BlockSpec((1, ATTN_HEADS * V_EXT, tm), lambda bb, i: (bb, 0, i))
    vext_shape = jax.ShapeDtypeStruct((b, ATTN_HEADS * V_EXT, ta), BF16)
    outs = [feat_shape, jax.ShapeDtypeStruct((b, ta, ATTN_WIDTH), BF16), vext_shape] \
        + [jax.ShapeDtypeStruct((b, ta, RWKV_WIDTH), F32)] * 3 \
        + [jax.ShapeDtypeStruct((b, ta, LORA_PAD), F32)]
    return pl.pallas_call(
        functools.partial(_inproj_kernel, tm=tm, n_lat=n_lat),
        grid=(b, ta // tm),
        in_specs=[tok(d), pl.BlockSpec((1, d), lambda bb, i: (0, 0))] + _mod_specs(d, 0, 1)
        + [pl.BlockSpec((d, n), lambda bb, i: (0, 0)),
           pl.BlockSpec((tm, LANES), lambda bb, i: (i, 0)),
           pl.BlockSpec((tm, LANES), lambda bb, i: (i, 0))],
        out_specs=[feat, tok(ATTN_WIDTH), vext] + [tok(RWKV_WIDTH)] * 3 + [tok(LORA_PAD)],
        out_shape=outs,
        compiler_params=_cparams(("parallel", "parallel")),
        name="norm1_in_proj",
    )(xa, g1, modb, modb, modc, modc, w_in_p, cos_t, sin_t)


def _attn_kernel(lam_ref, sg_ref, q_ref, k_ref, v_ref, o_ref, s_ref,
                 *, tq, k0, nkeys, tkk, ts, lam_init):
    q = q_ref[0]
    feat = lax.broadcasted_iota(jnp.int32, (LANES, tq), 0)
    zero = jnp.zeros_like(q)
    qs = jnp.concatenate([jnp.where(feat < ATTN_QK_DIM, q, zero),
                          jnp.where(feat >= ATTN_QK_DIM, q, zero)], axis=1)
    n_tiles = nkeys // tkk
    n_sub = tkk // ts

    def scores(t):
        mx = None
        for j in range(n_sub):
            r0 = k0 + t * tkk + j * ts
            s_j = jnp.dot(k_ref[0, r0:r0 + ts, :], qs, preferred_element_type=F32)
            s_ref[t % 2, j * ts:(j + 1) * ts, :] = s_j
            cm = jnp.max(s_j, axis=0, keepdims=True)
            mx = cm if mx is None else jnp.maximum(mx, cm)
        return mx

    m_run = jnp.full((1, 2 * tq), -jnp.inf, F32)
    acc = jnp.zeros((V_EXT, 2 * tq), F32)
    mx_next = scores(0)
    for t in range(n_tiles):
        mx_cur = mx_next
        if t + 1 < n_tiles:
            mx_next = scores(t + 1)
        m_new = jnp.maximum(m_run, mx_cur)
        alpha = jnp.exp2(m_run - m_new)
        pv = None
        for j in range(n_sub):
            p_j = jnp.exp2(s_ref[t % 2, j * ts:(j + 1) * ts, :] - m_new).astype(BF16)
            c0 = k0 + t * tkk + j * ts
            d = jnp.dot(v_ref[0, :, c0:c0 + ts], p_j, preferred_element_type=F32)
            pv = d if pv is None else pv + d
        acc = alpha * acc + pv
        m_run = m_new

    lp = lam_ref[...]
    lam = (jnp.exp(jnp.sum(lp[0:1] * lp[1:2], axis=1, keepdims=True))
           - jnp.exp(jnp.sum(lp[2:3] * lp[3:4], axis=1, keepdims=True)) + lam_init)
    l = acc[ATTN_V_DIM:ATTN_V_DIM + 1, :]
    o1 = acc[0:ATTN_V_DIM, 0:tq] / l[:, 0:tq]
    o2 = acc[0:ATTN_V_DIM, tq:2 * tq] / l[:, tq:2 * tq]
    o = o1 - lam * o2
    ms = jnp.mean(o * o, axis=0, keepdims=True)
    y = o * lax.rsqrt(ms + SUBLN_EPS) * sg_ref[...]
    o_ref[0] = (y * (1.0 - lam_init)).astype(BF16)


def _attention(q_t, k, v_t, lam_p, subln_g, lam_init, q0, nq, k0, nkeys, tq, tkk):
    b, ta, _ = k.shape
    qo = q0 // tq
    return pl.pallas_call(
        functools.partial(_attn_kernel, tq=tq, k0=k0, nkeys=nkeys, tkk=tkk, ts=_pick_tile(tkk, (256, LANES)),
                          lam_init=lam_init),
        grid=(b, ATTN_HEADS, nq // tq),
        in_specs=[pl.BlockSpec((4, ATTN_QK_DIM), lambda bb, h, i: (0, 0)),
                  pl.BlockSpec((ATTN_V_DIM, 1), lambda bb, h, i: (0, 0)),
                  pl.BlockSpec((1, LANES, tq), lambda bb, h, i: (bb, h, i + qo)),
                  pl.BlockSpec((1, ta, LANES), lambda bb, h, i: (bb, 0, h)),
                  pl.BlockSpec((1, V_EXT, ta), lambda bb, h, i: (bb, h, 0))],
        out_specs=pl.BlockSpec((1, ATTN_V_DIM, tq), lambda bb, h, i: (bb, h, i)),
        out_shape=jax.ShapeDtypeStruct((b, ATTN_WIDTH, nq), BF16),
        scratch_shapes=[pltpu.VMEM((2, tkk, 2 * tq), F32)],
        compiler_params=_cparams(("parallel", "parallel", "parallel")),
        name="diff_attention",
    )(lam_p, subln_g, q_t, k, v_t)


def _prep_kernel(*refs, tt, n_lat, ta, has_vres):
    (pr_ref, pk_ref, pv_ref, hpr_ref, hpk_ref, hpv_ref, hnr_ref, hnk_ref, hnv_ref, lo_ref,
     mu_ref, w0_ref, w2_ref, a0_ref, a2_ref, g2_ref, kk_ref, ka_ref, seg_ref) = refs[:19]
    n_in = 19
    if has_vres:
        vf_ref, v0_ref, v1_ref, v2_ref = refs[19:23]
        n_in = 23
    (r_out, v_out, kk_out, g_out, lwf_out, lwb_out, bf_out, bb_out, kdf_out, kdb_out) = refs[n_in:]

    i = pl.program_id(1)
    row = lax.broadcasted_iota(jnp.int32, (tt, 1), 0)
    gidx = i * tt + row
    no_prev = jnp.logical_or(gidx == 0, gidx == n_lat)
    no_next = jnp.logical_or(gidx == n_lat - 1, gidx == ta - 1)

    def shift_mix(p_ref, hp_ref, hn_ref, j):
        p = p_ref[0]
        pp = jnp.where(row == 0, hp_ref[0, 7:8, :], pltpu.roll(p, 1, 0))
        pp = jnp.where(no_prev, 0.0, pp)
        pn = jnp.where(row == tt - 1, hn_ref[0, 0:1, :], pltpu.roll(p, tt - 1, 0))
        pn = jnp.where(no_next, 0.0, pn)
        return p + mu_ref[2 * j:2 * j + 1, :] * (pp - p) + mu_ref[2 * j + 1:2 * j + 2, :] * (pn - p)

    r = shift_mix(pr_ref, hpr_ref, hnr_ref, 0)
    k = shift_mix(pk_ref, hpk_ref, hnk_ref, 1)
    v = shift_mix(pv_ref, hpv_ref, hnv_ref, 2)
    if has_vres:
        mix = jax.nn.sigmoid(v0_ref[...] + _bdot(_bdot(v, v1_ref[...]), v2_ref[...]))
        v = v + (vf_ref[0] - v) * mix
    lo = lo_ref[0]
    gd = lo[:, 2 * DECAY_LORA + 2 * AAA_LORA:2 * DECAY_LORA + 2 * AAA_LORA + GATE_LORA]
    g_out[0] = _bdot(jax.nn.sigmoid(gd), g2_ref[...])
    kk = k * kk_ref[...]
    ss = _dot_exact_rhs(kk * kk, seg_ref[...])
    kk = kk * lax.rsqrt(jnp.maximum(ss, 1e-24))
    r_out[0] = r
    v_out[0] = v
    kk_out[0] = kk
    lw_outs = (lwf_out, lwb_out)
    b_outs = (bf_out, bb_out)
    kd_outs = (kdf_out, kdb_out)
    for d in range(2):
        wd = lo[:, d * DECAY_LORA:(d + 1) * DECAY_LORA]
        wl = w0_ref[d:d + 1, :] + _bdot(jnp.tanh(wd), w2_ref[d])
        lw_outs[d][0] = -jax.nn.sigmoid(wl) * math.exp(-0.5)
        ad = lo[:, 2 * DECAY_LORA + d * AAA_LORA:2 * DECAY_LORA + (d + 1) * AAA_LORA]
        a = jax.nn.sigmoid(a0_ref[d:d + 1, :] + _bdot(ad, a2_ref[d]))
        b_outs[d][0] = a * kk
        kd_outs[d][0] = k * (1.0 + (a - 1.0) * ka_ref[...])


def _rwkv_prepare(pr, pk, pv, lora, mu, w0, w2, a0, a2, g2, k_k, k_a, seg, vres, n_lat, tt):
    b, ta, w = pr.shape
    nb8 = ta // 8
    tok = pl.BlockSpec((1, tt, w), lambda bb, i: (bb, i, 0))
    prev = pl.BlockSpec((1, 8, w), lambda bb, i: (bb, jnp.maximum(i * (tt // 8) - 1, 0), 0))
    nxt = pl.BlockSpec((1, 8, w), lambda bb, i: (bb, jnp.minimum((i + 1) * (tt // 8), nb8 - 1), 0))
    full = lambda a: pl.BlockSpec(a.shape, lambda bb, i, _n=a.ndim: (0,) * _n)
    consts = [mu.reshape(6, w), w0, w2, a0, a2, g2, k_k.reshape(1, w), k_a.reshape(1, w), seg]
    ins = [pr, pk, pv, pr, pk, pv, pr, pk, pv, lora] + consts
    specs = [tok] * 3 + [prev] * 3 + [nxt] * 3 \
        + [pl.BlockSpec((1, tt, LORA_PAD), lambda bb, i: (bb, i, 0))] + [full(a) for a in consts]
    if vres is not None:
        v_first, v0, v1, v2 = vres
        extra = [v0.reshape(1, w), v1, v2]
        ins += [v_first] + extra
        specs += [tok] + [full(a) for a in extra]
    return pl.pallas_call(
        functools.partial(_prep_kernel, tt=tt, n_lat=n_lat, ta=ta, has_vres=vres is not None),
        grid=(b, ta // tt),
        in_specs=specs,
        out_specs=[tok] * 10,
        out_shape=[jax.ShapeDtypeStruct((b, ta, w), F32)] * 10,
        compiler_params=_cparams(("parallel", "parallel")),
        name="rwkv_prepare",
    )(*ins)


_NN = (((2,), (1,)), ((0,), (0,)))
_NT = (((2,), (2,)), ((0,), (0,)))
_TN = (((1,), (1,)), ((0,), (0,)))


def _bmm(a, b, dims=_NN):
    return lax.dot_general(a.astype(BF16), b.astype(BF16), dims, preferred_element_type=F32)


def _scan_kernel(rf_ref, vf_ref, kkf_ref, lwf_ref, btf_ref, kdf_ref,
                 rb_ref, vb_ref, kkb_ref, lwb_ref, btb_ref, kdb_ref,
                 yf_ref, yb_ref, st_ref, *, nsub, nb):
    s = pl.program_id(0)
    c = CHUNK
    n = RWKV_HEAD
    g = nb * 2 * RWKV_HEADS

    @pl.when(s == 0)
    def _init():
        st_ref[...] = jnp.zeros(st_ref.shape, F32)

    t_i = lax.broadcasted_iota(jnp.int32, (c, c), 0)
    j_i = lax.broadcasted_iota(jnp.int32, (c, c), 1)
    t2 = lax.broadcasted_iota(jnp.int32, (2 * c, 2 * c), 0)
    j2 = lax.broadcasted_iota(jnp.int32, (2 * c, 2 * c), 1)
    tt2 = t2 & (c - 1)
    jj2 = j2 & (c - 1)
    eye = jnp.where(t_i == j_i, 1.0, 0.0)
    blk_bits = int(math.log2(INV_BLOCK))
    same_blk = (t_i >> blk_bits) == (j_i >> blk_bits)
    heads = lambda z: [z[:, h * n:(h + 1) * n] for h in range(RWKV_HEADS)]
    dirs = ((rf_ref, vf_ref, kkf_ref, lwf_ref, btf_ref, kdf_ref, False),
            (rb_ref, vb_ref, kkb_ref, lwb_ref, btb_ref, kdb_ref, True))
    at_l, rt_l, v_l, bk_l, pt_l, m_l = [], [], [], [], [], []
    for t in range(nsub):
        for bb, (r_ref, v_ref, kk_ref, lw_ref, bt_ref, kd_ref, rev) in itertools.product(range(nb), dirs):
            ci = nsub - 1 - t if rev else t
            rows = slice(ci * c, (ci + 1) * c)
            incl = (j_i >= t_i) if rev else (j_i <= t_i)
            tri = jnp.where(incl, 1.0, 0.0).astype(BF16)
            before = (jj2 > tt2) if rev else (jj2 < tt2)
            mask_m = jnp.logical_or(before, jnp.logical_and(t2 >= c, jj2 == tt2))
            lw = lw_ref[bb, rows, :]
            cs = _dot_exact_lhs(tri, lw)
            tot = cs[0:1, :] if rev else cs[c - 1:c, :]
            p_inv = jnp.exp(-cs)
            p_rest = jnp.exp(tot - cs)
            bt = bt_ref[bb, rows, :]
            kd = kd_ref[bb, rows, :]
            rt = heads(r_ref[bb, rows, :] * jnp.exp(cs))
            at = heads(-kk_ref[bb, rows, :] * jnp.exp(cs - lw))
            bti = heads(bt * p_inv)
            kti = heads(kd * p_inv)
            bp = heads(bt * p_rest)
            kp = heads(kd * p_rest)
            x1 = jnp.stack([jnp.concatenate([a, b], axis=0) for a, b in zip(at, rt)])
            x2 = jnp.stack([jnp.concatenate([a, b], axis=0) for a, b in zip(bti, kti)])
            m_l.append(jnp.where(mask_m[None], _bmm(x1, x2, _NT), 0.0))
            at_l += at
            rt_l += rt
            v_l += heads(v_ref[bb, rows, :])
            bk_l += [jnp.concatenate([a, b], axis=0) for a, b in zip(bp, kp)]
            pt_l += heads(jnp.exp(tot))
    m = jnp.concatenate(m_l, axis=0)
    at_s = jnp.stack(at_l)
    rt_s = jnp.stack(rt_l)
    v_s = jnp.stack(v_l)
    bk_s = jnp.stack(bk_l)
    pt_s = jnp.stack(pt_l)

    a_full = m[:, 0:c, 0:c]
    d_pow = jnp.where(same_blk[None], a_full, 0.0)
    l_off = a_full - d_pow
    t_diag = eye[None] + d_pow
    for _ in range(blk_bits - 1):
        d_pow = _bmm(d_pow, d_pow)
        t_diag = t_diag + _bmm(d_pow, t_diag)
    n_mat = _bmm(t_diag, l_off)
    x_mat = t_diag + _bmm(n_mat, t_diag)
    tinv = x_mat + _bmm(_bmm(n_mat, n_mat), x_mat)

    akv = _bmm(m[:, 0:c, c:2 * c], v_s)

    st = st_ref[...]
    for t in range(nsub):
        sl = slice(t * g, (t + 1) * g)
        u = _bmm(tinv[sl], akv[sl] + _bmm(at_s[sl], st, _NT))
        uv = jnp.concatenate([u, v_s[sl]], axis=1)
        y = _bmm(m[sl, c:2 * c, :], uv) + _bmm(rt_s[sl], st, _NT)
        st = pt_s[sl] * st + _bmm(uv, bk_s[sl], _TN)
        rf = slice(t * c, (t + 1) * c)
        rb = slice((nsub - 1 - t) * c, (nsub - t) * c)
        for bb, h in itertools.product(range(nb), range(RWKV_HEADS)):
            yf_ref[bb, rf, h * n:(h + 1) * n] = y[bb * 2 * RWKV_HEADS + h]
            yb_ref[bb, rb, h * n:(h + 1) * n] = y[bb * 2 * RWKV_HEADS + RWKV_HEADS + h]
    st_ref[...] = st


def _rwkv_scan(r, v, kk, lw_f, lw_b, bt_f, bt_b, kd_f, kd_b, n_lat):
    b, ta, w = r.shape
    nsub = SCAN_CHUNKS
    rows = nsub * CHUNK
    assert n_lat % rows == 0 and ta % rows == 0
    nc = ta // rows
    nlat = n_lat // rows
    nctx = nc - nlat

    def fwd_idx(s):
        return jnp.where(s < nctx, nlat + s, s - nctx)

    def rev_idx(s):
        return nc - 1 - s

    fwd = pl.BlockSpec((b, rows, w), lambda s: (0, fwd_idx(s), 0))
    bwd = pl.BlockSpec((b, rows, w), lambda s: (0, rev_idx(s), 0))
    return pl.pallas_call(
        functools.partial(_scan_kernel, nsub=nsub, nb=b),
        grid=(nc,),
        in_specs=[fwd] * 6 + [bwd] * 6,
        out_specs=[fwd, bwd],
        out_shape=[jax.ShapeDtypeStruct((b, ta, w), F32)] * 2,
        scratch_shapes=[pltpu.VMEM((b * 2 * RWKV_HEADS, RWKV_HEAD, RWKV_HEAD), F32)],
        compiler_params=_cparams(("arbitrary",)),
        name="rwkv_scan",
    )(r, v, kk, lw_f, bt_f, kd_f, r, v, kk, lw_b, bt_b, kd_b)


def _postmix_kernel(x_ref, att_ref, yf_ref, yb_ref, r_ref, v_ref, kdf_ref, kdb_ref, g_ref,
                    bgt_ref, cgt_ref, wo_ref, lnw_ref, lnb_ref, rk_ref, seg_ref, o_ref, *, tm, n_lat):
    i = pl.program_id(1)
    is_ctx = _is_ctx_rows(i, tm, n_lat)
    seg = seg_ref[...]
    y = yf_ref[0] + yb_ref[0]
    inv_n = 1.0 / RWKV_HEAD
    mean = _dot_exact_rhs(y, seg) * inv_n
    dy = y - mean
    var = _dot_exact_rhs(dy * dy, seg) * inv_n
    o = dy * lax.rsqrt(var + LNX_EPS) * lnw_ref[...] + lnb_ref[...]
    r = r_ref[0]
    v = v_ref[0]
    rrk = r * rk_ref[...]
    o = o + _dot_exact_rhs(rrk * kdf_ref[0], seg) * v
    o = o + _dot_exact_rhs(rrk * kdb_ref[0], seg) * v
    rw = o * g_ref[0]
    mixed = (lax.dot_general(att_ref[0], wo_ref[0:ATTN_WIDTH, :], (((0,), (0,)), ((), ())),
                             preferred_element_type=F32)
             + jnp.dot(rw.astype(BF16), wo_ref[ATTN_WIDTH:, :], preferred_element_type=F32))
    gt = jnp.where(is_ctx, cgt_ref[...], bgt_ref[0])
    o_ref[0] = x_ref[0] + gt * mixed


def _post_mix(xa, att, y_f, y_b, r, v, kd_f, kd_b, g, modb, modc, w_out, lnw, lnb, r_k, seg, n_lat, tm):
    b, ta, d = xa.shape
    w = RWKV_WIDTH
    tok = lambda ww: pl.BlockSpec((1, tm, ww), lambda bb, i: (bb, i, 0))
    full = lambda a: pl.BlockSpec(a.shape, lambda bb, i, _n=a.ndim: (0,) * _n)
    consts = [w_out, lnw.reshape(1, w), lnb.reshape(1, w), r_k.reshape(1, w), seg]
    return pl.pallas_call(
        functools.partial(_postmix_kernel, tm=tm, n_lat=n_lat),
        grid=(b, ta // tm),
        in_specs=[tok(d), pl.BlockSpec((1, ATTN_WIDTH, tm), lambda bb, i: (bb, 0, i))] + [tok(w)] * 7
        + [pl.BlockSpec((1, 1, d), lambda bb, i: (bb, 0, 2)), pl.BlockSpec((1, d), lambda bb, i: (0, 2))]
        + [full(a) for a in consts],
        out_specs=tok(d),
        out_shape=jax.ShapeDtypeStruct((b, ta, d), F32),
        compiler_params=_cparams(("parallel", "parallel")),
        name="rwkv_finish_out_proj",
    )(xa, att, y_f, y_b, r, v, kd_f, kd_b, g, modb, modc, *consts)


def _ffn_kernel(x_ref, g_ref, bsh_ref, bsc_ref, csh_ref, csc_ref, bgt_ref, cgt_ref,
                w1_ref, w3_ref, w2_ref, o_ref, *, tm, n_lat):
    i = pl.program_id(1)
    is_ctx = _is_ctx_rows(i, tm, n_lat)
    x = x_ref[0]
    h = _modnorm(x, g_ref[...], bsc_ref[0], bsh_ref[0], csc_ref[...], csh_ref[...], is_ctx).astype(BF16)
    a = jnp.dot(h, w1_ref[...], preferred_element_type=F32)
    bb = jnp.dot(h, w3_ref[...], preferred_element_type=F32)
    act = (a * jax.nn.sigmoid(a) * bb).astype(BF16)
    out = jnp.dot(act, w2_ref[...], preferred_element_type=F32)
    gt = jnp.where(is_ctx, cgt_ref[...], bgt_ref[0])
    o_ref[0] = x + gt * out


def _dense_ffn(xa, g2, modb, modc, w1, w3, w2, n_lat, tm):
    b, ta, d = xa.shape
    tok = pl.BlockSpec((1, tm, d), lambda bb, i: (bb, i, 0))
    full = lambda a: pl.BlockSpec(a.shape, lambda bb, i, _n=a.ndim: (0,) * _n,
                                  pipeline_mode=pl.Buffered(1))
    return pl.pallas_call(
        functools.partial(_ffn_kernel, tm=tm, n_lat=n_lat),
        grid=(b, ta // tm),
        in_specs=[tok, pl.BlockSpec((1, d), lambda bb, i: (0, 0))] + _mod_specs(d, 3, 4)
        + [pl.BlockSpec((1, 1, d), lambda bb, i: (bb, 0, 5)), pl.BlockSpec((1, d), lambda bb, i: (0, 5))]
        + [full(w1), full(w3), full(w2)],
        out_specs=tok,
        out_shape=jax.ShapeDtypeStruct((b, ta, d), F32),
        compiler_params=_cparams(("parallel", "parallel")),
        name="norm2_dense_swiglu",
    )(xa, g2, modb, modb, modc, modc, modb, modc, w1, w3, w2)


_NT2 = (((1,), (1,)), ((), ()))
_TN2 = (((0,), (0,)), ((), ()))


def _route_kernel(x_ref, g_ref, bsh_ref, bsc_ref, csh_ref, csc_ref, rw_ref, tri_ref,
                  h_ref, gate_ref, pos_ref, cnt_ref, *, tm, n_lat):
    i = pl.program_id(1)
    is_ctx = _is_ctx_rows(i, tm, n_lat)
    h = _modnorm(x_ref[0], g_ref[...], bsc_ref[0], bsh_ref[0], csc_ref[...], csh_ref[...], is_ctx)
    h_ref[0] = h.astype(BF16)
    hh, hl = _split(h)
    wh, wl = _split(rw_ref[...])
    dg = lambda a, b: lax.dot_general(a, b, _NT2, preferred_element_type=F32)
    logits = dg(wh, hh) + dg(wh, hl) + dg(wl, hh)
    eidx = lax.broadcasted_iota(jnp.int32, logits.shape, 0).astype(F32)
    neg = -jnp.inf
    v1 = jnp.max(logits, axis=0, keepdims=True)
    i1 = jnp.min(jnp.where(logits == v1, eidx, float(N_EXPERTS)), axis=0, keepdims=True)
    m1 = eidx == i1
    rest = jnp.where(m1, neg, logits)
    v2 = jnp.max(rest, axis=0, keepdims=True)
    i2 = jnp.min(jnp.where(rest == v2, eidx, float(N_EXPERTS)), axis=0, keepdims=True)
    m2 = eidx == i2
    e2 = jnp.exp(v2 - v1)
    den = 1.0 + e2
    gate_ref[0] = jnp.where(m1, 1.0 / den, 0.0) + jnp.where(m2, e2 / den, 0.0)
    sel = jnp.logical_or(m1, m2)
    self = jnp.where(sel, 1.0, 0.0)
    rank = jnp.dot(self.astype(BF16), tri_ref[...], preferred_element_type=F32)
    pos_ref[0] = jnp.where(sel, rank, -1.0)
    cnt = jnp.sum(self, axis=1, keepdims=True)
    cnt_ref[0, 0] = jnp.broadcast_to(cnt, (N_EXPERTS, LANES)).astype(jnp.int32)


def _expert_kernel(cnt_ref, x_ref, h_ref, gate_ref, pos_ref, bgt_ref, cgt_ref, w1_ref, w3_ref, w2_ref,
                   o_ref, xg_ref, gs_ref, og_ref, *, tm, n_lat, blk):
    bi = pl.program_id(0)
    ti = pl.program_id(1)
    e = pl.program_id(2)
    f = pl.program_id(3)
    nt = pl.num_programs(1)
    ne = pl.num_programs(2)
    nf = pl.num_programs(3)
    cnt = cnt_ref[(bi * nt + ti) * ne + e]
    nblk = (cnt + blk - 1) // blk
    pos_row = pos_ref[0, pl.ds(e, 1), :]
    gate_row = gate_ref[0, pl.ds(e, 1), :]
    slot = lax.broadcasted_iota(jnp.int32, (blk, tm), 0).astype(F32)

    def one_hot(j):
        return pos_row == slot + (j * blk).astype(F32)

    def rows_of(j):
        return pl.ds(pl.multiple_of(j * blk, blk), blk)

    nblk2 = (cnt + 2 * blk - 1) // (2 * blk)
    slot2 = lax.broadcasted_iota(jnp.int32, (2 * blk, tm), 0).astype(F32)

    def rows2_of(j):
        return pl.ds(pl.multiple_of(j * 2 * blk, 2 * blk), 2 * blk)

    @pl.when(jnp.logical_and(e == 0, f == 0))
    def _zero():
        o_ref[0] = jnp.zeros(o_ref.shape[1:], F32)

    @pl.when(f == 0)
    def _gather():
        def body(j, carry):
            hit = one_hot(j)
            g = jnp.where(hit, 1.0, 0.0).astype(BF16)
            xg_ref[rows_of(j), :] = jnp.dot(g, h_ref[0], preferred_element_type=F32).astype(BF16)
            gsl = jnp.sum(jnp.where(hit, gate_row, 0.0), axis=1, keepdims=True)
            gs_ref[rows_of(j), :] = jnp.broadcast_to(gsl, (blk, LANES))
            return carry
        lax.fori_loop(0, nblk, body, 0)

        def clear(j, carry):
            og_ref[rows2_of(j), :] = jnp.zeros((2 * blk, og_ref.shape[1]), F32)
            return carry
        lax.fori_loop(0, nblk2, clear, 0)

    def expert(j, carry):
        xb = xg_ref[rows_of(j), :]
        a = jnp.dot(xb, w1_ref[0], preferred_element_type=F32)
        bb = jnp.dot(xb, w3_ref[0], preferred_element_type=F32)
        act = (a * jax.nn.sigmoid(a) * bb * gs_ref[rows_of(j), 0:1]).astype(BF16)
        og_ref[rows_of(j), :] += jnp.dot(act, w2_ref[0], preferred_element_type=F32)
        return carry
    lax.fori_loop(0, nblk, expert, 0)

    @pl.when(f == nf - 1)
    def _scatter():
        d = o_ref.shape[2]
        def body(j, carry):
            hit = pos_row == slot2 + (j * 2 * blk).astype(F32)
            g = jnp.where(hit, 1.0, 0.0).astype(BF16)
            for c0 in range(0, d, d // 2):
                o = og_ref[rows2_of(j), c0:c0 + d // 2].astype(BF16)
                o_ref[0, :, c0:c0 + d // 2] += lax.dot_general(g, o, _TN2, preferred_element_type=F32)
            return carry
        lax.fori_loop(0, nblk2, body, 0)

    @pl.when(jnp.logical_and(e == ne - 1, f == nf - 1))
    def _finish():
        is_ctx = _is_ctx_rows(ti, tm, n_lat)
        gt = jnp.where(is_ctx, cgt_ref[...], bgt_ref[0])
        o_ref[0] = x_ref[0] + gt * o_ref[0]


def _moe_ffn(xa, g2, modb, modc, router_w, w1, w3, w2, n_lat, tm, nf):
    b, ta, d = xa.shape
    ne, _, ff = w1.shape
    tf = ff // nf
    nt = ta // tm
    assert tm % (2 * MOE_SLOT_BLOCK) == 0
    tok = pl.BlockSpec((1, tm, d), lambda bb, i, *_: (bb, i, 0))
    emaj = pl.BlockSpec((1, ne, tm), lambda bb, i, *_: (bb, 0, i))
    tri = jnp.triu(jnp.ones((tm, tm), BF16), 1)
    h, gate, pos, cnt = pl.pallas_call(
        functools.partial(_route_kernel, tm=tm, n_lat=n_lat),
        grid=(b, nt),
        in_specs=[tok, pl.BlockSpec((1, d), lambda bb, i: (0, 0))] + _mod_specs(d, 3, 4)
        + [pl.BlockSpec((ne, d), lambda bb, i: (0, 0)), pl.BlockSpec((tm, tm), lambda bb, i: (0, 0))],
        out_specs=[tok, emaj, emaj, pl.BlockSpec((1, 1, ne, LANES), lambda bb, i: (bb, i, 0, 0))],
        out_shape=[jax.ShapeDtypeStruct((b, ta, d), BF16), jax.ShapeDtypeStruct((b, ne, ta), F32),
                   jax.ShapeDtypeStruct((b, ne, ta), F32), jax.ShapeDtypeStruct((b, nt, ne, LANES), jnp.int32)],
        compiler_params=_cparams(("parallel", "parallel")),
        name="norm2_moe_route",
    )(xa, g2, modb, modb, modc, modc, router_w.T, tri)
    counts = cnt[:, :, :, 0].reshape(-1)
    once = dict(pipeline_mode=pl.Buffered(1))
    grid_spec = pltpu.PrefetchScalarGridSpec(
        num_scalar_prefetch=1,
        grid=(b, nt, ne, nf),
        in_specs=[pl.BlockSpec((1, tm, d), lambda bb, i, e, f, c: (bb, i, 0), **once),
                  pl.BlockSpec((1, tm, d), lambda bb, i, e, f, c: (bb, i, 0), **once),
                  emaj, emaj,
                  pl.BlockSpec((1, 1, d), lambda bb, i, e, f, c: (bb, 0, 5)),
                  pl.BlockSpec((1, d), lambda bb, i, e, f, c: (0, 5)),
                  pl.BlockSpec((1, d, tf), lambda bb, i, e, f, c: (e, 0, f)),
                  pl.BlockSpec((1, d, tf), lambda bb, i, e, f, c: (e, 0, f)),
                  pl.BlockSpec((1, tf, d), lambda bb, i, e, f, c: (e, f, 0))],
        out_specs=pl.BlockSpec((1, tm, d), lambda bb, i, e, f, c: (bb, i, 0)),
        scratch_shapes=[pltpu.VMEM((tm, d), BF16), pltpu.VMEM((tm, LANES), F32), pltpu.VMEM((tm, d), F32)])
    return pl.pallas_call(
        functools.partial(_expert_kernel, tm=tm, n_lat=n_lat, blk=MOE_SLOT_BLOCK),
        grid_spec=grid_spec,
        out_shape=jax.ShapeDtypeStruct((b, ta, d), F32),
        compiler_params=_cparams(("parallel", "parallel", "arbitrary", "arbitrary")),
        name="moe_experts",
    )(counts, xa, h, gate, pos, modb, modc, w1, w3, w2)


def _final_kernel(x_ref, g_ref, o_ref):
    x = x_ref[0]
    ms = jnp.mean(x * x, axis=-1, keepdims=True)
    o_ref[0] = x * lax.rsqrt(ms + NORM_EPS) * g_ref[...]


def _final_norm(xa, g, seq, tm):
    b, _, d = xa.shape
    return pl.pallas_call(
        _final_kernel,
        grid=(b, seq // tm),
        in_specs=[pl.BlockSpec((1, tm, d), lambda bb, i: (bb, i, 0)),
                  pl.BlockSpec((1, d), lambda bb, i: (0, 0))],
        out_specs=pl.BlockSpec((1, tm, d), lambda bb, i: (bb, i, 0)),
        out_shape=jax.ShapeDtypeStruct((b, seq, d), F32),
        compiler_params=_cparams(("parallel", "parallel")),
        name="final_norm",
    )(xa, g)


def _rope_tables(seq, ctx_len):
    axis_dim = ATTN_QK_DIM // 2
    inv_freq = ROPE_THETA ** (-jnp.arange(0, axis_dim, 2, dtype=F32) / axis_dim)
    t = jnp.arange(seq)
    pos = jnp.stack([(t // GRID_W).astype(F32), (t % GRID_W).astype(F32)], axis=1)
    lane = np.arange(LANES)
    axis = (lane % ATTN_QK_DIM) // axis_dim
    freq = lane % (axis_dim // 2)
    sign = np.where((lane % axis_dim) < axis_dim // 2, -1.0, 1.0).astype(np.float32)
    ang = pos[:, axis] * inv_freq[freq][None, :]
    cos_t = jnp.concatenate([jnp.cos(ang), jnp.ones((ctx_len, LANES), F32)], axis=0)
    sin_t = jnp.concatenate([jnp.sin(ang) * sign[None, :], jnp.zeros((ctx_len, LANES), F32)], axis=0)
    return cos_t, sin_t


def _pick_tile(n, candidates):
    for c in candidates:
        if n % c == 0:
            return c
    raise ValueError(f"no tile in {candidates} divides {n}")


def kernel(x, c, ctx, c_ctx, ada_w, ada_b, norm1_g, norm2_g, w_in, w_out, diff_lambda, subln_g, rwkv_mu, rwkv_w0, rwkv_w2, rwkv_a0, rwkv_a2, rwkv_v0, rwkv_v1, rwkv_v2, rwkv_g2, rwkv_k_k, rwkv_k_a, rwkv_r_k, lnx_w, lnx_b, ffn_w1, ffn_w3, ffn_w2, router_w, exp_w1, exp_w3, exp_w2, final_g):
    b, seq, d = x.shape
    n_lat = seq
    ctx_len = ctx.shape[1]
    depth = ada_w.shape[0]
    ta = n_lat + ctx_len
    assert b + 1 <= 8 and seq % GRID_W == 0 and ctx_len % CHUNK == 0 and seq % CHUNK == 0

    tq = _pick_tile(seq, (512, 256, 128))
    tkk = _pick_tile(ta, (3328, 1280, 640, 256, 128))
    tm_moe = _pick_tile(ta, (1280, 640, 256, 128))
    tm_row = _pick_tile(ta, (640, 256, 128))

    xa = jnp.concatenate([x, ctx], axis=1)
    cvec = jnp.zeros((8, d), F32).at[:b].set(c).at[b].set(c_ctx)
    mod = _modulation(cvec, ada_w, ada_b)
    cos_t, sin_t = _rope_tables(seq, ctx_len)
    seg = jnp.asarray(np.kron(np.eye(RWKV_HEADS), np.ones((RWKV_HEAD, RWKV_HEAD))), BF16)
    n_in = w_in.shape[2]
    n_in_p = 3 * ATTN_WIDTH + 3 * RWKV_WIDTH + LORA_PAD

    v_first = None
    for l in range(depth):
        lam_init = 0.8 - 0.6 * math.exp(-0.3 * l)
        modb = mod[l, :b].reshape(b, 1, 6 * d)
        modc = mod[l, b:b + 1]
        w_in_p = jnp.pad(w_in[l], ((0, 0), (0, n_in_p - n_in))).astype(BF16)
        q, k, v, pr, pk, pv, lora = _in_proj(xa, norm1_g[l].reshape(1, d), modb, modc, w_in_p,
                                             cos_t, sin_t, n_lat, tm_row)
        sg = subln_g[l].reshape(ATTN_V_DIM, 1)
        att_lat = _attention(q, k, v, diff_lambda[l], sg, lam_init, 0, seq, 0, ta, tq, tkk)
        att_ctx = _attention(q, k, v, diff_lambda[l], sg, lam_init, seq, ctx_len, seq, ctx_len,
                             ctx_len, ctx_len)
        att = jnp.concatenate([att_lat, att_ctx], axis=2)
        vres = None if l == 0 else (v_first, rwkv_v0[l - 1], rwkv_v1[l - 1], rwkv_v2[l - 1])
        r, vv, kk, g, lw_f, lw_b, bt_f, bt_b, kd_f, kd_b = _rwkv_prepare(
            pr, pk, pv, lora, rwkv_mu[l], rwkv_w0[l], rwkv_w2[l], rwkv_a0[l], rwkv_a2[l], rwkv_g2[l],
            rwkv_k_k[l], rwkv_k_a[l], seg, vres, n_lat, tm_row)
        if l == 0:
            v_first = vv
        y_f, y_b = _rwkv_scan(r, vv, kk, lw_f, lw_b, bt_f, bt_b, kd_f, kd_b, n_lat)
        xa = _post_mix(xa, att, y_f, y_b, r, vv, kd_f, kd_b, g, modb, modc, w_out[l].astype(BF16),
                       lnx_w[l], lnx_b[l], rwkv_r_k[l], seg, n_lat, tm_row)
        i = l // 2
        if l % 2 == 0:
            xa = _dense_ffn(xa, norm2_g[l].reshape(1, d), modb, modc, ffn_w1[i].astype(BF16),
                            ffn_w3[i].astype(BF16), ffn_w2[i].astype(BF16), n_lat, tm_row)
        else:
            xa = _moe_ffn(xa, norm2_g[l].reshape(1, d), modb, modc, router_w[i],
                          exp_w1[i].astype(BF16), exp_w3[i].astype(BF16), exp_w2[i].astype(BF16),
                          n_lat, tm_moe, 2)
    return _final_norm(xa, final_g.reshape(1, d), seq, _pick_tile(seq, (1024, 512, 256, 128)))
```

```python
import functools
import itertools
import math

import numpy as np
import jax
import jax.numpy as jnp
from jax import lax
from jax.experimental import pallas as pl
from jax.experimental.pallas import tpu as pltpu

F32 = jnp.float32
BF16 = jnp.bfloat16

ATTN_HEADS = 4
ATTN_QK_DIM = 64
ATTN_V_DIM = 128
ATTN_WIDTH = ATTN_HEADS * ATTN_V_DIM
RWKV_HEADS = 8
RWKV_HEAD = 64
RWKV_WIDTH = RWKV_HEADS * RWKV_HEAD
DECAY_LORA = 32
AAA_LORA = 32
GATE_LORA = 96
LORA_PAD = 256
N_EXPERTS = 8
GRID_W = 64
ROPE_THETA = 10000.0
NORM_EPS = 1e-6
SUBLN_EPS = 1e-5
LNX_EPS = 64e-5
LANES = 128
CHUNK = 64
INV_BLOCK = 16
MOE_SLOT_BLOCK = 128
SCAN_CHUNKS = 2
V_ONES = 16
V_EXT = ATTN_V_DIM + V_ONES
VMEM_LIMIT = 56 * 1024 * 1024
Q_SCALE = ATTN_QK_DIM ** -0.5 * math.log2(math.e)


def _cparams(sem):
    return pltpu.CompilerParams(dimension_semantics=sem, vmem_limit_bytes=VMEM_LIMIT)


def _bdot(a, b):
    return jnp.dot(a.astype(BF16), b.astype(BF16), preferred_element_type=F32)


def _split(a):
    hi = a.astype(BF16)
    lo = (a - hi.astype(F32)).astype(BF16)
    return hi, lo


def _dot_exact_rhs(a, b_bf16):
    hi, lo = _split(a)
    return (jnp.dot(hi, b_bf16, preferred_element_type=F32)
            + jnp.dot(lo, b_bf16, preferred_element_type=F32))


def _dot_exact_lhs(a_bf16, b):
    hi, lo = _split(b)
    return (jnp.dot(a_bf16, hi, preferred_element_type=F32)
            + jnp.dot(a_bf16, lo, preferred_element_type=F32))


def _modnorm(x, g, sc_b, sh_b, sc_c, sh_c, is_ctx):
    ms = jnp.mean(x * x, axis=-1, keepdims=True)
    y = x * lax.rsqrt(ms + NORM_EPS) * g
    sc = jnp.where(is_ctx, sc_c, sc_b)
    sh = jnp.where(is_ctx, sh_c, sh_b)
    return y * (1.0 + sc) + sh


def _is_ctx_rows(tile_idx, tm, n_lat):
    row = tile_idx * tm + lax.broadcasted_iota(jnp.int32, (tm, 1), 0)
    return row >= n_lat


def _mod_specs(d, k_sh, k_sc):
    return [
        pl.BlockSpec((1, 1, d), lambda b, i, *_: (b, 0, k_sh)),
        pl.BlockSpec((1, 1, d), lambda b, i, *_: (b, 0, k_sc)),
        pl.BlockSpec((1, d), lambda b, i, *_: (0, k_sh)),
        pl.BlockSpec((1, d), lambda b, i, *_: (0, k_sc)),
    ]


def _mod_kernel(c_ref, w_ref, b_ref, o_ref):
    c = c_ref[...]
    s = c * jax.nn.sigmoid(c)
    o_ref[0] = _bdot(s, w_ref[0]) + b_ref[0]


def _modulation(cvec, ada_w, ada_b):
    depth, d, n = ada_w.shape
    tn = 2048
    return pl.pallas_call(
        _mod_kernel,
        grid=(depth, n // tn),
        in_specs=[pl.BlockSpec((8, d), lambda l, j: (0, 0)),
                  pl.BlockSpec((1, d, tn), lambda l, j: (l, 0, j)),
                  pl.BlockSpec((1, 1, tn), lambda l, j: (l, 0, j))],
        out_specs=pl.BlockSpec((1, 8, tn), lambda l, j: (l, 0, j)),
        out_shape=jax.ShapeDtypeStruct((depth, 8, n), F32),
        compiler_params=_cparams(("parallel", "parallel")),
        name="adaln_mod",
    )(cvec, ada_w, ada_b.reshape(depth, 1, n))


def _inproj_kernel(x_ref, g_ref, bsh_ref, bsc_ref, csh_ref, csc_ref, w_ref, cos_ref, sin_ref,
                   q_ref, k_ref, v_ref, rr_ref, rk_ref, rv_ref, lo_ref, *, tm, n_lat):
    i = pl.program_id(1)
    is_ctx = _is_ctx_rows(i, tm, n_lat)
    h = _modnorm(x_ref[0], g_ref[...], bsc_ref[0], bsh_ref[0], csc_ref[...], csh_ref[...], is_ctx)
    p = jnp.dot(h.astype(BF16), w_ref[...], preferred_element_type=F32)
    cos_t = cos_ref[...]
    sin_t = sin_ref[...]
    lane = lax.broadcasted_iota(jnp.int32, (tm, LANES), 1)
    first = (lane & 31) < 16

    def rope(z):
        partner = jnp.where(first, pltpu.roll(z, LANES - 16, 1), pltpu.roll(z, 16, 1))
        return z * cos_t + partner * sin_t

    for hh in range(ATTN_HEADS):
        lo = LANES * hh
        q_ref[0, lo:lo + LANES, :] = jnp.transpose(rope(p[:, lo:lo + LANES]) * Q_SCALE).astype(BF16)
        k_ref[0, :, lo:lo + LANES] = rope(p[:, ATTN_WIDTH + lo:ATTN_WIDTH + lo + LANES]).astype(BF16)
        vo = 2 * ATTN_WIDTH + lo
        ve = V_EXT * hh
        v_ref[0, ve:ve + ATTN_V_DIM, :] = jnp.transpose(p[:, vo:vo + LANES]).astype(BF16)
        v_ref[0, ve + ATTN_V_DIM:ve + V_EXT, :] = jnp.ones((V_ONES, tm), BF16)
    o = 3 * ATTN_WIDTH
    rr_ref[0] = p[:, o:o + RWKV_WIDTH]
    rk_ref[0] = p[:, o + RWKV_WIDTH:o + 2 * RWKV_WIDTH]
    rv_ref[0] = p[:, o + 2 * RWKV_WIDTH:o + 3 * RWKV_WIDTH]
    o += 3 * RWKV_WIDTH
    lo_ref[0] = p[:, o:o + LORA_PAD]


def _in_proj(xa, g1, modb, modc, w_in_p, cos_t, sin_t, n_lat, tm):
    b, ta, d = xa.shape
    n = w_in_p.shape[1]
    tok = lambda w: pl.BlockSpec((1, tm, w), lambda bb, i: (bb, i, 0))
    feat = pl.BlockSpec((1, ATTN_WIDTH, tm), lambda bb, i: (bb, 0, i))
    feat_shape = jax.ShapeDtypeStruct((b, ATTN_WIDTH, ta), BF16)
    vext = pl.BlockSpec((1, ATTN_HEADS * V_EXT, tm), lambda bb, i: (bb, 0, i))
    vext_shape = jax.ShapeDtypeStruct((b, ATTN_HEADS * V_EXT, ta), BF16)
    outs = [feat_shape, jax.ShapeDtypeStruct((b, ta, ATTN_WIDTH), BF16), vext_shape] \
        + [jax.ShapeDtypeStruct((b, ta, RWKV_WIDTH), F32)] * 3 \
        + [jax.ShapeDtypeStruct((b, ta, LORA_PAD), F32)]
    return pl.pallas_call(
        functools.partial(_inproj_kernel, tm=tm, n_lat=n_lat),
        grid=(b, ta // tm),
        in_specs=[tok(d), pl.BlockSpec((1, d), lambda bb, i: (0, 0))] + _mod_specs(d, 0, 1)
        + [pl.BlockSpec((d, n), lambda bb, i: (0, 0)),
           pl.BlockSpec((tm, LANES), lambda bb, i: (i, 0)),
           pl.BlockSpec((tm, LANES), lambda bb, i: (i, 0))],
        out_specs=[feat, tok(ATTN_WIDTH), vext] + [tok(RWKV_WIDTH)] * 3 + [tok(LORA_PAD)],
        out_shape=outs,
        compiler_params=_cparams(("parallel", "parallel")),
        name="norm1_in_proj",
    )(xa, g1, modb, modb, modc, modc, w_in_p, cos_t, sin_t)


def _attn_kernel(lam_ref, sg_ref, q_ref, k_ref, v_ref, o_ref, s_ref,
                 *, tq, k0, nkeys, tkk, ts, lam_init):
    q = q_ref[0]
    feat = lax.broadcasted_iota(jnp.int32, (LANES, tq), 0)
    zero = jnp.zeros_like(q)
    qs = jnp.concatenate([jnp.where(feat < ATTN_QK_DIM, q, zero),
                          jnp.where(feat >= ATTN_QK_DIM, q, zero)], axis=1)
    n_tiles = nkeys // tkk
    n_sub = tkk // ts

    def scores(t):
        mx = None
        for j in range(n_sub):
            r0 = k0 + t * tkk + j * ts
            s_j = jnp.dot(k_ref[0, r0:r0 + ts, :], qs, preferred_element_type=F32)
            s_ref[t % 2, j * ts:(j + 1) * ts, :] = s_j
            cm = jnp.max(s_j, axis=0, keepdims=True)
            mx = cm if mx is None else jnp.maximum(mx, cm)
        return mx

    m_run = jnp.full((1, 2 * tq), -jnp.inf, F32)
    acc = jnp.zeros((V_EXT, 2 * tq), F32)
    mx_next = scores(0)
    for t in range(n_tiles):
        mx_cur = mx_next
        if t + 1 < n_tiles:
            mx_next = scores(t + 1)
        m_new = jnp.maximum(m_run, mx_cur)
        alpha = jnp.exp2(m_run - m_new)
        pv = None
        for j in range(n_sub):
            p_j = jnp.exp2(s_ref[t % 2, j * ts:(j + 1) * ts, :] - m_new).astype(BF16)
            c0 = k0 + t * tkk + j * ts
            d = jnp.dot(v_ref[0, :, c0:c0 + ts], p_j, preferred_element_type=F32)
            pv = d if pv is None else pv + d
        acc = alpha * acc + pv
        m_run = m_new

    lp = lam_ref[...]
    lam = (jnp.exp(jnp.sum(lp[0:1] * lp[1:2], axis=1, keepdims=True))
           - jnp.exp(jnp.sum(lp[2:3] * lp[3:4], axis=1, keepdims=True)) + lam_init)
    l = acc[ATTN_V_DIM:ATTN_V_DIM + 1, :]
    o1 = acc[0:ATTN_V_DIM, 0:tq] / l[:, 0:tq]
    o2 = acc[0:ATTN_V_DIM, tq:2 * tq] / l[:, tq:2 * tq]
    o = o1 - lam * o2
    ms = jnp.mean(o * o, axis=0, keepdims=True)
    y = o * lax.rsqrt(ms + SUBLN_EPS) * sg_ref[...]
    o_ref[0] = (y * (1.0 - lam_init)).astype(BF16)


def _attention(q_t, k, v_t, lam_p, subln_g, lam_init, q0, nq, k0, nkeys, tq, tkk):
    b, ta, _ = k.shape
    qo = q0 // tq
    return pl.pallas_call(
        functools.partial(_attn_kernel, tq=tq, k0=k0, nkeys=nkeys, tkk=tkk, ts=_pick_tile(tkk, (256, LANES)),
                          lam_init=lam_init),
        grid=(b, ATTN_HEADS, nq // tq),
        in_specs=[pl.BlockSpec((4, ATTN_QK_DIM), lambda bb, h, i: (0, 0)),
                  pl.BlockSpec((ATTN_V_DIM, 1), lambda bb, h, i: (0, 0)),
                  pl.BlockSpec((1, LANES, tq), lambda bb, h, i: (bb, h, i + qo)),
                  pl.BlockSpec((1, ta, LANES), lambda bb, h, i: (bb, 0, h)),
                  pl.BlockSpec((1, V_EXT, ta), lambda bb, h, i: (bb, h, 0))],
        out_specs=pl.BlockSpec((1, ATTN_V_DIM, tq), lambda bb, h, i: (bb, h, i)),
        out_shape=jax.ShapeDtypeStruct((b, ATTN_WIDTH, nq), BF16),
        scratch_shapes=[pltpu.VMEM((2, tkk, 2 * tq), F32)],
        compiler_params=_cparams(("parallel", "parallel", "parallel")),
        name="diff_attention",
    )(lam_p, subln_g, q_t, k, v_t)


def _prep_kernel(*refs, tt, n_lat, ta, has_vres):
    (pr_ref, pk_ref, pv_ref, hpr_ref, hpk_ref, hpv_ref, hnr_ref, hnk_ref, hnv_ref, lo_ref,
     mu_ref, w0_ref, w2_ref, a0_ref, a2_ref, g2_ref, kk_ref, ka_ref, seg_ref) = refs[:19]
    n_in = 19
    if has_vres:
        vf_ref, v0_ref, v1_ref, v2_ref = refs[19:23]
        n_in = 23
    (r_out, v_out, kk_out, g_out, lwf_out, lwb_out, bf_out, bb_out, kdf_out, kdb_out) = refs[n_in:]

    i = pl.program_id(1)
    row = lax.broadcasted_iota(jnp.int32, (tt, 1), 0)
    gidx = i * tt + row
    no_prev = jnp.logical_or(gidx == 0, gidx == n_lat)
    no_next = jnp.logical_or(gidx == n_lat - 1, gidx == ta - 1)

    def shift_mix(p_ref, hp_ref, hn_ref, j):
        p = p_ref[0]
        pp = jnp.where(row == 0, hp_ref[0, 7:8, :], pltpu.roll(p, 1, 0))
        pp = jnp.where(no_prev, 0.0, pp)
        pn = jnp.where(row == tt - 1, hn_ref[0, 0:1, :], pltpu.roll(p, tt - 1, 0))
        pn = jnp.where(no_next, 0.0, pn)
        return p + mu_ref[2 * j:2 * j + 1, :] * (pp - p) + mu_ref[2 * j + 1:2 * j + 2, :] * (pn - p)

    r = shift_mix(pr_ref, hpr_ref, hnr_ref, 0)
    k = shift_mix(pk_ref, hpk_ref, hnk_ref, 1)
    v = shift_mix(pv_ref, hpv_ref, hnv_ref, 2)
    if has_vres:
        mix = jax.nn.sigmoid(v0_ref[...] + _bdot(_bdot(v, v1_ref[...]), v2_ref[...]))
        v = v + (vf_ref[0] - v) * mix
    lo = lo_ref[0]
    gd = lo[:, 2 * DECAY_LORA + 2 * AAA_LORA:2 * DECAY_LORA + 2 * AAA_LORA + GATE_LORA]
    g_out[0] = _bdot(jax.nn.sigmoid(gd), g2_ref[...])
    kk = k * kk_ref[...]
    ss = _dot_exact_rhs(kk * kk, seg_ref[...])
    kk = kk * lax.rsqrt(jnp.maximum(ss, 1e-24))
    r_out[0] = r
    v_out[0] = v
    kk_out[0] = kk
    lw_outs = (lwf_out, lwb_out)
    b_outs = (bf_out, bb_out)
    kd_outs = (kdf_out, kdb_out)
    for d in range(2):
        wd = lo[:, d * DECAY_LORA:(d + 1) * DECAY_LORA]
        wl = w0_ref[d:d + 1, :] + _bdot(jnp.tanh(wd), w2_ref[d])
        lw_outs[d][0] = -jax.nn.sigmoid(wl) * math.exp(-0.5)
        ad = lo[:, 2 * DECAY_LORA + d * AAA_LORA:2 * DECAY_LORA + (d + 1) * AAA_LORA]
        a = jax.nn.sigmoid(a0_ref[d:d + 1, :] + _bdot(ad, a2_ref[d]))
        b_outs[d][0] = a * kk
        kd_outs[d][0] = k * (1.0 + (a - 1.0) * ka_ref[...])


def _rwkv_prepare(pr, pk, pv, lora, mu, w0, w2, a0, a2, g2, k_k, k_a, seg, vres, n_lat, tt):
    b, ta, w = pr.shape
    nb8 = ta // 8
    tok = pl.BlockSpec((1, tt, w), lambda bb, i: (bb, i, 0))
    prev = pl.BlockSpec((1, 8, w), lambda bb, i: (bb, jnp.maximum(i * (tt // 8) - 1, 0), 0))
    nxt = pl.BlockSpec((1, 8, w), lambda bb, i: (bb, jnp.minimum((i + 1) * (tt // 8), nb8 - 1), 0))
    full = lambda a: pl.BlockSpec(a.shape, lambda bb, i, _n=a.ndim: (0,) * _n)
    consts = [mu.reshape(6, w), w0, w2, a0, a2, g2, k_k.reshape(1, w), k_a.reshape(1, w), seg]
    ins = [pr, pk, pv, pr, pk, pv, pr, pk, pv, lora] + consts
    specs = [tok] * 3 + [prev] * 3 + [nxt] * 3 \
        + [pl.BlockSpec((1, tt, LORA_PAD), lambda bb, i: (bb, i, 0))] + [full(a) for a in consts]
    if vres is not None:
        v_first, v0, v1, v2 = vres
        extra = [v0.reshape(1, w), v1, v2]
        ins += [v_first] + extra
        specs += [tok] + [full(a) for a in extra]
    return pl.pallas_call(
        functools.partial(_prep_kernel, tt=tt, n_lat=n_lat, ta=ta, has_vres=vres is not None),
        grid=(b, ta // tt),
        in_specs=specs,
        out_specs=[tok] * 10,
        out_shape=[jax.ShapeDtypeStruct((b, ta, w), F32)] * 10,
        compiler_params=_cparams(("parallel", "parallel")),
        name="rwkv_prepare",
    )(*ins)


_NN = (((2,), (1,)), ((0,), (0,)))
_NT = (((2,), (2,)), ((0,), (0,)))
_TN = (((1,), (1,)), ((0,), (0,)))


def _bmm(a, b, dims=_NN):
    return lax.dot_general(a.astype(BF16), b.astype(BF16), dims, preferred_element_type=F32)


def _scan_kernel(rf_ref, vf_ref, kkf_ref, lwf_ref, btf_ref, kdf_ref,
                 rb_ref, vb_ref, kkb_ref, lwb_ref, btb_ref, kdb_ref,
                 yf_ref, yb_ref, st_ref, *, nsub, nb):
    s = pl.program_id(0)
    c = CHUNK
    n = RWKV_HEAD
    g = nb * 2 * RWKV_HEADS

    @pl.when(s == 0)
    def _init():
        st_ref[...] = jnp.zeros(st_ref.shape, F32)

    t_i = lax.broadcasted_iota(jnp.int32, (c, c), 0)
    j_i = lax.broadcasted_iota(jnp.int32, (c, c), 1)
    t2 = lax.broadcasted_iota(jnp.int32, (2 * c, 2 * c), 0)
    j2 = lax.broadcasted_iota(jnp.int32, (2 * c, 2 * c), 1)
    tt2 = t2 & (c - 1)
    jj2 = j2 & (c - 1)
    eye = jnp.where(t_i == j_i, 1.0, 0.0)
    blk_bits = int(math.log2(INV_BLOCK))
    same_blk = (t_i >> blk_bits) == (j_i >> blk_bits)
    heads = lambda z: [z[:, h * n:(h + 1) * n] for h in range(RWKV_HEADS)]
    dirs = ((rf_ref, vf_ref, kkf_ref, lwf_ref, btf_ref, kdf_ref, False),
            (rb_ref, vb_ref, kkb_ref, lwb_ref, btb_ref, kdb_ref, True))
    at_l, rt_l, v_l, bk_l, pt_l, m_l = [], [], [], [], [], []
    for t in range(nsub):
        for bb, (r_ref, v_ref, kk_ref, lw_ref, bt_ref, kd_ref, rev) in itertools.product(range(nb), dirs):
            ci = nsub - 1 - t if rev else t
            rows = slice(ci * c, (ci + 1) * c)
            incl = (j_i >= t_i) if rev else (j_i <= t_i)
            tri = jnp.where(incl, 1.0, 0.0).astype(BF16)
            before = (jj2 > tt2) if rev else (jj2 < tt2)
            mask_m = jnp.logical_or(before, jnp.logical_and(t2 >= c, jj2 == tt2))
            lw = lw_ref[bb, rows, :]
            cs = _dot_exact_lhs(tri, lw)
            tot = cs[0:1, :] if rev else cs[c - 1:c, :]
            p_inv = jnp.exp(-cs)
            p_rest = jnp.exp(tot - cs)
            bt = bt_ref[bb, rows, :]
            kd = kd_ref[bb, rows, :]
            rt = heads(r_ref[bb, rows, :] * jnp.exp(cs))
            at = heads(-kk_ref[bb, rows, :] * jnp.exp(cs - lw))
            bti = heads(bt * p_inv)
            kti = heads(kd * p_inv)
            bp = heads(bt * p_rest)
            kp = heads(kd * p_rest)
            x1 = jnp.stack([jnp.concatenate([a, b], axis=0) for a, b in zip(at, rt)])
            x2 = jnp.stack([jnp.concatenate([a, b], axis=0) for a, b in zip(bti, kti)])
            m_l.append(jnp.where(mask_m[None], _bmm(x1, x2, _NT), 0.0))
            at_l += at
            rt_l += rt
            v_l += heads(v_ref[bb, rows, :])
            bk_l += [jnp.concatenate([a, b], axis=0) for a, b in zip(bp, kp)]
            pt_l += heads(jnp.exp(tot))
    m = jnp.concatenate(m_l, axis=0)
    at_s = jnp.stack(at_l)
    rt_s = jnp.stack(rt_l)
    v_s = jnp.stack(v_l)
    bk_s = jnp.stack(bk_l)
    pt_s = jnp.stack(pt_l)

    a_full = m[:, 0:c, 0:c]
    d_pow = jnp.where(same_blk[None], a_full, 0.0)
    l_off = a_full - d_pow
    t_diag = eye[None] + d_pow
    for _ in range(blk_bits - 1):
        d_pow = _bmm(d_pow, d_pow)
        t_diag = t_diag + _bmm(d_pow, t_diag)
    n_mat = _bmm(t_diag, l_off)
    x_mat = t_diag + _bmm(n_mat, t_diag)
    tinv = x_mat + _bmm(_bmm(n_mat, n_mat), x_mat)

    akv = _bmm(m[:, 0:c, c:2 * c], v_s)

    st = st_ref[...]
    for t in range(nsub):
        sl = slice(t * g, (t + 1) * g)
        u = _bmm(tinv[sl], akv[sl] + _bmm(at_s[sl], st, _NT))
        uv = jnp.concatenate([u, v_s[sl]], axis=1)
        y = _bmm(m[sl, c:2 * c, :], uv) + _bmm(rt_s[sl], st, _NT)
        st = pt_s[sl] * st + _bmm(uv, bk_s[sl], _TN)
        rf = slice(t * c, (t + 1) * c)
        rb = slice((nsub - 1 - t) * c, (nsub - t) * c)
        for bb, h in itertools.product(range(nb), range(RWKV_HEADS)):
            yf_ref[bb, rf, h * n:(h + 1) * n] = y[bb * 2 * RWKV_HEADS + h]
            yb_ref[bb, rb, h * n:(h + 1) * n] = y[bb * 2 * RWKV_HEADS + RWKV_HEADS + h]
    st_ref[...] = st


def _rwkv_scan(r, v, kk, lw_f, lw_b, bt_f, bt_b, kd_f, kd_b, n_lat):
    b, ta, w = r.shape
    nsub = SCAN_CHUNKS
    rows = nsub * CHUNK
    assert n_lat % rows == 0 and ta % rows == 0
    nc = ta // rows
    nlat = n_lat // rows
    nctx = nc - nlat

    def fwd_idx(s):
        return jnp.where(s < nctx, nlat + s, s - nctx)

    def rev_idx(s):
        return nc - 1 - s

    fwd = pl.BlockSpec((b, rows, w), lambda s: (0, fwd_idx(s), 0))
    bwd = pl.BlockSpec((b, rows, w), lambda s: (0, rev_idx(s), 0))
    return pl.pallas_call(
        functools.partial(_scan_kernel, nsub=nsub, nb=b),
        grid=(nc,),
        in_specs=[fwd] * 6 + [bwd] * 6,
        out_specs=[fwd, bwd],
        out_shape=[jax.ShapeDtypeStruct((b, ta, w), F32)] * 2,
        scratch_shapes=[pltpu.VMEM((b * 2 * RWKV_HEADS, RWKV_HEAD, RWKV_HEAD), F32)],
        compiler_params=_cparams(("arbitrary",)),
        name="rwkv_scan",
    )(r, v, kk, lw_f, bt_f, kd_f, r, v, kk, lw_b, bt_b, kd_b)


def _postmix_kernel(x_ref, att_ref, yf_ref, yb_ref, r_ref, v_ref, kdf_ref, kdb_ref, g_ref,
                    bgt_ref, cgt_ref, wo_ref, lnw_ref, lnb_ref, rk_ref, seg_ref, o_ref, *, tm, n_lat):
    i = pl.program_id(1)
    is_ctx = _is_ctx_rows(i, tm, n_lat)
    seg = seg_ref[...]
    y = yf_ref[0] + yb_ref[0]
    inv_n = 1.0 / RWKV_HEAD
    mean = _dot_exact_rhs(y, seg) * inv_n
    dy = y - mean
    var = _dot_exact_rhs(dy * dy, seg) * inv_n
    o = dy * lax.rsqrt(var + LNX_EPS) * lnw_ref[...] + lnb_ref[...]
    r = r_ref[0]
    v = v_ref[0]
    rrk = r * rk_ref[...]
    o = o + _dot_exact_rhs(rrk * kdf_ref[0], seg) * v
    o = o + _dot_exact_rhs(rrk * kdb_ref[0], seg) * v
    rw = o * g_ref[0]
    mixed = (lax.dot_general(att_ref[0], wo_ref[0:ATTN_WIDTH, :], (((0,), (0,)), ((), ())),
                             preferred_element_type=F32)
             + jnp.dot(rw.astype(BF16), wo_ref[ATTN_WIDTH:, :], preferred_element_type=F32))
    gt = jnp.where(is_ctx, cgt_ref[...], bgt_ref[0])
    o_ref[0] = x_ref[0] + gt * mixed


def _post_mix(xa, att, y_f, y_b, r, v, kd_f, kd_b, g, modb, modc, w_out, lnw, lnb, r_k, seg, n_lat, tm):
    b, ta, d = xa.shape
    w = RWKV_WIDTH
    tok = lambda ww: pl.BlockSpec((1, tm, ww), lambda bb, i: (bb, i, 0))
    full = lambda a: pl.BlockSpec(a.shape, lambda bb, i, _n=a.ndim: (0,) * _n)
    consts = [w_out, lnw.reshape(1, w), lnb.reshape(1, w), r_k.reshape(1, w), seg]
    return pl.pallas_call(
        functools.partial(_postmix_kernel, tm=tm, n_lat=n_lat),
        grid=(b, ta // tm),
        in_specs=[tok(d), pl.BlockSpec((1, ATTN_WIDTH, tm), lambda bb, i: (bb, 0, i))] + [tok(w)] * 7
        + [pl.BlockSpec((1, 1, d), lambda bb, i: (bb, 0, 2)), pl.BlockSpec((1, d), lambda bb, i: (0, 2))]
        + [full(a) for a in consts],
        out_specs=tok(d),
        out_shape=jax.ShapeDtypeStruct((b, ta, d), F32),
        compiler_params=_cparams(("parallel", "parallel")),
        name="rwkv_finish_out_proj",
    )(xa, att, y_f, y_b, r, v, kd_f, kd_b, g, modb, modc, *consts)


def _ffn_kernel(x_ref, g_ref, bsh_ref, bsc_ref, csh_ref, csc_ref, bgt_ref, cgt_ref,
                w1_ref, w3_ref, w2_ref, o_ref, *, tm, n_lat):
    i = pl.program_id(1)
    is_ctx = _is_ctx_rows(i, tm, n_lat)
    x = x_ref[0]
    h = _modnorm(x, g_ref[...], bsc_ref[0], bsh_ref[0], csc_ref[...], csh_ref[...], is_ctx).astype(BF16)
    a = jnp.dot(h, w1_ref[...], preferred_element_type=F32)
    bb = jnp.dot(h, w3_ref[...], preferred_element_type=F32)
    act = (a * jax.nn.sigmoid(a) * bb).astype(BF16)
    out = jnp.dot(act, w2_ref[...], preferred_element_type=F32)
    gt = jnp.where(is_ctx, cgt_ref[...], bgt_ref[0])
    o_ref[0] = x + gt * out


def _dense_ffn(xa, g2, modb, modc, w1, w3, w2, n_lat, tm):
    b, ta, d = xa.shape
    tok = pl.BlockSpec((1, tm, d), lambda bb, i: (bb, i, 0))
    full = lambda a: pl.BlockSpec(a.shape, lambda bb, i, _n=a.ndim: (0,) * _n,
                                  pipeline_mode=pl.Buffered(1))
    return pl.pallas_call(
        functools.partial(_ffn_kernel, tm=tm, n_lat=n_lat),
        grid=(b, ta // tm),
        in_specs=[tok, pl.BlockSpec((1, d), lambda bb, i: (0, 0))] + _mod_specs(d, 3, 4)
        + [pl.BlockSpec((1, 1, d), lambda bb, i: (bb, 0, 5)), pl.BlockSpec((1, d), lambda bb, i: (0, 5))]
        + [full(w1), full(w3), full(w2)],
        out_specs=tok,
        out_shape=jax.ShapeDtypeStruct((b, ta, d), F32),
        compiler_params=_cparams(("parallel", "parallel")),
        name="norm2_dense_swiglu",
    )(xa, g2, modb, modb, modc, modc, modb, modc, w1, w3, w2)


_NT2 = (((1,), (1,)), ((), ()))
_TN2 = (((0,), (0,)), ((), ()))


def _route_kernel(x_ref, g_ref, bsh_ref, bsc_ref, csh_ref, csc_ref, rw_ref, tri_ref,
                  h_ref, gate_ref, pos_ref, cnt_ref, *, tm, n_lat):
    i = pl.program_id(1)
    is_ctx = _is_ctx_rows(i, tm, n_lat)
    h = _modnorm(x_ref[0], g_ref[...], bsc_ref[0], bsh_ref[0], csc_ref[...], csh_ref[...], is_ctx)
    h_ref[0] = h.astype(BF16)
    hh, hl = _split(h)
    wh, wl = _split(rw_ref[...])
    dg = lambda a, b: lax.dot_general(a, b, _NT2, preferred_element_type=F32)
    logits = dg(wh, hh) + dg(wh, hl) + dg(wl, hh)
    eidx = lax.broadcasted_iota(jnp.int32, logits.shape, 0).astype(F32)
    neg = -jnp.inf
    v1 = jnp.max(logits, axis=0, keepdims=True)
    i1 = jnp.min(jnp.where(logits == v1, eidx, float(N_EXPERTS)), axis=0, keepdims=True)
    m1 = eidx == i1
    rest = jnp.where(m1, neg, logits)
    v2 = jnp.max(rest, axis=0, keepdims=True)
    i2 = jnp.min(jnp.where(rest == v2, eidx, float(N_EXPERTS)), axis=0, keepdims=True)
    m2 = eidx == i2
    e2 = jnp.exp(v2 - v1)
    den = 1.0 + e2
    gate_ref[0] = jnp.where(m1, 1.0 / den, 0.0) + jnp.where(m2, e2 / den, 0.0)
    sel = jnp.logical_or(m1, m2)
    self = jnp.where(sel, 1.0, 0.0)
    rank = jnp.dot(self.astype(BF16), tri_ref[...], preferred_element_type=F32)
    pos_ref[0] = jnp.where(sel, rank, -1.0)
    cnt = jnp.sum(self, axis=1, keepdims=True)
    cnt_ref[0, 0] = jnp.broadcast_to(cnt, (N_EXPERTS, LANES)).astype(jnp.int32)


def _expert_kernel(cnt_ref, x_ref, h_ref, gate_ref, pos_ref, bgt_ref, cgt_ref, w1_ref, w3_ref, w2_ref,
                   o_ref, xg_ref, gs_ref, og_ref, *, tm, n_lat, blk):
    bi = pl.program_id(0)
    ti = pl.program_id(1)
    e = pl.program_id(2)
    f = pl.program_id(3)
    nt = pl.num_programs(1)
    ne = pl.num_programs(2)
    nf = pl.num_programs(3)
    cnt = cnt_ref[(bi * nt + ti) * ne + e]
    nblk = (cnt + blk - 1) // blk
    pos_row = pos_ref[0, pl.ds(e, 1), :]
    gate_row = gate_ref[0, pl.ds(e, 1), :]
    slot = lax.broadcasted_iota(jnp.int32, (blk, tm), 0).astype(F32)

    def one_hot(j):
        return pos_row == slot + (j * blk).astype(F32)

    def rows_of(j):
        return pl.ds(pl.multiple_of(j * blk, blk), blk)

    nblk2 = (cnt + 2 * blk - 1) // (2 * blk)
    slot2 = lax.broadcasted_iota(jnp.int32, (2 * blk, tm), 0).astype(F32)

    def rows2_of(j):
        return pl.ds(pl.multiple_of(j * 2 * blk, 2 * blk), 2 * blk)

    @pl.when(jnp.logical_and(e == 0, f == 0))
    def _zero():
        o_ref[0] = jnp.zeros(o_ref.shape[1:], F32)

    @pl.when(f == 0)
    def _gather():
        def body(j, carry):
            hit = one_hot(j)
            g = jnp.where(hit, 1.0, 0.0).astype(BF16)
            xg_ref[rows_of(j), :] = jnp.dot(g, h_ref[0], preferred_element_type=F32).astype(BF16)
            gsl = jnp.sum(jnp.where(hit, gate_row, 0.0), axis=1, keepdims=True)
            gs_ref[rows_of(j), :] = jnp.broadcast_to(gsl, (blk, LANES))
            return carry
        lax.fori_loop(0, nblk, body, 0)

        def clear(j, carry):
            og_ref[rows2_of(j), :] = jnp.zeros((2 * blk, og_ref.shape[1]), F32)
            return carry
        lax.fori_loop(0, nblk2, clear, 0)

    def expert(j, carry):
        xb = xg_ref[rows_of(j), :]
        a = jnp.dot(xb, w1_ref[0], preferred_element_type=F32)
        bb = jnp.dot(xb, w3_ref[0], preferred_element_type=F32)
        act = (a * jax.nn.sigmoid(a) * bb * gs_ref[rows_of(j), 0:1]).astype(BF16)
        og_ref[rows_of(j), :] += jnp.dot(act, w2_ref[0], preferred_element_type=F32)
        return carry
    lax.fori_loop(0, nblk, expert, 0)

    @pl.when(f == nf - 1)
    def _scatter():
        d = o_ref.shape[2]
        def body(j, carry):
            hit = pos_row == slot2 + (j * 2 * blk).astype(F32)
            g = jnp.where(hit, 1.0, 0.0).astype(BF16)
            g_t = jnp.transpose(g)
            for c0 in range(0, d, d // 2):
                o = og_ref[rows2_of(j), c0:c0 + d // 2].astype(BF16)
                o_ref[0, :, c0:c0 + d // 2] += jnp.dot(g_t, o, preferred_element_type=F32)
            return carry
        lax.fori_loop(0, nblk2, body, 0)

    @pl.when(jnp.logical_and(e == ne - 1, f == nf - 1))
    def _finish():
        is_ctx = _is_ctx_rows(ti, tm, n_lat)
        gt = jnp.where(is_ctx, cgt_ref[...], bgt_ref[0])
        o_ref[0] = x_ref[0] + gt * o_ref[0]


def _moe_ffn(xa, g2, modb, modc, router_w, w1, w3, w2, n_lat, tm, nf):
    b, ta, d = xa.shape
    ne, _, ff = w1.shape
    tf = ff // nf
    nt = ta // tm
    assert tm % (2 * MOE_SLOT_BLOCK) == 0
    tok = pl.BlockSpec((1, tm, d), lambda bb, i, *_: (bb, i, 0))
    emaj = pl.BlockSpec((1, ne, tm), lambda bb, i, *_: (bb, 0, i))
    tri = jnp.triu(jnp.ones((tm, tm), BF16), 1)
    h, gate, pos, cnt = pl.pallas_call(
        functools.partial(_route_kernel, tm=tm, n_lat=n_lat),
        grid=(b, nt),
        in_specs=[tok, pl.BlockSpec((1, d), lambda bb, i: (0, 0))] + _mod_specs(d, 3, 4)
        + [pl.BlockSpec((ne, d), lambda bb, i: (0, 0)), pl.BlockSpec((tm, tm), lambda bb, i: (0, 0))],
        out_specs=[tok, emaj, emaj, pl.BlockSpec((1, 1, ne, LANES), lambda bb, i: (bb, i, 0, 0))],
        out_shape=[jax.ShapeDtypeStruct((b, ta, d), BF16), jax.ShapeDtypeStruct((b, ne, ta), F32),
                   jax.ShapeDtypeStruct((b, ne, ta), F32), jax.ShapeDtypeStruct((b, nt, ne, LANES), jnp.int32)],
        compiler_params=_cparams(("parallel", "parallel")),
        name="norm2_moe_route",
    )(xa, g2, modb, modb, modc, modc, router_w.T, tri)
    counts = cnt[:, :, :, 0].reshape(-1)
    once = dict(pipeline_mode=pl.Buffered(1))
    grid_spec = pltpu.PrefetchScalarGridSpec(
        num_scalar_prefetch=1,
        grid=(b, nt, ne, nf),
        in_specs=[pl.BlockSpec((1, tm, d), lambda bb, i, e, f, c: (bb, i, 0), **once),
                  pl.BlockSpec((1, tm, d), lambda bb, i, e, f, c: (bb, i, 0), **once),
                  emaj, emaj,
                  pl.BlockSpec((1, 1, d), lambda bb, i, e, f, c: (bb, 0, 5)),
                  pl.BlockSpec((1, d), lambda bb, i, e, f, c: (0, 5)),
                  pl.BlockSpec((1, d, tf), lambda bb, i, e, f, c: (e, 0, f)),
                  pl.BlockSpec((1, d, tf), lambda bb, i, e, f, c: (e, 0, f)),
                  pl.BlockSpec((1, tf, d), lambda bb, i, e, f, c: (e, f, 0))],
        out_specs=pl.BlockSpec((1, tm, d), lambda bb, i, e, f, c: (bb, i, 0)),
        scratch_shapes=[pltpu.VMEM((tm, d), BF16), pltpu.VMEM((tm, LANES), F32), pltpu.VMEM((tm, d), F32)])
    return pl.pallas_call(
        functools.partial(_expert_kernel, tm=tm, n_lat=n_lat, blk=MOE_SLOT_BLOCK),
        grid_spec=grid_spec,
        out_shape=jax.ShapeDtypeStruct((b, ta, d), F32),
        compiler_params=_cparams(("parallel", "parallel", "arbitrary", "arbitrary")),
        name="moe_experts",
    )(counts, xa, h, gate, pos, modb, modc, w1, w3, w2)


def _final_kernel(x_ref, g_ref, o_ref):
    x = x_ref[0]
    ms = jnp.mean(x * x, axis=-1, keepdims=True)
    o_ref[0] = x * lax.rsqrt(ms + NORM_EPS) * g_ref[...]


def _final_norm(xa, g, seq, tm):
    b, _, d = xa.shape
    return pl.pallas_call(
        _final_kernel,
        grid=(b, seq // tm),
        in_specs=[pl.BlockSpec((1, tm, d), lambda bb, i: (bb, i, 0)),
                  pl.BlockSpec((1, d), lambda bb, i: (0, 0))],
        out_specs=pl.BlockSpec((1, tm, d), lambda bb, i: (bb, i, 0)),
        out_shape=jax.ShapeDtypeStruct((b, seq, d), F32),
        compiler_params=_cparams(("parallel", "parallel")),
        name="final_norm",
    )(xa, g)


def _rope_tables(seq, ctx_len):
    axis_dim = ATTN_QK_DIM // 2
    inv_freq = ROPE_THETA ** (-jnp.arange(0, axis_dim, 2, dtype=F32) / axis_dim)
    t = jnp.arange(seq)
    pos = jnp.stack([(t // GRID_W).astype(F32), (t % GRID_W).astype(F32)], axis=1)
    lane = np.arange(LANES)
    axis = (lane % ATTN_QK_DIM) // axis_dim
    freq = lane % (axis_dim // 2)
    sign = np.where((lane % axis_dim) < axis_dim // 2, -1.0, 1.0).astype(np.float32)
    ang = pos[:, axis] * inv_freq[freq][None, :]
    cos_t = jnp.concatenate([jnp.cos(ang), jnp.ones((ctx_len, LANES), F32)], axis=0)
    sin_t = jnp.concatenate([jnp.sin(ang) * sign[None, :], jnp.zeros((ctx_len, LANES), F32)], axis=0)
    return cos_t, sin_t


def _pick_tile(n, candidates):
    for c in candidates:
        if n % c == 0:
            return c
    raise ValueError(f"no tile in {candidates} divides {n}")


def kernel(x, c, ctx, c_ctx, ada_w, ada_b, norm1_g, norm2_g, w_in, w_out, diff_lambda, subln_g, rwkv_mu, rwkv_w0, rwkv_w2, rwkv_a0, rwkv_a2, rwkv_v0, rwkv_v1, rwkv_v2, rwkv_g2, rwkv_k_k, rwkv_k_a, rwkv_r_k, lnx_w, lnx_b, ffn_w1, ffn_w3, ffn_w2, router_w, exp_w1, exp_w3, exp_w2, final_g):
    b, seq, d = x.shape
    n_lat = seq
    ctx_len = ctx.shape[1]
    depth = ada_w.shape[0]
    ta = n_lat + ctx_len
    assert b + 1 <= 8 and seq % GRID_W == 0 and ctx_len % CHUNK == 0 and seq % CHUNK == 0

    tq = _pick_tile(seq, (512, 256, 128))
    tkk = _pick_tile(ta, (3328, 1280, 640, 256, 128))
    tm_moe = _pick_tile(ta, (1280, 640, 256, 128))
    tm_row = _pick_tile(ta, (640, 256, 128))

    xa = jnp.concatenate([x, ctx], axis=1)
    cvec = jnp.zeros((8, d), F32).at[:b].set(c).at[b].set(c_ctx)
    mod = _modulation(cvec, ada_w, ada_b)
    cos_t, sin_t = _rope_tables(seq, ctx_len)
    seg = jnp.asarray(np.kron(np.eye(RWKV_HEADS), np.ones((RWKV_HEAD, RWKV_HEAD))), BF16)
    n_in = w_in.shape[2]
    n_in_p = 3 * ATTN_WIDTH + 3 * RWKV_WIDTH + LORA_PAD

    v_first = None
    for l in range(depth):
        lam_init = 0.8 - 0.6 * math.exp(-0.3 * l)
        modb = mod[l, :b].reshape(b, 1, 6 * d)
        modc = mod[l, b:b + 1]
        w_in_p = jnp.pad(w_in[l], ((0, 0), (0, n_in_p - n_in))).astype(BF16)
        q, k, v, pr, pk, pv, lora = _in_proj(xa, norm1_g[l].reshape(1, d), modb, modc, w_in_p,
                                             cos_t, sin_t, n_lat, tm_row)
        sg = subln_g[l].reshape(ATTN_V_DIM, 1)
        att_lat = _attention(q, k, v, diff_lambda[l], sg, lam_init, 0, seq, 0, ta, tq, tkk)
        att_ctx = _attention(q, k, v, diff_lambda[l], sg, lam_init, seq, ctx_len, seq, ctx_len,
                             ctx_len, ctx_len)
        att = jnp.concatenate([att_lat, att_ctx], axis=2)
        vres = None if l == 0 else (v_first, rwkv_v0[l - 1], rwkv_v1[l - 1], rwkv_v2[l - 1])
        r, vv, kk, g, lw_f, lw_b, bt_f, bt_b, kd_f, kd_b = _rwkv_prepare(
            pr, pk, pv, lora, rwkv_mu[l], rwkv_w0[l], rwkv_w2[l], rwkv_a0[l], rwkv_a2[l], rwkv_g2[l],
            rwkv_k_k[l], rwkv_k_a[l], seg, vres, n_lat, tm_row)
        if l == 0:
            v_first = vv
        y_f, y_b = _rwkv_scan(r, vv, kk, lw_f, lw_b, bt_f, bt_b, kd_f, kd_b, n_lat)
        xa = _post_mix(xa, att, y_f, y_b, r, vv, kd_f, kd_b, g, modb, modc, w_out[l].astype(BF16),
                       lnx_w[l], lnx_b[l], rwkv_r_k[l], seg, n_lat, tm_row)
        i = l // 2
        if l % 2 == 0:
            xa = _dense_ffn(xa, norm2_g[l].reshape(1, d), modb, modc, ffn_w1[i].astype(BF16),
                            ffn_w3[i].astype(BF16), ffn_w2[i].astype(BF16), n_lat, tm_row)
        else:
            xa = _moe_ffn(xa, norm2_g[l].reshape(1, d), modb, modc, router_w[i],
                          exp_w1[i].astype(BF16), exp_w3[i].astype(BF16), exp_w2[i].astype(BF16),
                          n_lat, tm_moe, 2)
    return _final_norm(xa, final_g.reshape(1, d), seq, _pick_tile(seq, (1024, 512, 256, 128)))
```
